```python
import math
import jax, jax.numpy as jnp
from jax import lax
import numpy as np

D_MODEL = 1024
BATCH = 2
SEQ = 16384
DEPTH = 1
DEC_BATCH = 8
DEC_SEQ = 8192
PAST_LEN = 128

HEAD_DIM = 64
ATTN_HEADS = 8
ATTN_KV_HEADS = 2
ATTN_GROUP = ATTN_HEADS // ATTN_KV_HEADS
ATTN_WIDTH = ATTN_HEADS * HEAD_DIM
ATTN_KV_WIDTH = ATTN_KV_HEADS * HEAD_DIM
WINDOW = 128
BLOCK = 128
REL_BUCKETS = 32
REL_MAX_DIST = 128
DN_HEADS = 4
DN_KEY_DIM = 64
DN_VAL_DIM = 64
DN_QK_WIDTH = DN_HEADS * DN_KEY_DIM
DN_WIDTH = DN_HEADS * DN_VAL_DIM
DN_CONV = 5
DN_CONV_DIM = 2 * DN_QK_WIDTH + DN_WIDTH
DN_CHUNK = 64
MEM_TOKENS = 256
MEM_HEADS = 4
MEM_WIDTH = MEM_HEADS * HEAD_DIM
MIX_WIDTH = ATTN_WIDTH + DN_WIDTH + MEM_WIDTH
IN_SPLITS = (ATTN_WIDTH, ATTN_KV_WIDTH, ATTN_KV_WIDTH, ATTN_WIDTH,
             DN_QK_WIDTH, DN_QK_WIDTH, DN_WIDTH, DN_WIDTH, 2 * DN_HEADS, 2 * DN_HEADS,
             MEM_WIDTH, MEM_WIDTH)
IN_WIDTH = 2 * ATTN_WIDTH + 2 * ATTN_KV_WIDTH + 2 * DN_QK_WIDTH + 2 * DN_WIDTH + 4 * DN_HEADS + 2 * MEM_WIDTH
DEEPNORM_ALPHA = (2 * DEPTH) ** 0.25
DEEPNORM_BETA = (8 * DEPTH) ** -0.25
LN_EPS = 1e-5
RMS_EPS = 1e-6
f32 = jnp.float32

kernel_name = 'hymba_bidir_window_gdn_mem_encoder'


def layer_norm(x, g, b):
    xf = x.astype(f32)
    mu = xf.mean(-1, keepdims=True)
    var = jnp.square(xf - mu).mean(-1, keepdims=True)
    return ((xf - mu) * lax.rsqrt(var + LN_EPS) * g.astype(f32) + b.astype(f32)).astype(x.dtype)


def t5_bucket(rel):
    nb = REL_BUCKETS // 2
    max_exact = nb // 2
    n = jnp.abs(rel)
    large = max_exact + (jnp.log(jnp.maximum(n, 1).astype(f32) / max_exact)
                         / math.log(REL_MAX_DIST / max_exact) * (nb - max_exact)).astype(jnp.int32)
    large = jnp.minimum(large, nb - 1)
    return jnp.where(rel > 0, nb, 0) + jnp.where(n < max_exact, n, large)


def window_attention(q, k, v, sink, rel_bias):
    B, L = q.shape[:2]
    nb = L // BLOCK
    qb = q.reshape(B, nb, BLOCK, ATTN_KV_HEADS, ATTN_GROUP, HEAD_DIM)

    def band(t):
        tp = jnp.pad(t, ((0, 0), (BLOCK, BLOCK), (0, 0), (0, 0)))
        tp = tp.reshape(B, nb + 2, BLOCK, ATTN_KV_HEADS, HEAD_DIM)
        return jnp.concatenate([tp[:, :-2], tp[:, 1:-1], tp[:, 2:]], axis=2)

    kb, vb = band(k), band(v)
    t = jnp.arange(BLOCK)[:, None]
    s = jnp.arange(3 * BLOCK)[None, :]
    rel = s - BLOCK - t
    bias = rel_bias[t5_bucket(rel)].astype(f32)
    bias = bias.transpose(2, 0, 1).reshape(ATTN_KV_HEADS, ATTN_GROUP, BLOCK, 3 * BLOCK)
    kpos = jnp.arange(nb)[:, None] * BLOCK - BLOCK + s
    valid = (jnp.abs(rel) <= WINDOW)[None] & ((kpos >= 0) & (kpos < L))[:, None, :]
    logits = jnp.einsum('bnqkgd,bnskd->bnkgqs', qb, kb, preferred_element_type=f32) * HEAD_DIM ** -0.5 + bias
    logits = jnp.where(valid[None, :, None, None], logits, -jnp.inf)
    sink_l = sink.astype(f32).reshape(1, 1, ATTN_KV_HEADS, ATTN_GROUP, 1)
    m = jnp.maximum(logits.max(-1), sink_l)
    p = jnp.exp(logits - m[..., None])
    denom = p.sum(-1) + jnp.exp(sink_l - m)
    p = (p / denom[..., None]).astype(v.dtype)
    o = jnp.einsum('bnkgqs,bnskd->bnqkgd', p, vb)
    return o.reshape(B, L, ATTN_WIDTH)


def short_conv(x, w):
    return lax.conv_general_dilated(x, w[:, None, :].astype(x.dtype), window_strides=(1,),
                                    padding=[(DN_CONV // 2, DN_CONV // 2)],
                                    dimension_numbers=('NWC', 'WIO', 'NWC'),
                                    feature_group_count=x.shape[-1])


def l2norm(t):
    return t * lax.rsqrt(jnp.sum(t * t, -1, keepdims=True) + 1e-6)


def gated_delta_chunked(q, k, v, g, beta):
    lead = q.shape[:-2]
    L = q.shape[-2]
    n, C = L // DN_CHUNK, DN_CHUNK
    ax = len(lead)

    def chunks(t):
        return t.reshape(*lead, n, C, *t.shape[ax + 1:])

    q, k, v, g, beta = (chunks(t) for t in (q, k, v, g, beta))
    g = jnp.cumsum(g, axis=-1)
    causal = jnp.tril(jnp.ones((C, C), bool))
    decay = jnp.exp(jnp.where(causal, g[..., :, None] - g[..., None, :], -jnp.inf))
    k_beta = k * beta[..., None]
    a = jnp.einsum('...id,...jd->...ij', k_beta, k) * decay

    def solve(rhs):
        return lax.linalg.triangular_solve(a, rhs, left_side=True, lower=True, unit_diagonal=True)

    u = solve(v * beta[..., None])
    w = solve(k_beta * jnp.exp(g)[..., None])
    qk = jnp.einsum('...id,...jd->...ij', q, k) * decay
    q_dec = q * jnp.exp(g)[..., None]
    k_dec = k * jnp.exp(g[..., -1:] - g)[..., None]
    g_tot = jnp.exp(g[..., -1])

    def step(S, xs):
        u_c, w_c, qk_c, qd_c, kd_c, gt_c = xs
        v_new = u_c - jnp.einsum('...cd,...de->...ce', w_c, S)
        o_c = jnp.einsum('...cd,...de->...ce', qd_c, S) + jnp.einsum('...ij,...je->...ie', qk_c, v_new)
        S = S * gt_c[..., None, None] + jnp.einsum('...cd,...ce->...de', kd_c, v_new)
        return S, o_c

    xs = tuple(jnp.moveaxis(t, ax, 0) for t in (u, w, qk, q_dec, k_dec, g_tot))
    S0 = jnp.zeros((*lead, q.shape[-1], v.shape[-1]), f32)
    _, o = lax.scan(step, S0, xs)
    return jnp.moveaxis(o, 0, ax).reshape(*lead, L, v.shape[-1])


def deltanet_branch(q, k, v, a, b, conv_w, A_log, dt_bias, norm_g):
    B, L = q.shape[:2]
    qkv = jax.nn.silu(short_conv(jnp.concatenate([q, k, v], -1), conv_w)).astype(f32)
    q, k, v = jnp.split(qkv, [DN_QK_WIDTH, 2 * DN_QK_WIDTH], axis=-1)

    def heads(t, d):
        return t.reshape(B, L, DN_HEADS, d).transpose(0, 2, 1, 3)

    q = l2norm(heads(q, DN_KEY_DIM)) * DN_KEY_DIM ** -0.5
    k = l2norm(heads(k, DN_KEY_DIM))
    v = heads(v, DN_VAL_DIM)
    a = a.astype(f32).reshape(B, L, 2, DN_HEADS)
    b = b.astype(f32).reshape(B, L, 2, DN_HEADS)
    g = -jnp.exp(A_log.astype(f32)) * jax.nn.softplus(a + dt_bias.astype(f32))
    beta = jax.nn.sigmoid(b)
    g, beta = g.transpose(2, 0, 3, 1), beta.transpose(2, 0, 3, 1)
    qs = jnp.stack([q, q[..., ::-1, :]])
    ks = jnp.stack([k, k[..., ::-1, :]])
    vs = jnp.stack([v, v[..., ::-1, :]])
    gs = jnp.stack([g[0], g[1][..., ::-1]])
    bs = jnp.stack([beta[0], beta[1][..., ::-1]])
    o = gated_delta_chunked(qs, ks, vs, gs, bs)
    o = o[0] + o[1][..., ::-1, :]
    o = o * lax.rsqrt(jnp.mean(o * o, -1, keepdims=True) + RMS_EPS) * norm_g.astype(f32)
    return o.transpose(0, 2, 1, 3).reshape(B, L, DN_WIDTH)


def memory_attention(q, mem_kv):
    B, L = q.shape[:2]
    M = mem_kv.shape[1]
    q = q.reshape(B, L, MEM_HEADS, HEAD_DIM)
    k, v = jnp.split(mem_kv, 2, axis=-1)
    k = k.reshape(B, M, MEM_HEADS, HEAD_DIM)
    v = v.reshape(B, M, MEM_HEADS, HEAD_DIM)
    s = jnp.einsum('blhd,bmhd->bhlm', q, k, preferred_element_type=f32) * HEAD_DIM ** -0.5
    p = jax.nn.softmax(s, axis=-1).astype(v.dtype)
    return jnp.einsum('bhlm,bmhd->blhd', p, v).reshape(B, L, MEM_WIDTH)


def mixer_layer(x, mem, w_in, attn_sink, rel_bias, dn_conv, dn_A_log, dn_dt_bias, dn_norm_g, w_mem_kv, w_out):
    B, L, _ = x.shape
    h = x @ w_in
    offs = np.cumsum(IN_SPLITS)[:-1].tolist()
    (aq, ak, av, az, dn_q, dn_k, dn_v, dn_z, dn_a, dn_b, mq, mz) = jnp.split(h, offs, axis=-1)
    y_attn = window_attention(aq.reshape(B, L, ATTN_HEADS, HEAD_DIM),
                              ak.reshape(B, L, ATTN_KV_HEADS, HEAD_DIM),
                              av.reshape(B, L, ATTN_KV_HEADS, HEAD_DIM),
                              attn_sink, rel_bias) * jax.nn.silu(az)
    y_dn = deltanet_branch(dn_q, dn_k, dn_v, dn_a, dn_b, dn_conv, dn_A_log, dn_dt_bias,
                           dn_norm_g).astype(x.dtype) * jax.nn.silu(dn_z)
    y_mem = memory_attention(mq, mem @ w_mem_kv) * jax.nn.silu(mz)
    return jnp.concatenate([y_attn, y_dn, y_mem], axis=-1) @ w_out


def encoder_trunk(x, mem, ln_in_g, ln_in_b, rel_bias, w_in, attn_sink, dn_conv, dn_A_log, dn_dt_bias,
                  dn_norm_g, w_mem_kv, w_out, ln_g, ln_b):
    x = layer_norm(x, ln_in_g, ln_in_b)
    mem = layer_norm(mem, ln_in_g, ln_in_b)
    for l in range(DEPTH):
        y = mixer_layer(x, mem, w_in[l], attn_sink[l], rel_bias, dn_conv[l], dn_A_log[l], dn_dt_bias[l],
                        dn_norm_g[l], w_mem_kv[l], w_out[l])
        x = layer_norm(DEEPNORM_ALPHA * x + y, ln_g[l], ln_b[l])
    return x


def setup_inputs(seed: int = 0) -> dict:
    key = jax.random.key(seed)
    ks = jax.random.split(key, 20)
    nrm = jax.random.normal
    dt = jnp.exp(jax.random.uniform(ks[12], (DEPTH, 2, DN_HEADS), minval=math.log(1e-3), maxval=math.log(1e-1)))
    return {
        'x_prompt': nrm(ks[0], (BATCH, SEQ, D_MODEL), f32),
        'x_sample': nrm(ks[1], (DEC_BATCH, DEC_SEQ, D_MODEL), f32),
        'mem_prompt': nrm(ks[2], (BATCH, MEM_TOKENS, D_MODEL), f32),
        'mem_sample': nrm(ks[3], (DEC_BATCH, MEM_TOKENS, D_MODEL), f32),
        'ln_in_g': 1.0 + 0.02 * nrm(ks[4], (D_MODEL,), f32),
        'ln_in_b': 0.02 * nrm(ks[5], (D_MODEL,), f32),
        'rel_bias': 0.5 * nrm(ks[6], (REL_BUCKETS, ATTN_HEADS), f32),
        'w_in': nrm(ks[7], (DEPTH, D_MODEL, IN_WIDTH), f32) * D_MODEL ** -0.5,
        'attn_sink': 0.5 * nrm(ks[8], (DEPTH, ATTN_HEADS), f32),
        'dn_conv': nrm(ks[9], (DEPTH, DN_CONV, DN_CONV_DIM), f32) * DN_CONV ** -0.5,
        'dn_A_log': jnp.log(jax.random.uniform(ks[10], (DEPTH, 2, DN_HEADS), minval=1.0, maxval=16.0)),
        'dn_dt_bias': dt + jnp.log(-jnp.expm1(-dt)),
        'dn_norm_g': 1.0 + 0.02 * nrm(ks[13], (DEPTH, DN_VAL_DIM), f32),
        'w_mem_kv': nrm(ks[14], (DEPTH, D_MODEL, 2 * MEM_WIDTH), f32) * D_MODEL ** -0.5,
        'w_out': nrm(ks[15], (DEPTH, MIX_WIDTH, D_MODEL), f32) * (MIX_WIDTH ** -0.5 * DEEPNORM_BETA),
        'ln_g': 1.0 + 0.02 * nrm(ks[16], (DEPTH, D_MODEL), f32),
        'ln_b': 0.02 * nrm(ks[17], (DEPTH, D_MODEL), f32),
    }


def reference(x_prompt, x_sample, mem_prompt, mem_sample, ln_in_g, ln_in_b, rel_bias, w_in, attn_sink,
              dn_conv, dn_A_log, dn_dt_bias, dn_norm_g, w_mem_kv, w_out, ln_g, ln_b):
    y_prompt = encoder_trunk(x_prompt, mem_prompt, ln_in_g, ln_in_b, rel_bias, w_in, attn_sink, dn_conv,
                             dn_A_log, dn_dt_bias, dn_norm_g, w_mem_kv, w_out, ln_g, ln_b)
    y_sample = encoder_trunk(x_sample, mem_sample, ln_in_g, ln_in_b, rel_bias, w_in, attn_sink, dn_conv,
                             dn_A_log, dn_dt_bias, dn_norm_g, w_mem_kv, w_out, ln_g, ln_b)
    return (y_prompt, y_sample)
```

```python
import functools
import math

import jax
import jax.numpy as jnp
import numpy as np
from jax import lax
from jax.experimental import pallas as pl
from jax.experimental.pallas import tpu as pltpu

f32 = jnp.float32
bf16 = jnp.bfloat16

D_MODEL = 1024
HEAD_DIM = 64
ATTN_HEADS = 8
ATTN_KV_HEADS = 2
ATTN_GROUP = ATTN_HEADS // ATTN_KV_HEADS
ATTN_WIDTH = ATTN_HEADS * HEAD_DIM
ATTN_KV_WIDTH = ATTN_KV_HEADS * HEAD_DIM
WINDOW = 128
BLOCK = 128
REL_BUCKETS = 32
REL_MAX_DIST = 128
DN_HEADS = 4
DN_WIDTH = DN_HEADS * HEAD_DIM
DN_CONV = 5
DN_CONV_DIM = 3 * DN_WIDTH
DN_CHUNK = 64
MEM_HEADS = 4
MEM_WIDTH = MEM_HEADS * HEAD_DIM
DEPTH = 1
DEEPNORM_ALPHA = (2 * DEPTH) ** 0.25
LN_EPS = 1e-5
RMS_EPS = 1e-6
NEG = -1e30

LANES = 128
SUBLANES = 8
HALO = SUBLANES
VMEM_LIMIT = 56 * 1024 * 1024

NT = (((1,), (1,)), ((), ()))
TN = (((0,), (0,)), ((), ()))


def _dot(a, b):
    return jnp.dot(a, b, preferred_element_type=f32)


def _dot_nt(a, b):
    return lax.dot_general(a, b, NT, preferred_element_type=f32)


def _split3(x):
    hi = x.astype(bf16)
    r1 = x - hi.astype(f32)
    mid = r1.astype(bf16)
    lo = (r1 - mid.astype(f32)).astype(bf16)
    return hi, mid, lo


def _dot_exact_lhs(x, sel):
    hi, mid, lo = _split3(x)
    return _dot(hi, sel) + _dot(mid, sel) + _dot(lo, sel)


def _dot_exact_rhs(sel, x):
    hi, mid, lo = _split3(x)
    return _dot(sel, hi) + _dot(sel, mid) + _dot(sel, lo)


def _layer_norm(x, g, b):
    mu = jnp.mean(x, axis=-1, keepdims=True)
    xc = x - mu
    var = jnp.mean(xc * xc, axis=-1, keepdims=True)
    return xc * lax.rsqrt(var + LN_EPS) * g + b


def _silu(x):
    return x / (1.0 + jnp.exp(-x))


def _tile4(x):
    return jnp.concatenate([x, x, x, x], axis=0)


def _inproj_kernel(x_ref, xp_ref, xn_ref, lng_ref, lnb_ref, wattn_ref, wmem_ref, wdn_ref, wab_ref,
                   conv_ref, alog_ref, dtb_ref, ones_ref,
                   aq_ref, ak_ref, av_ref, az_ref, mq_ref, mz_ref, dq_ref, dk_ref, dv_ref, dz_ref, gb_ref,
                   hbuf, *, tm):
    i = pl.program_id(1)
    n = pl.num_programs(1)
    g = lng_ref[...]
    b = lnb_ref[...]
    xb = _layer_norm(x_ref[0], g, b).astype(bf16)

    aq_ref[0] = _dot(xb, wattn_ref[:, 0:512]).astype(bf16)
    ak_ref[0] = _dot(xb, wattn_ref[:, 512:640]).astype(bf16)
    av_ref[0] = _dot(xb, wattn_ref[:, 640:768]).astype(bf16)
    az_ref[0] = _dot(xb, wattn_ref[:, 768:1280]).astype(bf16)
    mq_ref[0] = _dot(xb, wmem_ref[:, 0:256]).astype(bf16)
    mz_ref[0] = _dot(xb, wmem_ref[:, 256:512]).astype(bf16)
    dz_ref[0] = _dot(xb, wdn_ref[:, DN_CONV_DIM:DN_CONV_DIM + DN_WIDTH]).astype(bf16)

    wqkv = wdn_ref[:, 0:DN_CONV_DIM]
    hp = _dot(_layer_norm(xp_ref[0], g, b).astype(bf16), wqkv)
    hn = _dot(_layer_norm(xn_ref[0], g, b).astype(bf16), wqkv)
    hbuf[0:HALO, :] = jnp.where(i > 0, hp, 0.0)
    hbuf[HALO:HALO + tm, :] = _dot(xb, wqkv)
    hbuf[HALO + tm:HALO + tm + HALO, :] = jnp.where(i < n - 1, hn, 0.0)
    half = DN_CONV // 2
    c = conv_ref[0:1, :] * hbuf[HALO - half:HALO - half + tm, :]
    for j in range(1, DN_CONV):
        c = c + conv_ref[j:j + 1, :] * hbuf[HALO - half + j:HALO - half + j + tm, :]
    c = _silu(c)
    ones = ones_ref[...]

    def l2n(t, scale):
        sq = t * t
        hi = sq.astype(bf16)
        lo = (sq - hi.astype(f32)).astype(bf16)
        ss = _dot(hi, ones) + _dot(lo, ones)
        return t * (lax.rsqrt(ss + 1e-6) * scale)

    dq_ref[0] = l2n(c[:, 0:DN_WIDTH], HEAD_DIM ** -0.5).astype(bf16)
    dk_ref[0] = l2n(c[:, DN_WIDTH:2 * DN_WIDTH], 1.0).astype(bf16)
    dv_ref[0] = c[:, 2 * DN_WIDTH:3 * DN_WIDTH].astype(bf16)

    ab = _dot(xb, wab_ref[...])
    z = ab + dtb_ref[...]
    sp = jnp.maximum(z, 0.0) + jnp.log1p(jnp.exp(-jnp.abs(z)))
    gdec = -jnp.exp(alog_ref[...]) * sp
    beta = 1.0 / (1.0 + jnp.exp(-ab))
    lane = lax.broadcasted_iota(jnp.int32, ab.shape, 1)
    gb_ref[0] = jnp.where(lane < 2 * DN_HEADS, gdec, beta)


def _inproj(x, p, tm):
    B, L, _ = x.shape
    nt = L // tm
    r8 = tm // HALO
    full = lambda a: pl.BlockSpec(a.shape, lambda b, i: (0,) * a.ndim)
    consts = [p['ln_in_g'], p['ln_in_b'], p['w_attn'], p['w_mem'], p['w_dn'], p['w_ab'],
              p['conv'], p['alog'], p['dtb'], p['ones_bd']]
    tok = lambda w: pl.BlockSpec((1, tm, w), lambda b, i: (b, i, 0))
    widths = [512, 128, 128, 512, 256, 256, 256, 256, 256, 256]
    out_shape = [jax.ShapeDtypeStruct((B, L, w), bf16) for w in widths] + [jax.ShapeDtypeStruct((B, L, LANES), f32)]
    return pl.pallas_call(
        functools.partial(_inproj_kernel, tm=tm),
        grid=(B, nt),
        in_specs=[pl.BlockSpec((1, tm, D_MODEL), lambda b, i: (b, i, 0)),
                  pl.BlockSpec((1, HALO, D_MODEL), lambda b, i: (b, jnp.maximum(i * r8 - 1, 0), 0)),
                  pl.BlockSpec((1, HALO, D_MODEL), lambda b, i: (b, jnp.minimum((i + 1) * r8, L // HALO - 1), 0)),
                  ] + [full(a) for a in consts],
        out_specs=[tok(w) for w in widths] + [tok(LANES)],
        out_shape=out_shape,
        scratch_shapes=[pltpu.VMEM((tm + 2 * HALO, DN_CONV_DIM), f32)],
        compiler_params=pltpu.CompilerParams(dimension_semantics=("parallel", "parallel"),
                                             vmem_limit_bytes=VMEM_LIMIT),
        name="inproj",
    )(x, x, x, *consts)


def _memkv_kernel(m_ref, lng_ref, lnb_ref, w_ref, o_ref):
    mb = _layer_norm(m_ref[0], lng_ref[...], lnb_ref[...]).astype(bf16)
    o_ref[0] = _dot(mb, w_ref[...]).astype(bf16)


def _memkv(mem, p):
    B, M, _ = mem.shape
    return pl.pallas_call(
        _memkv_kernel,
        grid=(B,),
        in_specs=[pl.BlockSpec((1, M, D_MODEL), lambda b: (b, 0, 0)),
                  pl.BlockSpec((1, D_MODEL), lambda b: (0, 0)),
                  pl.BlockSpec((1, D_MODEL), lambda b: (0, 0)),
                  pl.BlockSpec((D_MODEL, 2 * MEM_WIDTH), lambda b: (0, 0))],
        out_specs=pl.BlockSpec((1, M, 2 * MEM_WIDTH), lambda b: (b, 0, 0)),
        out_shape=jax.ShapeDtypeStruct((B, M, 2 * MEM_WIDTH), bf16),
        compiler_params=pltpu.CompilerParams(dimension_semantics=("parallel",), vmem_limit_bytes=VMEM_LIMIT),
        name="memkv",
    )(mem, p['ln_in_g'], p['ln_in_b'], p['w_memkv'])


def _attn_kernel(sink_ref, aq_ref, ak_ref, av_ref, az_ref, mq_ref, mz_ref, mkv_ref, bias_ref,
                 ya_ref, ym_ref):
    i = pl.program_id(1)
    nb = pl.num_programs(1)
    ip = jnp.maximum(i - 1, 0)
    inx = jnp.minimum(i + 1, nb - 1)

    def rows(ref, blk):
        return ref[0, pl.ds(pl.multiple_of(blk * BLOCK, BLOCK), BLOCK), :]

    kcat = jnp.concatenate([rows(ak_ref, ip), rows(ak_ref, i), rows(ak_ref, inx)], axis=0)
    vcat = jnp.concatenate([rows(av_ref, ip), rows(av_ref, i), rows(av_ref, inx)], axis=0)

    q = aq_ref[0]
    lane = lax.broadcasted_iota(jnp.int32, (BLOCK, LANES), 1)
    low = lane < HEAD_DIM
    zero = jnp.zeros((BLOCK, LANES), bf16)
    parts = []
    for c in range(ATTN_GROUP):
        t = q[:, c * LANES:(c + 1) * LANES]
        parts.append(jnp.where(low, t, zero))
        parts.append(jnp.where(low, zero, t))
    qs = jnp.concatenate(parts, axis=0)
    s = _dot_nt(qs, kcat) + bias_ref[...]
    col = lax.broadcasted_iota(jnp.int32, (1, 3 * BLOCK), 1)
    edge = jnp.where(((col < BLOCK) & (i == 0)) | ((col >= 2 * BLOCK) & (i == nb - 1)), NEG, 0.0)
    s = s + edge

    ps, rinv = [], []
    for h in range(ATTN_HEADS):
        sh = s[h * BLOCK:(h + 1) * BLOCK, :]
        sk = sink_ref[h]
        m = jnp.maximum(jnp.max(sh, axis=-1, keepdims=True), sk)
        e = jnp.exp(sh - m)
        den = jnp.sum(e, axis=-1, keepdims=True) + jnp.exp(sk - m)
        ps.append(e.astype(bf16))
        rinv.append(1.0 / den)
    o = _dot(jnp.concatenate(ps, axis=0), vcat)
    az = az_ref[0].astype(f32)
    for c in range(ATTN_GROUP):
        o0 = o[(2 * c) * BLOCK:(2 * c + 1) * BLOCK, :] * rinv[2 * c]
        o1 = o[(2 * c + 1) * BLOCK:(2 * c + 2) * BLOCK, :] * rinv[2 * c + 1]
        y = jnp.where(low, o0, o1) * _silu(az[:, c * LANES:(c + 1) * LANES])
        ya_ref[0, :, c * LANES:(c + 1) * LANES] = y.astype(bf16)

    mq = mq_ref[0]
    lane_m = lax.broadcasted_iota(jnp.int32, (BLOCK, MEM_WIDTH), 1) // HEAD_DIM
    zm = jnp.zeros((BLOCK, MEM_WIDTH), bf16)
    mqs = jnp.concatenate([jnp.where(lane_m == h, mq, zm) for h in range(MEM_HEADS)], axis=0)
    mk = mkv_ref[0, :, 0:MEM_WIDTH]
    mv = mkv_ref[0, :, MEM_WIDTH:2 * MEM_WIDTH]
    sm = _dot_nt(mqs, mk)
    mm = jnp.max(sm, axis=-1, keepdims=True)
    em = jnp.exp(sm - mm)
    rm = 1.0 / jnp.sum(em, axis=-1, keepdims=True)
    om = _dot(em.astype(bf16), mv) * rm
    ym = jnp.zeros((BLOCK, MEM_WIDTH), f32)
    for h in range(MEM_HEADS):
        ym = jnp.where(lane_m == h, om[h * BLOCK:(h + 1) * BLOCK, :], ym)
    ym_ref[0] = (ym * _silu(mz_ref[0].astype(f32))).astype(bf16)


def _attn(aq, ak, av, az, mq, mz, mkv, p):
    B, L, _ = aq.shape
    nb = L // BLOCK
    M = mkv.shape[1]
    tok = lambda w: pl.BlockSpec((1, BLOCK, w), lambda b, i: (b, i, 0))
    seq = lambda w: pl.BlockSpec((1, L, w), lambda b, i: (b, 0, 0))
    return pl.pallas_call(
        _attn_kernel,
        grid=(B, nb),
        in_specs=[pl.BlockSpec(memory_space=pltpu.SMEM),
                  tok(ATTN_WIDTH), seq(ATTN_KV_WIDTH), seq(ATTN_KV_WIDTH), tok(ATTN_WIDTH),
                  tok(MEM_WIDTH), tok(MEM_WIDTH),
                  pl.BlockSpec((1, M, 2 * MEM_WIDTH), lambda b, i: (b, 0, 0)),
                  pl.BlockSpec((ATTN_HEADS * BLOCK, 3 * BLOCK), lambda b, i: (0, 0))],
        out_specs=[tok(ATTN_WIDTH), tok(MEM_WIDTH)],
        out_shape=[jax.ShapeDtypeStruct((B, L, ATTN_WIDTH), bf16), jax.ShapeDtypeStruct((B, L, MEM_WIDTH), bf16)],
        compiler_params=pltpu.CompilerParams(dimension_semantics=("parallel", "arbitrary"),
                                             vmem_limit_bytes=VMEM_LIMIT),
        name="attn",
    )(p['sink'], aq, ak, av, az, mq, mz, mkv, p['bias'])


def _dn_intra_kernel(q_ref, k_ref, v_ref, gb_ref, tri_ref, esel_ref, bdm_ref,
                     m_ref, n_ref, qp_ref, op_ref, gt_ref, *, nc):
    C = DN_CHUNK
    W = DN_WIDTH
    ri = lax.broadcasted_iota(jnp.int32, (C, W), 0)
    li = lax.broadcasted_iota(jnp.int32, (C, W), 1)
    lj = li % C
    lh = li // C
    eye = ri == lj
    eye_f = jnp.where(eye, 1.0, 0.0)
    bdm = bdm_ref[...]
    col = lax.broadcasted_iota(jnp.int32, (C, LANES), 1)
    tri = tri_ref[...]
    esel = esel_ref[...]

    def bd(y):
        return _tile4(y) * bdm

    def compact(fm):
        acc = jnp.zeros((C, W), f32)
        for h in range(DN_HEADS):
            acc = jnp.where(lh == h, fm[h * C:(h + 1) * C, :], acc)
        return acc

    def chunk(ci, carry):
        r0 = pl.multiple_of(ci * C, C)
        sl = pl.ds(r0, C)
        qb = q_ref[0, sl, :]
        kb = k_ref[0, sl, :]
        vb = v_ref[0, sl, :]
        gbv = gb_ref[0, sl, :]
        gc = _dot_exact_rhs(tri, gbv)
        tot = gc[C - 1:C, :]
        is_bwd = (col >= DN_HEADS) & (col < 2 * DN_HEADS)
        src = jnp.where(col >= 2 * DN_HEADS, gbv, jnp.where(is_bwd, tot - gc + gbv, gc))
        bc = _dot_exact_lhs(src, esel)
        gram = _dot_nt(jnp.concatenate([kb, qb], axis=0), bd(kb))
        kk = gram[0:C]
        qk = gram[C:2 * C]
        qf = qb.astype(f32)
        kf = kb.astype(f32)
        vf = vb.astype(f32)
        for d in range(2):
            gq = bc[:, d * W:(d + 1) * W]
            bq = bc[:, 2 * W + d * W:2 * W + (d + 1) * W]
            r = jnp.sum(jnp.where(eye, gq, 0.0), axis=0, keepdims=True)
            if d == 0:
                incl = ri >= lj
                strict = ri > lj
                glast = gq[C - 1:C, :]
            else:
                incl = ri <= lj
                strict = ri < lj
                glast = gq[0:1, :]
            dec = jnp.where(incl, jnp.exp(jnp.minimum(gq - r, 0.0)), 0.0)
            nn = jnp.where(strict, -(kk * bq * dec), 0.0)
            qkd = qk * dec
            eg = jnp.exp(gq)
            vbeta = (vf * bq).astype(bf16)
            kbg = (kf * bq * eg).astype(bf16)
            qd = qf * eg
            kd = (kf * jnp.exp(glast - gq)).astype(bf16)
            t = eye_f + nn
            nb_ = nn.astype(bf16)
            x = _dot(nb_, bd(nb_))
            for s_ in range(5):
                xb = x.astype(bf16)
                xbd = bd(xb)
                if s_ < 4:
                    rr = _dot(jnp.concatenate([t.astype(bf16), xb], axis=0), xbd)
                    t = t + rr[0:C]
                    x = rr[C:2 * C]
                else:
                    t = t + _dot(t.astype(bf16), xbd)
            tb = t.astype(bf16)
            u = _dot(tb, bd(vbeta)).astype(bf16)
            w = _dot(tb, bd(kbg)).astype(bf16)
            fm = lax.dot_general(kd, jnp.concatenate([w, u], axis=1), TN, preferred_element_type=f32)
            m_ref[0, d, ci] = (-compact(fm[:, 0:W])).astype(bf16)
            n_ref[0, d, ci] = compact(fm[:, W:2 * W])
            qkb = qkd.astype(bf16)
            qp_ref[0, d, sl, :] = (qd - _dot(qkb, bd(w))).astype(bf16)
            op_ref[0, d, sl, :] = _dot(qkb, bd(u))
            gt_ref[0, d, ci] = jnp.exp(glast)
        return carry

    lax.fori_loop(0, nc, chunk, 0)


def _dn_intra(dq, dk, dv, gb, p, tb):
    B, L, _ = dq.shape
    nc = tb // DN_CHUNK
    nchunk = L // DN_CHUNK
    tok = lambda w: pl.BlockSpec((1, tb, w), lambda b, i: (b, i, 0))
    full = lambda a: pl.BlockSpec(a.shape, lambda b, i: (0,) * a.ndim)
    consts = [p['tri'], p['esel'], p['bdm']]
    return pl.pallas_call(
        functools.partial(_dn_intra_kernel, nc=nc),
        grid=(B, L // tb),
        in_specs=[tok(DN_WIDTH), tok(DN_WIDTH), tok(DN_WIDTH), tok(LANES)] + [full(a) for a in consts],
        out_specs=[pl.BlockSpec((1, 2, nc, DN_CHUNK, DN_WIDTH), lambda b, i: (b, 0, i, 0, 0)),
                   pl.BlockSpec((1, 2, nc, DN_CHUNK, DN_WIDTH), lambda b, i: (b, 0, i, 0, 0)),
                   pl.BlockSpec((1, 2, tb, DN_WIDTH), lambda b, i: (b, 0, i, 0)),
                   pl.BlockSpec((1, 2, tb, DN_WIDTH), lambda b, i: (b, 0, i, 0)),
                   pl.BlockSpec((1, 2, nc, 1, DN_WIDTH), lambda b, i: (b, 0, i, 0, 0))],
        out_shape=[jax.ShapeDtypeStruct((B, 2, nchunk, DN_CHUNK, DN_WIDTH), bf16),
                   jax.ShapeDtypeStruct((B, 2, nchunk, DN_CHUNK, DN_WIDTH), f32),
                   jax.ShapeDtypeStruct((B, 2, L, DN_WIDTH), bf16),
                   jax.ShapeDtypeStruct((B, 2, L, DN_WIDTH), f32),
                   jax.ShapeDtypeStruct((B, 2, nchunk, 1, DN_WIDTH), f32)],
        compiler_params=pltpu.CompilerParams(dimension_semantics=("parallel", "parallel"),
                                             vmem_limit_bytes=VMEM_LIMIT),
        name="dn_intra",
    )(dq, dk, dv, gb, *consts)


def _dn_scan_kernel(mf_ref, nf_ref, qf_ref, of_ref, gf_ref, mb_ref, nb_ref, qb_ref, ob_ref, gbk_ref,
                    bdm_ref, bdmf_ref, yf_ref, yb_ref, s_ref, *, nc):
    C = DN_CHUNK

    @pl.when(pl.program_id(1) == 0)
    def _():
        s_ref[...] = jnp.zeros_like(s_ref)

    bdm = bdm_ref[...]
    bdmf = bdmf_ref[...]

    def one(d, ci, m_ref, n_ref, q_ref, o_ref, g_ref, y_ref):
        sl = pl.ds(pl.multiple_of(ci * C, C), C)
        s = s_ref[d]
        mbd = _tile4(m_ref[0, 0, ci]) * bdm
        rr = _dot(jnp.concatenate([mbd, q_ref[0, 0, sl, :]], axis=0), s.astype(bf16))
        y_ref[0, sl, :] = rr[4 * C:5 * C] + o_ref[0, 0, sl, :]
        s_ref[d] = s * g_ref[0, 0, ci] + rr[0:4 * C] + _tile4(n_ref[0, 0, ci]) * bdmf

    def step(j, carry):
        one(0, j, mf_ref, nf_ref, qf_ref, of_ref, gf_ref, yf_ref)
        one(1, nc - 1 - j, mb_ref, nb_ref, qb_ref, ob_ref, gbk_ref, yb_ref)
        return carry

    lax.fori_loop(0, nc, step, 0)


def _dn_scan(m, n, qp, op, gt, p, tb):
    B, _, nchunk, C, W = m.shape
    L = nchunk * C
    nc = tb // C
    ns = L // tb
    fwd5 = lambda last: pl.BlockSpec((1, 1, nc, last, W), lambda b, i: (b, 0, i, 0, 0))
    bwd5 = lambda last: pl.BlockSpec((1, 1, nc, last, W), lambda b, i: (b, 1, ns - 1 - i, 0, 0))
    fwd4 = pl.BlockSpec((1, 1, tb, W), lambda b, i: (b, 0, i, 0))
    bwd4 = pl.BlockSpec((1, 1, tb, W), lambda b, i: (b, 1, ns - 1 - i, 0))
    full = lambda a: pl.BlockSpec(a.shape, lambda b, i: (0,) * a.ndim)
    return pl.pallas_call(
        functools.partial(_dn_scan_kernel, nc=nc),
        grid=(B, ns),
        in_specs=[fwd5(C), fwd5(C), fwd4, fwd4, fwd5(1), bwd5(C), bwd5(C), bwd4, bwd4, bwd5(1),
                  full(p['bdm']), full(p['bdm_f32'])],
        out_specs=[pl.BlockSpec((1, tb, W), lambda b, i: (b, i, 0)),
                   pl.BlockSpec((1, tb, W), lambda b, i: (b, ns - 1 - i, 0))],
        out_shape=[jax.ShapeDtypeStruct((B, L, W), f32), jax.ShapeDtypeStruct((B, L, W), f32)],
        scratch_shapes=[pltpu.VMEM((2, W, W), f32)],
        compiler_params=pltpu.CompilerParams(dimension_semantics=("parallel", "arbitrary"),
                                             vmem_limit_bytes=VMEM_LIMIT),
        name="dn_scan",
    )(m, n, qp, op, gt, m, n, qp, op, gt, p['bdm'], p['bdm_f32'])


def _outproj_kernel(x_ref, ya_ref, ym_ref, yf_ref, yb_ref, dz_ref, lig_ref, lib_ref, lg_ref, lb_ref,
                    ng_ref, wa_ref, wd_ref, wm_ref, ones_ref, o_ref):
    xln = _layer_norm(x_ref[0], lig_ref[...], lib_ref[...])
    o = yf_ref[0] + yb_ref[0]
    sq = o * o
    hi = sq.astype(bf16)
    lo = (sq - hi.astype(f32)).astype(bf16)
    ones = ones_ref[...]
    ms = (_dot(hi, ones) + _dot(lo, ones)) * (1.0 / HEAD_DIM)
    yd = o * lax.rsqrt(ms + RMS_EPS) * ng_ref[...] * _silu(dz_ref[0].astype(f32))
    y = _dot(ya_ref[0], wa_ref[...]) + _dot(yd.astype(bf16), wd_ref[...]) + _dot(ym_ref[0], wm_ref[...])
    o_ref[0] = _layer_norm(DEEPNORM_ALPHA * xln + y, lg_ref[...], lb_ref[...])


def _outproj(x, ya, ym, yf, yb, dz, p, tm):
    B, L, _ = x.shape
    tok = lambda w: pl.BlockSpec((1, tm, w), lambda b, i: (b, i, 0))
    full = lambda a: pl.BlockSpec(a.shape, lambda b, i: (0,) * a.ndim)
    consts = [p['ln_in_g'], p['ln_in_b'], p['ln_g'], p['ln_b'], p['norm_g'], p['wo_attn'], p['wo_dn'], p['wo_mem'],
              p['ones_bd']]
    return pl.pallas_call(
        _outproj_kernel,
        grid=(B, L // tm),
        in_specs=[tok(D_MODEL), tok(ATTN_WIDTH), tok(MEM_WIDTH), tok(DN_WIDTH), tok(DN_WIDTH), tok(DN_WIDTH)]
                 + [full(a) for a in consts],
        out_specs=tok(D_MODEL),
        out_shape=jax.ShapeDtypeStruct((B, L, D_MODEL), f32),
        compiler_params=pltpu.CompilerParams(dimension_semantics=("parallel", "parallel"),
                                             vmem_limit_bytes=VMEM_LIMIT),
        name="outproj",
    )(x, ya, ym, yf, yb, dz, *consts)


def _t5_bucket(rel):
    nb = REL_BUCKETS // 2
    max_exact = nb // 2
    n = jnp.abs(rel)
    large = max_exact + (jnp.log(jnp.maximum(n, 1).astype(f32) / max_exact)
                         / math.log(REL_MAX_DIST / max_exact) * (nb - max_exact)).astype(jnp.int32)
    large = jnp.minimum(large, nb - 1)
    return jnp.where(rel > 0, nb, 0) + jnp.where(n < max_exact, n, large)


def _prepare(ln_in_g, ln_in_b, rel_bias, w_in, attn_sink, dn_conv, dn_A_log, dn_dt_bias, dn_norm_g,
             w_mem_kv, w_out, ln_g, ln_b):
    offs = np.cumsum([0, ATTN_WIDTH, ATTN_KV_WIDTH, ATTN_KV_WIDTH, ATTN_WIDTH, DN_WIDTH, DN_WIDTH, DN_WIDTH,
                      DN_WIDTH, 2 * DN_HEADS, 2 * DN_HEADS, MEM_WIDTH, MEM_WIDTH])
    (w_aq, w_ak, w_av, w_az, w_dq, w_dk, w_dv, w_dz, w_da, w_db, w_mq, w_mz) = [
        w_in[:, int(offs[j]):int(offs[j + 1])] for j in range(12)]
    order = [kv * ATTN_GROUP + c for c in range(ATTN_GROUP) for kv in range(ATTN_KV_HEADS)]
    perm = np.concatenate([np.arange(h * HEAD_DIM, (h + 1) * HEAD_DIM) for h in order])
    scale = HEAD_DIM ** -0.5
    p = {}
    p['w_attn'] = jnp.concatenate([w_aq[:, perm] * scale, w_ak, w_av, w_az[:, perm]], axis=1).astype(bf16)
    p['w_mem'] = jnp.concatenate([w_mq * scale, w_mz], axis=1).astype(bf16)
    p['w_dn'] = jnp.concatenate([w_dq, w_dk, w_dv, w_dz], axis=1).astype(bf16)
    p['w_ab'] = jnp.pad(jnp.concatenate([w_da, w_db], axis=1), ((0, 0), (0, LANES - 4 * DN_HEADS))).astype(bf16)
    p['conv'] = jnp.pad(dn_conv, ((0, SUBLANES - DN_CONV), (0, 0)))
    p['alog'] = jnp.pad(dn_A_log.reshape(1, 2 * DN_HEADS), ((0, 0), (0, LANES - 2 * DN_HEADS)))
    p['dtb'] = jnp.pad(dn_dt_bias.reshape(1, 2 * DN_HEADS), ((0, 0), (0, LANES - 2 * DN_HEADS)))
    p['ln_in_g'] = ln_in_g.reshape(1, D_MODEL)
    p['ln_in_b'] = ln_in_b.reshape(1, D_MODEL)
    p['ln_g'] = ln_g.reshape(1, D_MODEL)
    p['ln_b'] = ln_b.reshape(1, D_MODEL)
    p['norm_g'] = jnp.tile(dn_norm_g.reshape(1, HEAD_DIM), (1, DN_HEADS))
    p['w_memkv'] = w_mem_kv.astype(bf16)
    p['wo_attn'] = w_out[0:ATTN_WIDTH][perm].astype(bf16)
    p['wo_dn'] = w_out[ATTN_WIDTH:ATTN_WIDTH + DN_WIDTH].astype(bf16)
    p['wo_mem'] = w_out[ATTN_WIDTH + DN_WIDTH:].astype(bf16)
    p['sink'] = attn_sink[jnp.asarray(order)]

    blk = np.arange(DN_WIDTH) // HEAD_DIM
    bd = (blk[:, None] == blk[None, :])
    p['bdm'] = jnp.asarray(bd, bf16)
    p['bdm_f32'] = jnp.asarray(bd, f32)
    p['ones_bd'] = p['bdm']
    p['tri'] = jnp.asarray(np.tril(np.ones((DN_CHUNK, DN_CHUNK))), bf16)
    esel = np.zeros((LANES, 4 * DN_WIDTH), np.float32)
    for c in range(4 * DN_HEADS):
        esel[c, c * HEAD_DIM:(c + 1) * HEAD_DIM] = 1.0
    p['esel'] = jnp.asarray(esel, bf16)

    t = jnp.arange(BLOCK)[:, None]
    s = jnp.arange(3 * BLOCK)[None, :]
    rel = s - BLOCK - t
    bias = rel_bias[_t5_bucket(rel)].astype(f32)
    bias = jnp.where((jnp.abs(rel) <= WINDOW)[..., None], bias, NEG)
    bias = bias.transpose(2, 0, 1)[jnp.asarray(order)]
    p['bias'] = bias.reshape(ATTN_HEADS * BLOCK, 3 * BLOCK)
    return p


def _trunk(x, mem, p, tm=512, tb_intra=512, tb_scan=512):
    L = x.shape[1]
    tm = min(tm, L)
    tb_intra = min(tb_intra, L)
    tb_scan = min(tb_scan, L)
    aq, ak, av, az, mq, mz, dq, dk, dv, dz, gb = _inproj(x, p, tm)
    mkv = _memkv(mem, p)
    ya, ym = _attn(aq, ak, av, az, mq, mz, mkv, p)
    m, n, qp, op, gt = _dn_intra(dq, dk, dv, gb, p, tb_intra)
    yf, yb = _dn_scan(m, n, qp, op, gt, p, tb_scan)
    return _outproj(x, ya, ym, yf, yb, dz, p, tm)


def kernel(x_prompt, x_sample, mem_prompt, mem_sample, ln_in_g, ln_in_b, rel_bias, w_in, attn_sink, dn_conv,
           dn_A_log, dn_dt_bias, dn_norm_g, w_mem_kv, w_out, ln_g, ln_b):
    p = _prepare(ln_in_g, ln_in_b, rel_bias, w_in[0], attn_sink[0], dn_conv[0], dn_A_log[0], dn_dt_bias[0],
                 dn_norm_g[0], w_mem_kv[0], w_out[0], ln_g[0], ln_b[0])
    return (_trunk(x_prompt, mem_prompt, p), _trunk(x_sample, mem_sample, p))
```

```python
import functools
import math

import jax
import jax.numpy as jnp
import numpy as np
from jax import lax
from jax.experimental import pallas as pl
from jax.experimental.pallas import tpu as pltpu

f32 = jnp.float32
bf16 = jnp.bfloat16

D_MODEL = 1024
HEAD_DIM = 64
ATTN_HEADS = 8
ATTN_KV_HEADS = 2
ATTN_GROUP = ATTN_HEADS // ATTN_KV_HEADS
ATTN_WIDTH = ATTN_HEADS * HEAD_DIM
ATTN_KV_WIDTH = ATTN_KV_HEADS * HEAD_DIM
WINDOW = 128
BLOCK = 128
REL_BUCKETS = 32
REL_MAX_DIST = 128
DN_HEADS = 4
DN_WIDTH = DN_HEADS * HEAD_DIM
DN_CONV = 5
DN_CONV_DIM = 3 * DN_WIDTH
DN_CHUNK = 64
MEM_HEADS = 4
MEM_WIDTH = MEM_HEADS * HEAD_DIM
DEPTH = 1
DEEPNORM_ALPHA = (2 * DEPTH) ** 0.25
LN_EPS = 1e-5
RMS_EPS = 1e-6
NEG = -1e30

LANES = 128
SUBLANES = 8
HALO = SUBLANES
VMEM_LIMIT = 56 * 1024 * 1024

NT = (((1,), (1,)), ((), ()))
TN = (((0,), (0,)), ((), ()))


def _dot(a, b):
    return jnp.dot(a, b, preferred_element_type=f32)


def _dot_nt(a, b):
    return lax.dot_general(a, b, NT, preferred_element_type=f32)


def _split3(x):
    hi = x.astype(bf16)
    r1 = x - hi.astype(f32)
    mid = r1.astype(bf16)
    lo = (r1 - mid.astype(f32)).astype(bf16)
    return hi, mid, lo


def _dot_exact_lhs(x, sel):
    hi, mid, lo = _split3(x)
    return _dot(hi, sel) + _dot(mid, sel) + _dot(lo, sel)


def _dot_exact_rhs(sel, x):
    hi, mid, lo = _split3(x)
    return _dot(sel, hi) + _dot(sel, mid) + _dot(sel, lo)


def _layer_norm(x, g, b):
    mu = jnp.mean(x, axis=-1, keepdims=True)
    xc = x - mu
    var = jnp.mean(xc * xc, axis=-1, keepdims=True)
    return xc * lax.rsqrt(var + LN_EPS) * g + b


def _silu(x):
    return x / (1.0 + jnp.exp(-x))


def _inproj_kernel(x_ref, xp_ref, xn_ref, lng_ref, lnb_ref, wattn_ref, wmem_ref, wdn_ref, wab_ref,
                   conv_ref, alog_ref, dtb_ref, ones_ref,
                   aq_ref, ak_ref, av_ref, az_ref, mq_ref, mz_ref, dq_ref, dk_ref, dv_ref, dz_ref, gb_ref,
                   hbuf, *, tm):
    i = pl.program_id(1)
    n = pl.num_programs(1)
    g = lng_ref[...]
    b = lnb_ref[...]
    xb = _layer_norm(x_ref[0], g, b).astype(bf16)

    aq_ref[0] = _dot(xb, wattn_ref[:, 0:512]).astype(bf16)
    ak_ref[0] = _dot(xb, wattn_ref[:, 512:640]).astype(bf16)
    av_ref[0] = _dot(xb, wattn_ref[:, 640:768]).astype(bf16)
    az_ref[0] = _dot(xb, wattn_ref[:, 768:1280]).astype(bf16)
    mq_ref[0] = _dot(xb, wmem_ref[:, 0:256]).astype(bf16)
    mz_ref[0] = _dot(xb, wmem_ref[:, 256:512]).astype(bf16)
    dz_ref[0] = _dot(xb, wdn_ref[:, DN_CONV_DIM:DN_CONV_DIM + DN_WIDTH]).astype(bf16)

    wqkv = wdn_ref[:, 0:DN_CONV_DIM]
    hp = _dot(_layer_norm(xp_ref[0], g, b).astype(bf16), wqkv)
    hn = _dot(_layer_norm(xn_ref[0], g, b).astype(bf16), wqkv)
    hbuf[0:HALO, :] = jnp.where(i > 0, hp, 0.0)
    hbuf[HALO:HALO + tm, :] = _dot(xb, wqkv)
    hbuf[HALO + tm:HALO + tm + HALO, :] = jnp.where(i < n - 1, hn, 0.0)
    half = DN_CONV // 2
    c = conv_ref[0:1, :] * hbuf[HALO - half:HALO - half + tm, :]
    for j in range(1, DN_CONV):
        c = c + conv_ref[j:j + 1, :] * hbuf[HALO - half + j:HALO - half + j + tm, :]
    c = _silu(c)
    ones = ones_ref[...]

    def l2n(t, scale):
        sq = t * t
        hi = sq.astype(bf16)
        lo = (sq - hi.astype(f32)).astype(bf16)
        ss = _dot(hi, ones) + _dot(lo, ones)
        return t * (lax.rsqrt(ss + 1e-6) * scale)

    dq_ref[0] = l2n(c[:, 0:DN_WIDTH], HEAD_DIM ** -0.5).astype(bf16)
    dk_ref[0] = l2n(c[:, DN_WIDTH:2 * DN_WIDTH], 1.0).astype(bf16)
    dv_ref[0] = c[:, 2 * DN_WIDTH:3 * DN_WIDTH].astype(bf16)

    ab = _dot(xb, wab_ref[...])
    z = ab + dtb_ref[...]
    sp = jnp.maximum(z, 0.0) + jnp.log1p(jnp.exp(-jnp.abs(z)))
    gdec = -jnp.exp(alog_ref[...]) * sp
    beta = 1.0 / (1.0 + jnp.exp(-ab))
    lane = lax.broadcasted_iota(jnp.int32, ab.shape, 1)
    gb_ref[0] = jnp.where(lane < 2 * DN_HEADS, gdec, beta)


def _inproj(x, p, tm):
    B, L, _ = x.shape
    nt = L // tm
    r8 = tm // HALO
    full = lambda a: pl.BlockSpec(a.shape, lambda b, i: (0,) * a.ndim)
    consts = [p['ln_in_g'], p['ln_in_b'], p['w_attn'], p['w_mem'], p['w_dn'], p['w_ab'],
              p['conv'], p['alog'], p['dtb'], p['ones_bd']]
    tok = lambda w: pl.BlockSpec((1, tm, w), lambda b, i: (b, i, 0))
    widths = [512, 128, 128, 512, 256, 256, 256, 256, 256, 256]
    out_shape = [jax.ShapeDtypeStruct((B, L, w), bf16) for w in widths] + [jax.ShapeDtypeStruct((B, L, LANES), f32)]
    return pl.pallas_call(
        functools.partial(_inproj_kernel, tm=tm),
        grid=(B, nt),
        in_specs=[pl.BlockSpec((1, tm, D_MODEL), lambda b, i: (b, i, 0)),
                  pl.BlockSpec((1, HALO, D_MODEL), lambda b, i: (b, jnp.maximum(i * r8 - 1, 0), 0)),
                  pl.BlockSpec((1, HALO, D_MODEL), lambda b, i: (b, jnp.minimum((i + 1) * r8, L // HALO - 1), 0)),
                  ] + [full(a) for a in consts],
        out_specs=[tok(w) for w in widths] + [tok(LANES)],
        out_shape=out_shape,
        scratch_shapes=[pltpu.VMEM((tm + 2 * HALO, DN_CONV_DIM), f32)],
        compiler_params=pltpu.CompilerParams(dimension_semantics=("parallel", "parallel"),
                                             vmem_limit_bytes=VMEM_LIMIT),
        name="inproj",
    )(x, x, x, *consts)


def _memkv_kernel(m_ref, lng_ref, lnb_ref, w_ref, o_ref):
    mb = _layer_norm(m_ref[0], lng_ref[...], lnb_ref[...]).astype(bf16)
    o_ref[0] = _dot(mb, w_ref[...]).astype(bf16)


def _memkv(mem, p):
    B, M, _ = mem.shape
    return pl.pallas_call(
        _memkv_kernel,
        grid=(B,),
        in_specs=[pl.BlockSpec((1, M, D_MODEL), lambda b: (b, 0, 0)),
                  pl.BlockSpec((1, D_MODEL), lambda b: (0, 0)),
                  pl.BlockSpec((1, D_MODEL), lambda b: (0, 0)),
                  pl.BlockSpec((D_MODEL, 2 * MEM_WIDTH), lambda b: (0, 0))],
        out_specs=pl.BlockSpec((1, M, 2 * MEM_WIDTH), lambda b: (b, 0, 0)),
        out_shape=jax.ShapeDtypeStruct((B, M, 2 * MEM_WIDTH), bf16),
        compiler_params=pltpu.CompilerParams(dimension_semantics=("parallel",), vmem_limit_bytes=VMEM_LIMIT),
        name="memkv",
    )(mem, p['ln_in_g'], p['ln_in_b'], p['w_memkv'])


def _attn_kernel(sink_ref, aq_ref, ak_ref, av_ref, az_ref, mq_ref, mz_ref, mkv_ref, bias_ref,
                 ya_ref, ym_ref):
    i = pl.program_id(1)
    nb = pl.num_programs(1)
    ip = jnp.maximum(i - 1, 0)
    inx = jnp.minimum(i + 1, nb - 1)

    def rows(ref, blk):
        return ref[0, pl.ds(pl.multiple_of(blk * BLOCK, BLOCK), BLOCK), :]

    kcat = jnp.concatenate([rows(ak_ref, ip), rows(ak_ref, i), rows(ak_ref, inx)], axis=0)
    vcat = jnp.concatenate([rows(av_ref, ip), rows(av_ref, i), rows(av_ref, inx)], axis=0)

    q = aq_ref[0]
    lane = lax.broadcasted_iota(jnp.int32, (BLOCK, LANES), 1)
    low = lane < HEAD_DIM
    zero = jnp.zeros((BLOCK, LANES), bf16)
    parts = []
    for c in range(ATTN_GROUP):
        t = q[:, c * LANES:(c + 1) * LANES]
        parts.append(jnp.where(low, t, zero))
        parts.append(jnp.where(low, zero, t))
    qs = jnp.concatenate(parts, axis=0)
    s = _dot_nt(qs, kcat) + bias_ref[...]
    col = lax.broadcasted_iota(jnp.int32, (1, 3 * BLOCK), 1)
    edge = jnp.where(((col < BLOCK) & (i == 0)) | ((col >= 2 * BLOCK) & (i == nb - 1)), NEG, 0.0)
    s = s + edge

    ps, rinv = [], []
    for h in range(ATTN_HEADS):
        sh = s[h * BLOCK:(h + 1) * BLOCK, :]
        sk = sink_ref[h]
        m = jnp.maximum(jnp.max(sh, axis=-1, keepdims=True), sk)
        e = jnp.exp(sh - m)
        den = jnp.sum(e, axis=-1, keepdims=True) + jnp.exp(sk - m)
        ps.append(e.astype(bf16))
        rinv.append(1.0 / den)
    o = _dot(jnp.concatenate(ps, axis=0), vcat)
    az = az_ref[0].astype(f32)
    for c in range(ATTN_GROUP):
        o0 = o[(2 * c) * BLOCK:(2 * c + 1) * BLOCK, :] * rinv[2 * c]
        o1 = o[(2 * c + 1) * BLOCK:(2 * c + 2) * BLOCK, :] * rinv[2 * c + 1]
        y = jnp.where(low, o0, o1) * _silu(az[:, c * LANES:(c + 1) * LANES])
        ya_ref[0, :, c * LANES:(c + 1) * LANES] = y.astype(bf16)

    mq = mq_ref[0]
    lane_m = lax.broadcasted_iota(jnp.int32, (BLOCK, MEM_WIDTH), 1) // HEAD_DIM
    zm = jnp.zeros((BLOCK, MEM_WIDTH), bf16)
    mqs = jnp.concatenate([jnp.where(lane_m == h, mq, zm) for h in range(MEM_HEADS)], axis=0)
    mk = mkv_ref[0, :, 0:MEM_WIDTH]
    mv = mkv_ref[0, :, MEM_WIDTH:2 * MEM_WIDTH]
    sm = _dot_nt(mqs, mk)
    mm = jnp.max(sm, axis=-1, keepdims=True)
    em = jnp.exp(sm - mm)
    rm = 1.0 / jnp.sum(em, axis=-1, keepdims=True)
    om = _dot(em.astype(bf16), mv) * rm
    ym = jnp.zeros((BLOCK, MEM_WIDTH), f32)
    for h in range(MEM_HEADS):
        ym = jnp.where(lane_m == h, om[h * BLOCK:(h + 1) * BLOCK, :], ym)
    ym_ref[0] = (ym * _silu(mz_ref[0].astype(f32))).astype(bf16)


def _attn(aq, ak, av, az, mq, mz, mkv, p):
    B, L, _ = aq.shape
    nb = L // BLOCK
    M = mkv.shape[1]
    tok = lambda w: pl.BlockSpec((1, BLOCK, w), lambda b, i: (b, i, 0))
    seq = lambda w: pl.BlockSpec((1, L, w), lambda b, i: (b, 0, 0))
    return pl.pallas_call(
        _attn_kernel,
        grid=(B, nb),
        in_specs=[pl.BlockSpec(memory_space=pltpu.SMEM),
                  tok(ATTN_WIDTH), seq(ATTN_KV_WIDTH), seq(ATTN_KV_WIDTH), tok(ATTN_WIDTH),
                  tok(MEM_WIDTH), tok(MEM_WIDTH),
                  pl.BlockSpec((1, M, 2 * MEM_WIDTH), lambda b, i: (b, 0, 0)),
                  pl.BlockSpec((ATTN_HEADS * BLOCK, 3 * BLOCK), lambda b, i: (0, 0))],
        out_specs=[tok(ATTN_WIDTH), tok(MEM_WIDTH)],
        out_shape=[jax.ShapeDtypeStruct((B, L, ATTN_WIDTH), bf16), jax.ShapeDtypeStruct((B, L, MEM_WIDTH), bf16)],
        compiler_params=pltpu.CompilerParams(dimension_semantics=("parallel", "arbitrary"),
                                             vmem_limit_bytes=VMEM_LIMIT),
        name="attn",
    )(p['sink'], aq, ak, av, az, mq, mz, mkv, p['bias'])


def _bd2(y, lo):
    z = jnp.zeros_like(y)
    return jnp.concatenate([jnp.where(lo, y, z), jnp.where(lo, z, y)], axis=0)


def _halves(x):
    return x[:, 0:LANES], x[:, LANES:2 * LANES]


def _dn_intra_kernel(q_ref, k_ref, v_ref, gb_ref, tri_ref, esel_ref,
                     m_ref, n_ref, qp_ref, op_ref, gt_ref, bc_ref, *, nc, group):
    C = DN_CHUNK
    W = DN_WIDTH
    ri = lax.broadcasted_iota(jnp.int32, (C, W), 0)
    lj = lax.broadcasted_iota(jnp.int32, (C, W), 1) % C
    eye = ri == lj
    eye_f = jnp.where(eye, 1.0, 0.0)
    masks = ((ri >= lj, ri > lj), (ri <= lj, ri < lj))
    lo = lax.broadcasted_iota(jnp.int32, (C, LANES), 1) < HEAD_DIM
    col = lax.broadcasted_iota(jnp.int32, (C, LANES), 1)
    is_bwd = (col >= DN_HEADS) & (col < 2 * DN_HEADS)
    is_beta = col >= 2 * DN_HEADS
    tri = tri_ref[...]

    srcs = []
    for c in range(nc):
        gbv = gb_ref[0, c * C:(c + 1) * C, :]
        gc = _dot_exact_rhs(tri, gbv)
        srcs.append(jnp.where(is_beta, gbv, jnp.where(is_bwd, gc[C - 1:C, :] - gc + gbv, gc)))
    bc_ref[...] = _dot_exact_lhs(jnp.concatenate(srcs, axis=0), esel_ref[...])

    def pmm(x, y):
        return [_dot(xh, _bd2(yh, lo)) for xh, yh in zip(_halves(x), _halves(y))]

    def cat(parts):
        return jnp.concatenate(parts, axis=1)

    def body(gi, carry):
        cis = [gi * group + t for t in range(group)]
        sls = [pl.ds(pl.multiple_of(ci * C, C), C) for ci in cis]
        qb = [q_ref[0, sl, :] for sl in sls]
        kb = [k_ref[0, sl, :] for sl in sls]
        vb = [v_ref[0, sl, :] for sl in sls]
        gram = [cat([_dot_nt(xh, _bd2(kh, lo)) for xh, kh in zip(_halves(jnp.concatenate([k, q], axis=0)), _halves(k))])
                for k, q in zip(kb, qb)]
        inst = [(t, d) for t in range(group) for d in range(2)]
        gq = [bc_ref[sls[t], d * W:(d + 1) * W] for t, d in inst]
        bq = [bc_ref[sls[t], (2 + d) * W:(3 + d) * W] for t, d in inst]
        glast = [g[C - 1:C, :] if d == 0 else g[0:1, :] for g, (t, d) in zip(gq, inst)]
        r = [jnp.sum(jnp.where(eye, g, 0.0), axis=0, keepdims=True) for g in gq]
        dec = [jnp.where(masks[d][0], jnp.exp(jnp.minimum(g - rr, 0.0)), 0.0) for g, rr, (t, d) in zip(gq, r, inst)]
        nn = [jnp.where(masks[d][1], -(gram[t][0:C] * b * dc), 0.0) for b, dc, (t, d) in zip(bq, dec, inst)]
        tm = [eye_f + a for a in nn]
        nb_ = [a.astype(bf16) for a in nn]
        x = [cat(pmm(a, a)) for a in nb_]
        for s_ in range(5):
            xb = [a.astype(bf16) for a in x]
            if s_ < 4:
                rr = [cat(pmm(jnp.concatenate([a.astype(bf16), b_], axis=0), b_)) for a, b_ in zip(tm, xb)]
                tm = [a + b_[0:C] for a, b_ in zip(tm, rr)]
                x = [b_[C:2 * C] for b_ in rr]
            else:
                tm = [a + cat(pmm(a.astype(bf16), b_)) for a, b_ in zip(tm, xb)]
        tb = [a.astype(bf16) for a in tm]
        eg = [jnp.exp(g) for g in gq]
        vbeta = [(vb[t].astype(f32) * b).astype(bf16) for b, (t, d) in zip(bq, inst)]
        kbg = [(kb[t].astype(f32) * b * e).astype(bf16) for b, e, (t, d) in zip(bq, eg, inst)]
        uw = [[_dot(th, jnp.concatenate([_bd2(vh, lo), _bd2(kh, lo)], axis=1)).astype(bf16)
               for th, vh, kh in zip(_halves(a), _halves(v_), _halves(k_))]
              for a, v_, k_ in zip(tb, vbeta, kbg)]
        kd = [(kb[t].astype(f32) * jnp.exp(gl - g)).astype(bf16) for g, gl, (t, d) in zip(gq, glast, inst)]
        fm = [[lax.dot_general(kh, uwp, TN, preferred_element_type=f32) for kh, uwp in zip(_halves(k_), uw_)]
              for k_, uw_ in zip(kd, uw)]
        qkb = [(gram[t][C:2 * C] * dc).astype(bf16) for dc, (t, d) in zip(dec, inst)]
        qo = [[_dot(qh, jnp.concatenate([_bd2(uwp[:, 0:LANES], lo), _bd2(uwp[:, LANES:2 * LANES], lo)], axis=1))
               for qh, uwp in zip(_halves(a), uw_)] for a, uw_ in zip(qkb, uw)]
        for idx, (t, d) in enumerate(inst):
            ci, sl = cis[t], sls[t]
            f0, f1 = fm[idx]
            n_ref[0, d, ci] = cat([jnp.where(lo, f[0:C, 0:LANES], f[C:2 * C, 0:LANES]) for f in (f0, f1)])
            m_ref[0, d, ci] = cat([-jnp.where(lo, f[0:C, LANES:2 * LANES], f[C:2 * C, LANES:2 * LANES])
                                   for f in (f0, f1)]).astype(bf16)
            qd = qb[t].astype(f32) * eg[idx]
            qp_ref[0, d, sl, :] = (qd - cat([p_[:, LANES:2 * LANES] for p_ in qo[idx]])).astype(bf16)
            op_ref[0, d, sl, :] = cat([p_[:, 0:LANES] for p_ in qo[idx]])
            gt_ref[0, d, ci] = jnp.exp(glast[idx])
        return carry

    lax.fori_loop(0, nc // group, body, 0)


def _dn_intra(dq, dk, dv, gb, p, tb, group=2):
    B, L, _ = dq.shape
    nc = tb // DN_CHUNK
    nchunk = L // DN_CHUNK
    tok = lambda w: pl.BlockSpec((1, tb, w), lambda b, i: (b, i, 0))
    full = lambda a: pl.BlockSpec(a.shape, lambda b, i: (0,) * a.ndim)
    consts = [p['tri'], p['esel']]
    return pl.pallas_call(
        functools.partial(_dn_intra_kernel, nc=nc, group=group),
        grid=(B, L // tb),
        in_specs=[tok(DN_WIDTH), tok(DN_WIDTH), tok(DN_WIDTH), tok(LANES)] + [full(a) for a in consts],
        out_specs=[pl.BlockSpec((1, 2, nc, DN_CHUNK, DN_WIDTH), lambda b, i: (b, 0, i, 0, 0)),
                   pl.BlockSpec((1, 2, nc, DN_CHUNK, DN_WIDTH), lambda b, i: (b, 0, i, 0, 0)),
                   pl.BlockSpec((1, 2, tb, DN_WIDTH), lambda b, i: (b, 0, i, 0)),
                   pl.BlockSpec((1, 2, tb, DN_WIDTH), lambda b, i: (b, 0, i, 0)),
                   pl.BlockSpec((1, 2, nc, 1, DN_WIDTH), lambda b, i: (b, 0, i, 0, 0))],
        out_shape=[jax.ShapeDtypeStruct((B, 2, nchunk, DN_CHUNK, DN_WIDTH), bf16),
                   jax.ShapeDtypeStruct((B, 2, nchunk, DN_CHUNK, DN_WIDTH), f32),
                   jax.ShapeDtypeStruct((B, 2, L, DN_WIDTH), bf16),
                   jax.ShapeDtypeStruct((B, 2, L, DN_WIDTH), f32),
                   jax.ShapeDtypeStruct((B, 2, nchunk, 1, DN_WIDTH), f32)],
        scratch_shapes=[pltpu.VMEM((tb, 4 * DN_WIDTH), f32)],
        compiler_params=pltpu.CompilerParams(dimension_semantics=("parallel", "parallel"),
                                             vmem_limit_bytes=VMEM_LIMIT),
        name="dn_intra",
    )(dq, dk, dv, gb, *consts)


def _dn_scan_kernel(mf_ref, nf_ref, qf_ref, of_ref, gf_ref, mb_ref, nb_ref, qb_ref, ob_ref, gbk_ref,
                    yf_ref, yb_ref, s_ref, *, nc):
    C = DN_CHUNK

    @pl.when(pl.program_id(1) == 0)
    def _():
        s_ref[...] = jnp.zeros_like(s_ref)

    lo = lax.broadcasted_iota(jnp.int32, (C, LANES), 1) < HEAD_DIM
    dirs = ((mf_ref, nf_ref, qf_ref, of_ref, gf_ref, yf_ref), (mb_ref, nb_ref, qb_ref, ob_ref, gbk_ref, yb_ref))

    def step(j, carry):
        cis = (j, nc - 1 - j)
        sls = [pl.ds(pl.multiple_of(ci * C, C), C) for ci in cis]
        inst = [(d, h) for d in range(2) for h in range(2)]
        hs = lambda a, h: a[:, h * LANES:(h + 1) * LANES]
        m = [dirs[d][0][0, 0, cis[d]] for d in range(2)]
        q = [dirs[d][2][0, 0, sls[d], :] for d in range(2)]
        s = [s_ref[d, h] for d, h in inst]
        lhs = [jnp.concatenate([_bd2(hs(m[d], h), lo), hs(q[d], h)], axis=0) for d, h in inst]
        rr = [_dot(a, b_.astype(bf16)) for a, b_ in zip(lhs, s)]
        for idx, (d, h) in enumerate(inst):
            gt = dirs[d][4][0, 0, cis[d]]
            nn = dirs[d][1][0, 0, cis[d]]
            s_ref[d, h] = s[idx] * hs(gt, h) + rr[idx][0:2 * C] + _bd2(hs(nn, h), lo)
        for d in range(2):
            y = jnp.concatenate([rr[2 * d][2 * C:3 * C], rr[2 * d + 1][2 * C:3 * C]], axis=1)
            dirs[d][5][0, sls[d], :] = y + dirs[d][3][0, 0, sls[d], :]
        return carry

    lax.fori_loop(0, nc, step, 0)


def _dn_scan(m, n, qp, op, gt, tb):
    B, _, nchunk, C, W = m.shape
    L = nchunk * C
    nc = tb // C
    ns = L // tb
    fwd5 = lambda last: pl.BlockSpec((1, 1, nc, last, W), lambda b, i: (b, 0, i, 0, 0))
    bwd5 = lambda last: pl.BlockSpec((1, 1, nc, last, W), lambda b, i: (b, 1, ns - 1 - i, 0, 0))
    fwd4 = pl.BlockSpec((1, 1, tb, W), lambda b, i: (b, 0, i, 0))
    bwd4 = pl.BlockSpec((1, 1, tb, W), lambda b, i: (b, 1, ns - 1 - i, 0))
    return pl.pallas_call(
        functools.partial(_dn_scan_kernel, nc=nc),
        grid=(B, ns),
        in_specs=[fwd5(C), fwd5(C), fwd4, fwd4, fwd5(1), bwd5(C), bwd5(C), bwd4, bwd4, bwd5(1)],
        out_specs=[pl.BlockSpec((1, tb, W), lambda b, i: (b, i, 0)),
                   pl.BlockSpec((1, tb, W), lambda b, i: (b, ns - 1 - i, 0))],
        out_shape=[jax.ShapeDtypeStruct((B, L, W), f32), jax.ShapeDtypeStruct((B, L, W), f32)],
        scratch_shapes=[pltpu.VMEM((2, 2, 2 * C, LANES), f32)],
        compiler_params=pltpu.CompilerParams(dimension_semantics=("parallel", "arbitrary"),
                                             vmem_limit_bytes=VMEM_LIMIT),
        name="dn_scan",
    )(m, n, qp, op, gt, m, n, qp, op, gt)


def _outproj_kernel(x_ref, ya_ref, ym_ref, yf_ref, yb_ref, dz_ref, lig_ref, lib_ref, lg_ref, lb_ref,
                    ng_ref, wa_ref, wd_ref, wm_ref, ones_ref, o_ref):
    xln = _layer_norm(x_ref[0], lig_ref[...], lib_ref[...])
    o = yf_ref[0] + yb_ref[0]
    sq = o * o
    hi = sq.astype(bf16)
    lo = (sq - hi.astype(f32)).astype(bf16)
    ones = ones_ref[...]
    ms = (_dot(hi, ones) + _dot(lo, ones)) * (1.0 / HEAD_DIM)
    yd = o * lax.rsqrt(ms + RMS_EPS) * ng_ref[...] * _silu(dz_ref[0].astype(f32))
    y = _dot(ya_ref[0], wa_ref[...]) + _dot(yd.astype(bf16), wd_ref[...]) + _dot(ym_ref[0], wm_ref[...])
    o_ref[0] = _layer_norm(DEEPNORM_ALPHA * xln + y, lg_ref[...], lb_ref[...])


def _outproj(x, ya, ym, yf, yb, dz, p, tm):
    B, L, _ = x.shape
    tok = lambda w: pl.BlockSpec((1, tm, w), lambda b, i: (b, i, 0))
    full = lambda a: pl.BlockSpec(a.shape, lambda b, i: (0,) * a.ndim)
    consts = [p['ln_in_g'], p['ln_in_b'], p['ln_g'], p['ln_b'], p['norm_g'], p['wo_attn'], p['wo_dn'], p['wo_mem'],
              p['ones_bd']]
    return pl.pallas_call(
        _outproj_kernel,
        grid=(B, L // tm),
        in_specs=[tok(D_MODEL), tok(ATTN_WIDTH), tok(MEM_WIDTH), tok(DN_WIDTH), tok(DN_WIDTH), tok(DN_WIDTH)]
                 + [full(a) for a in consts],
        out_specs=tok(D_MODEL),
        out_shape=jax.ShapeDtypeStruct((B, L, D_MODEL), f32),
        compiler_params=pltpu.CompilerParams(dimension_semantics=("parallel", "parallel"),
                                             vmem_limit_bytes=VMEM_LIMIT),
        name="outproj",
    )(x, ya, ym, yf, yb, dz, *consts)


def _t5_bucket(rel):
    nb = REL_BUCKETS // 2
    max_exact = nb // 2
    n = jnp.abs(rel)
    large = max_exact + (jnp.log(jnp.maximum(n, 1).astype(f32) / max_exact)
                         / math.log(REL_MAX_DIST / max_exact) * (nb - max_exact)).astype(jnp.int32)
    large = jnp.minimum(large, nb - 1)
    return jnp.where(rel > 0, nb, 0) + jnp.where(n < max_exact, n, large)


def _prepare(ln_in_g, ln_in_b, rel_bias, w_in, attn_sink, dn_conv, dn_A_log, dn_dt_bias, dn_norm_g,
             w_mem_kv, w_out, ln_g, ln_b):
    offs = np.cumsum([0, ATTN_WIDTH, ATTN_KV_WIDTH, ATTN_KV_WIDTH, ATTN_WIDTH, DN_WIDTH, DN_WIDTH, DN_WIDTH,
                      DN_WIDTH, 2 * DN_HEADS, 2 * DN_HEADS, MEM_WIDTH, MEM_WIDTH])
    (w_aq, w_ak, w_av, w_az, w_dq, w_dk, w_dv, w_dz, w_da, w_db, w_mq, w_mz) = [
        w_in[:, int(offs[j]):int(offs[j + 1])] for j in range(12)]
    order = [kv * ATTN_GROUP + c for c in range(ATTN_GROUP) for kv in range(ATTN_KV_HEADS)]
    head_cols = lambda w: jnp.concatenate([w[:, h * HEAD_DIM:(h + 1) * HEAD_DIM] for h in order], axis=1)
    scale = HEAD_DIM ** -0.5
    p = {}
    p['w_attn'] = jnp.concatenate([head_cols(w_aq) * scale, w_ak, w_av, head_cols(w_az)], axis=1).astype(bf16)
    p['w_mem'] = jnp.concatenate([w_mq * scale, w_mz], axis=1).astype(bf16)
    p['w_dn'] = jnp.concatenate([w_dq, w_dk, w_dv, w_dz], axis=1).astype(bf16)
    p['w_ab'] = jnp.pad(jnp.concatenate([w_da, w_db], axis=1), ((0, 0), (0, LANES - 4 * DN_HEADS))).astype(bf16)
    p['conv'] = jnp.pad(dn_conv, ((0, SUBLANES - DN_CONV), (0, 0)))
    p['alog'] = jnp.pad(dn_A_log.reshape(1, 2 * DN_HEADS), ((0, 0), (0, LANES - 2 * DN_HEADS)))
    p['dtb'] = jnp.pad(dn_dt_bias.reshape(1, 2 * DN_HEADS), ((0, 0), (0, LANES - 2 * DN_HEADS)))
    p['ln_in_g'] = ln_in_g.reshape(1, D_MODEL)
    p['ln_in_b'] = ln_in_b.reshape(1, D_MODEL)
    p['ln_g'] = ln_g.reshape(1, D_MODEL)
    p['ln_b'] = ln_b.reshape(1, D_MODEL)
    p['norm_g'] = jnp.tile(dn_norm_g.reshape(1, HEAD_DIM), (1, DN_HEADS))
    p['w_memkv'] = w_mem_kv.astype(bf16)
    p['wo_attn'] = jnp.concatenate([w_out[h * HEAD_DIM:(h + 1) * HEAD_DIM] for h in order], axis=0).astype(bf16)
    p['wo_dn'] = w_out[ATTN_WIDTH:ATTN_WIDTH + DN_WIDTH].astype(bf16)
    p['wo_mem'] = w_out[ATTN_WIDTH + DN_WIDTH:].astype(bf16)
    p['sink'] = jnp.stack([attn_sink[h] for h in order])

    blk = np.arange(DN_WIDTH) // HEAD_DIM
    bd = (blk[:, None] == blk[None, :])
    p['ones_bd'] = jnp.asarray(bd, bf16)
    p['tri'] = jnp.asarray(np.tril(np.ones((DN_CHUNK, DN_CHUNK))), bf16)
    esel = np.zeros((LANES, 4 * DN_WIDTH), np.float32)
    for c in range(4 * DN_HEADS):
        esel[c, c * HEAD_DIM:(c + 1) * HEAD_DIM] = 1.0
    p['esel'] = jnp.asarray(esel, bf16)

    t = jnp.arange(BLOCK)[:, None]
    s = jnp.arange(3 * BLOCK)[None, :]
    rel = s - BLOCK - t
    onehot = jax.nn.one_hot(_t5_bucket(rel), REL_BUCKETS, dtype=f32)
    rb = jnp.stack([rel_bias[:, h] for h in order], axis=0)
    bias = jnp.einsum('tsk,hk->hts', onehot, rb, precision=lax.Precision.HIGHEST)
    bias = jnp.where((jnp.abs(rel) <= WINDOW)[None], bias, NEG)
    p['bias'] = bias.reshape(ATTN_HEADS * BLOCK, 3 * BLOCK)
    return p


def _trunk(x, mem, p, tm=512, tb_intra=512, tb_scan=512):
    L = x.shape[1]
    tm = min(tm, L)
    tb_intra = min(tb_intra, L)
    tb_scan = min(tb_scan, L)
    aq, ak, av, az, mq, mz, dq, dk, dv, dz, gb = _inproj(x, p, tm)
    mkv = _memkv(mem, p)
    ya, ym = _attn(aq, ak, av, az, mq, mz, mkv, p)
    m, n, qp, op, gt = _dn_intra(dq, dk, dv, gb, p, tb_intra)
    yf, yb = _dn_scan(m, n, qp, op, gt, tb_scan)
    return _outproj(x, ya, ym, yf, yb, dz, p, tm)


def kernel(x_prompt, x_sample, mem_prompt, mem_sample, ln_in_g, ln_in_b, rel_bias, w_in, attn_sink, dn_conv,
           dn_A_log, dn_dt_bias, dn_norm_g, w_mem_kv, w_out, ln_g, ln_b):
    p = _prepare(ln_in_g, ln_in_b, rel_bias, w_in[0], attn_sink[0], dn_conv[0], dn_A_log[0], dn_dt_bias[0],
                 dn_norm_g[0], w_mem_kv[0], w_out[0], ln_g[0], ln_b[0])
    return (_trunk(x_prompt, mem_prompt, p), _trunk(x_sample, mem_sample, p))
```

```python
import functools
import math

import jax
import jax.numpy as jnp
import numpy as np
from jax import lax
from jax.experimental import pallas as pl
from jax.experimental.pallas import tpu as pltpu

f32 = jnp.float32
bf16 = jnp.bfloat16

D_MODEL = 1024
HEAD_DIM = 64
ATTN_HEADS = 8
ATTN_KV_HEADS = 2
ATTN_GROUP = ATTN_HEADS // ATTN_KV_HEADS
ATTN_WIDTH = ATTN_HEADS * HEAD_DIM
ATTN_KV_WIDTH = ATTN_KV_HEADS * HEAD_DIM
WINDOW = 128
BLOCK = 128
REL_BUCKETS = 32
REL_MAX_DIST = 128
DN_HEADS = 4
DN_WIDTH = DN_HEADS * HEAD_DIM
DN_CONV = 5
DN_CONV_DIM = 3 * DN_WIDTH
DN_CHUNK = 64
MEM_HEADS = 4
MEM_WIDTH = MEM_HEADS * HEAD_DIM
DEPTH = 1
DEEPNORM_ALPHA = (2 * DEPTH) ** 0.25
LN_EPS = 1e-5
RMS_EPS = 1e-6
NEG = -1e30

LANES = 128
SUBLANES = 8
HALO = SUBLANES
VMEM_LIMIT = 56 * 1024 * 1024

NT = (((1,), (1,)), ((), ()))
TN = (((0,), (0,)), ((), ()))


def _dot(a, b):
    return jnp.dot(a, b, preferred_element_type=f32)


def _dot_nt(a, b):
    return lax.dot_general(a, b, NT, preferred_element_type=f32)


def _split3(x):
    hi = x.astype(bf16)
    r1 = x - hi.astype(f32)
    mid = r1.astype(bf16)
    lo = (r1 - mid.astype(f32)).astype(bf16)
    return hi, mid, lo


def _dot_exact_lhs(x, sel):
    hi, mid, lo = _split3(x)
    return _dot(hi, sel) + _dot(mid, sel) + _dot(lo, sel)


def _dot_exact_rhs(sel, x):
    hi, mid, lo = _split3(x)
    return _dot(sel, hi) + _dot(sel, mid) + _dot(sel, lo)


def _layer_norm(x, g, b):
    mu = jnp.mean(x, axis=-1, keepdims=True)
    xc = x - mu
    var = jnp.mean(xc * xc, axis=-1, keepdims=True)
    return xc * lax.rsqrt(var + LN_EPS) * g + b


def _silu(x):
    return x / (1.0 + jnp.exp(-x))


def _inproj_kernel(x_ref, xp_ref, xn_ref, lng_ref, lnb_ref, wattn_ref, wmem_ref, wdn_ref, wab_ref,
                   conv_ref, alog_ref, dtb_ref, ones_ref,
                   aq_ref, ak_ref, av_ref, az_ref, mq_ref, mz_ref, dq_ref, dk_ref, dv_ref, dz_ref, gb_ref,
                   hbuf, *, tm):
    i = pl.program_id(1)
    n = pl.num_programs(1)
    g = lng_ref[...]
    b = lnb_ref[...]
    xb = _layer_norm(x_ref[0], g, b).astype(bf16)

    wqkv = wdn_ref[:, 0:DN_CONV_DIM]
    hp = _dot(_layer_norm(xp_ref[0], g, b).astype(bf16), wqkv)
    hn = _dot(_layer_norm(xn_ref[0], g, b).astype(bf16), wqkv)
    hbuf[0:HALO, :] = jnp.where(i > 0, hp, 0.0)
    hbuf[HALO:HALO + tm, :] = _dot(xb, wqkv)
    hbuf[HALO + tm:HALO + tm + HALO, :] = jnp.where(i < n - 1, hn, 0.0)
    ab = _dot(xb, wab_ref[...])

    aq_ref[0] = _dot(xb, wattn_ref[:, 0:512]).astype(bf16)

    rows = tm + 2 * HALO
    half = DN_CONV // 2
    hfull = hbuf[...]
    c = conv_ref[half:half + 1, :] * hfull[HALO:HALO + tm, :]
    for j in range(DN_CONV):
        if j != half:
            c = c + conv_ref[j:j + 1, :] * pltpu.roll(hfull, (half - j) % rows, axis=0)[HALO:HALO + tm, :]

    kv = _dot(xb, wattn_ref[:, 512:768]).astype(bf16)
    ak_ref[0] = kv[:, 0:ATTN_KV_WIDTH]
    av_ref[0] = kv[:, ATTN_KV_WIDTH:2 * ATTN_KV_WIDTH]
    az_ref[0] = _dot(xb, wattn_ref[:, 768:1280]).astype(bf16)

    c = _silu(c)
    ones = ones_ref[...]

    def l2n(t, scale):
        ss = _dot((t * t).astype(bf16), ones)
        return t * (lax.rsqrt(ss + 1e-6) * scale)

    dq_ref[0] = l2n(c[:, 0:DN_WIDTH], HEAD_DIM ** -0.5).astype(bf16)
    mq_ref[0] = _dot(xb, wmem_ref[:, 0:256]).astype(bf16)
    dk_ref[0] = l2n(c[:, DN_WIDTH:2 * DN_WIDTH], 1.0).astype(bf16)
    mz_ref[0] = _dot(xb, wmem_ref[:, 256:512]).astype(bf16)
    dv_ref[0] = c[:, 2 * DN_WIDTH:3 * DN_WIDTH].astype(bf16)
    dz_ref[0] = _dot(xb, wdn_ref[:, DN_CONV_DIM:DN_CONV_DIM + DN_WIDTH]).astype(bf16)

    z = ab + dtb_ref[...]
    sp = jnp.maximum(z, 0.0) + jnp.log1p(jnp.exp(-jnp.abs(z)))
    gdec = -jnp.exp(alog_ref[...]) * sp
    beta = 1.0 / (1.0 + jnp.exp(-ab))
    lane = lax.broadcasted_iota(jnp.int32, ab.shape, 1)
    gb_ref[0] = jnp.where(lane < 2 * DN_HEADS, gdec, beta)


def _inproj(x, p, tm):
    B, L, _ = x.shape
    nt = L // tm
    r8 = tm // HALO
    full = lambda a: pl.BlockSpec(a.shape, lambda b, i: (0,) * a.ndim)
    consts = [p['ln_in_g'], p['ln_in_b'], p['w_attn'], p['w_mem'], p['w_dn'], p['w_ab'],
              p['conv'], p['alog'], p['dtb'], p['ones_bd']]
    tok = lambda w: pl.BlockSpec((1, tm, w), lambda b, i: (b, i, 0))
    widths = [512, 128, 128, 512, 256, 256, 256, 256, 256, 256]
    out_shape = [jax.ShapeDtypeStruct((B, L, w), bf16) for w in widths] + [jax.ShapeDtypeStruct((B, L, LANES), f32)]
    return pl.pallas_call(
        functools.partial(_inproj_kernel, tm=tm),
        grid=(B, nt),
        in_specs=[pl.BlockSpec((1, tm, D_MODEL), lambda b, i: (b, i, 0)),
                  pl.BlockSpec((1, HALO, D_MODEL), lambda b, i: (b, jnp.maximum(i * r8 - 1, 0), 0)),
                  pl.BlockSpec((1, HALO, D_MODEL), lambda b, i: (b, jnp.minimum((i + 1) * r8, L // HALO - 1), 0)),
                  ] + [full(a) for a in consts],
        out_specs=[tok(w) for w in widths] + [tok(LANES)],
        out_shape=out_shape,
        scratch_shapes=[pltpu.VMEM((tm + 2 * HALO, DN_CONV_DIM), f32)],
        compiler_params=pltpu.CompilerParams(dimension_semantics=("parallel", "parallel"),
                                             vmem_limit_bytes=VMEM_LIMIT),
        name="inproj",
    )(x, x, x, *consts)


def _memkv_kernel(m_ref, lng_ref, lnb_ref, w_ref, o_ref):
    mb = _layer_norm(m_ref[0], lng_ref[...], lnb_ref[...]).astype(bf16)
    o_ref[0] = _dot(mb, w_ref[...]).astype(bf16)


def _memkv(mem, p):
    B, M, _ = mem.shape
    return pl.pallas_call(
        _memkv_kernel,
        grid=(B,),
        in_specs=[pl.BlockSpec((1, M, D_MODEL), lambda b: (b, 0, 0)),
                  pl.BlockSpec((1, D_MODEL), lambda b: (0, 0)),
                  pl.BlockSpec((1, D_MODEL), lambda b: (0, 0)),
                  pl.BlockSpec((D_MODEL, 2 * MEM_WIDTH), lambda b: (0, 0))],
        out_specs=pl.BlockSpec((1, M, 2 * MEM_WIDTH), lambda b: (b, 0, 0)),
        out_shape=jax.ShapeDtypeStruct((B, M, 2 * MEM_WIDTH), bf16),
        compiler_params=pltpu.CompilerParams(dimension_semantics=("parallel",), vmem_limit_bytes=VMEM_LIMIT),
        name="memkv",
    )(mem, p['ln_in_g'], p['ln_in_b'], p['w_memkv'])


def _attn_kernel(sink_ref, aq_ref, ak_ref, av_ref, az_ref, mq_ref, mz_ref, mkv_ref, bias_ref,
                 ya_ref, ym_ref):
    i = pl.program_id(1)
    nb = pl.num_programs(1)
    ip = jnp.maximum(i - 1, 0)
    inx = jnp.minimum(i + 1, nb - 1)

    def rows(ref, blk):
        return ref[0, pl.ds(pl.multiple_of(blk * BLOCK, BLOCK), BLOCK), :]

    kcat = jnp.concatenate([rows(ak_ref, ip), rows(ak_ref, i), rows(ak_ref, inx)], axis=0)
    vcat = jnp.concatenate([rows(av_ref, ip), rows(av_ref, i), rows(av_ref, inx)], axis=0)

    q = aq_ref[0]
    lane = lax.broadcasted_iota(jnp.int32, (BLOCK, LANES), 1)
    low = lane < HEAD_DIM
    zero = jnp.zeros((BLOCK, LANES), bf16)
    parts = []
    for c in range(ATTN_GROUP):
        t = q[:, c * LANES:(c + 1) * LANES]
        parts.append(jnp.where(low, t, zero))
        parts.append(jnp.where(low, zero, t))
    qs = jnp.concatenate(parts, axis=0)
    s = _dot_nt(qs, kcat) + bias_ref[0]

    ps, rinv = [], []
    for h in range(ATTN_HEADS):
        sh = s[h * BLOCK:(h + 1) * BLOCK, :]
        sk = sink_ref[h]
        m = jnp.maximum(jnp.max(sh, axis=-1, keepdims=True), sk)
        e = jnp.exp(sh - m)
        den = jnp.sum(e, axis=-1, keepdims=True) + jnp.exp(sk - m)
        ps.append(e.astype(bf16))
        rinv.append(1.0 / den)
    o = _dot(jnp.concatenate(ps, axis=0), vcat)
    az = az_ref[0].astype(f32)
    for c in range(ATTN_GROUP):
        o0 = o[(2 * c) * BLOCK:(2 * c + 1) * BLOCK, :] * rinv[2 * c]
        o1 = o[(2 * c + 1) * BLOCK:(2 * c + 2) * BLOCK, :] * rinv[2 * c + 1]
        y = jnp.where(low, o0, o1) * _silu(az[:, c * LANES:(c + 1) * LANES])
        ya_ref[0, :, c * LANES:(c + 1) * LANES] = y.astype(bf16)

    mq = mq_ref[0]
    lane_m = lax.broadcasted_iota(jnp.int32, (BLOCK, MEM_WIDTH), 1) // HEAD_DIM
    zm = jnp.zeros((BLOCK, MEM_WIDTH), bf16)
    mqs = jnp.concatenate([jnp.where(lane_m == h, mq, zm) for h in range(MEM_HEADS)], axis=0)
    mk = mkv_ref[0, :, 0:MEM_WIDTH]
    mv = mkv_ref[0, :, MEM_WIDTH:2 * MEM_WIDTH]
    sm = _dot_nt(mqs, mk)
    mm = jnp.max(sm, axis=-1, keepdims=True)
    em = jnp.exp(sm - mm)
    rm = 1.0 / jnp.sum(em, axis=-1, keepdims=True)
    om = _dot(em.astype(bf16), mv) * rm
    ym = jnp.zeros((BLOCK, MEM_WIDTH), f32)
    for h in range(MEM_HEADS):
        ym = jnp.where(lane_m == h, om[h * BLOCK:(h + 1) * BLOCK, :], ym)
    ym_ref[0] = (ym * _silu(mz_ref[0].astype(f32))).astype(bf16)


def _attn(aq, ak, av, az, mq, mz, mkv, p):
    B, L, _ = aq.shape
    nb = L // BLOCK
    M = mkv.shape[1]
    tok = lambda w: pl.BlockSpec((1, BLOCK, w), lambda b, i: (b, i, 0))
    seq = lambda w: pl.BlockSpec((1, L, w), lambda b, i: (b, 0, 0))
    return pl.pallas_call(
        _attn_kernel,
        grid=(B, nb),
        in_specs=[pl.BlockSpec(memory_space=pltpu.SMEM),
                  tok(ATTN_WIDTH), seq(ATTN_KV_WIDTH), seq(ATTN_KV_WIDTH), tok(ATTN_WIDTH),
                  tok(MEM_WIDTH), tok(MEM_WIDTH),
                  pl.BlockSpec((1, M, 2 * MEM_WIDTH), lambda b, i: (b, 0, 0)),
                  pl.BlockSpec((1, ATTN_HEADS * BLOCK, 3 * BLOCK),
                               lambda b, i: (jnp.where(i == 0, 2, 0) + jnp.where(i == nb - 1, 1, 0), 0, 0))],
        out_specs=[tok(ATTN_WIDTH), tok(MEM_WIDTH)],
        out_shape=[jax.ShapeDtypeStruct((B, L, ATTN_WIDTH), bf16), jax.ShapeDtypeStruct((B, L, MEM_WIDTH), bf16)],
        compiler_params=pltpu.CompilerParams(dimension_semantics=("parallel", "arbitrary"),
                                             vmem_limit_bytes=VMEM_LIMIT),
        name="attn",
    )(p['sink'], aq, ak, av, az, mq, mz, mkv, p['bias'])


def _bd2(y, lo):
    z = jnp.zeros_like(y)
    return jnp.concatenate([jnp.where(lo, y, z), jnp.where(lo, z, y)], axis=0)


def _halves(x):
    return x[:, 0:LANES], x[:, LANES:2 * LANES]


def _dn_intra_kernel(q_ref, k_ref, v_ref, gb_ref, tri_ref, esel_ref,
                     m_ref, n_ref, qp_ref, op_ref, gt_ref, bc_ref, *, nc, group):
    C = DN_CHUNK
    W = DN_WIDTH
    ri = lax.broadcasted_iota(jnp.int32, (C, W), 0)
    lj = lax.broadcasted_iota(jnp.int32, (C, W), 1) % C
    eye = ri == lj
    eye_f = jnp.where(eye, 1.0, 0.0)
    masks = ((ri >= lj, ri > lj), (ri <= lj, ri < lj))
    lo = lax.broadcasted_iota(jnp.int32, (C, LANES), 1) < HEAD_DIM
    col = lax.broadcasted_iota(jnp.int32, (C, LANES), 1)
    is_bwd = (col >= DN_HEADS) & (col < 2 * DN_HEADS)
    is_beta = col >= 2 * DN_HEADS
    tri = tri_ref[...]

    srcs = []
    for c in range(nc):
        gbv = gb_ref[0, c * C:(c + 1) * C, :]
        gc = _dot_exact_rhs(tri, gbv)
        srcs.append(jnp.where(is_beta, gbv, jnp.where(is_bwd, gc[C - 1:C, :] - gc + gbv, gc)))
    hi, mid, _ = _split3(jnp.concatenate(srcs, axis=0))
    bc_ref[:, 0:2 * W] = _dot(hi, esel_ref[:, 0:2 * W]) + _dot(mid, esel_ref[:, 0:2 * W])
    bc_ref[:, 2 * W:4 * W] = _dot(hi, esel_ref[:, 2 * W:4 * W])

    def pmm(x, y):
        return [_dot(xh, _bd2(yh, lo)) for xh, yh in zip(_halves(x), _halves(y))]

    def cat(parts):
        return jnp.concatenate(parts, axis=1)

    def body(gi, carry):
        cis = [gi * group + t for t in range(group)]
        sls = [pl.ds(pl.multiple_of(ci * C, C), C) for ci in cis]
        qb = [q_ref[0, sl, :] for sl in sls]
        kb = [k_ref[0, sl, :] for sl in sls]
        vb = [v_ref[0, sl, :] for sl in sls]
        gram = [cat([_dot_nt(xh, _bd2(kh, lo)) for xh, kh in zip(_halves(jnp.concatenate([k, q], axis=0)), _halves(k))])
                for k, q in zip(kb, qb)]
        inst = [(t, d) for t in range(group) for d in range(2)]
        gq = [bc_ref[sls[t], d * W:(d + 1) * W] for t, d in inst]
        bq = [bc_ref[sls[t], (2 + d) * W:(3 + d) * W] for t, d in inst]
        glast = [g[C - 1:C, :] if d == 0 else g[0:1, :] for g, (t, d) in zip(gq, inst)]
        r = [jnp.sum(jnp.where(eye, g, 0.0), axis=0, keepdims=True) for g in gq]
        dec = [jnp.where(masks[d][0], jnp.exp(jnp.minimum(g - rr, 0.0)), 0.0) for g, rr, (t, d) in zip(gq, r, inst)]
        nn = [jnp.where(masks[d][1], -(gram[t][0:C] * b * dc), 0.0) for b, dc, (t, d) in zip(bq, dec, inst)]
        tm = [eye_f + a for a in nn]
        nb_ = [a.astype(bf16) for a in nn]
        x = [cat(pmm(a, a)) for a in nb_]
        for s_ in range(5):
            xb = [a.astype(bf16) for a in x]
            if s_ < 4:
                rr = [cat(pmm(jnp.concatenate([a.astype(bf16), b_], axis=0), b_)) for a, b_ in zip(tm, xb)]
                tm = [a + b_[0:C] for a, b_ in zip(tm, rr)]
                x = [b_[C:2 * C] for b_ in rr]
            else:
                tm = [a + cat(pmm(a.astype(bf16), b_)) for a, b_ in zip(tm, xb)]
        tb = [a.astype(bf16) for a in tm]
        eg = [jnp.exp(g) for g in gq]
        vbeta = [(vb[t].astype(f32) * b).astype(bf16) for b, (t, d) in zip(bq, inst)]
        kbg = [(kb[t].astype(f32) * b * e).astype(bf16) for b, e, (t, d) in zip(bq, eg, inst)]
        uw = [[_dot(th, jnp.concatenate([_bd2(vh, lo), _bd2(kh, lo)], axis=1)).astype(bf16)
               for th, vh, kh in zip(_halves(a), _halves(v_), _halves(k_))]
              for a, v_, k_ in zip(tb, vbeta, kbg)]
        kd = [(kb[t].astype(f32) * jnp.exp(gl - g)).astype(bf16) for g, gl, (t, d) in zip(gq, glast, inst)]
        fm = [[lax.dot_general(kh, uwp, TN, preferred_element_type=f32) for kh, uwp in zip(_halves(k_), uw_)]
              for k_, uw_ in zip(kd, uw)]
        qkb = [(gram[t][C:2 * C] * dc).astype(bf16) for dc, (t, d) in zip(dec, inst)]
        qo = [[_dot(qh, jnp.concatenate([_bd2(uwp[:, 0:LANES], lo), _bd2(uwp[:, LANES:2 * LANES], lo)], axis=1))
               for qh, uwp in zip(_halves(a), uw_)] for a, uw_ in zip(qkb, uw)]
        for idx, (t, d) in enumerate(inst):
            ci, sl = cis[t], sls[t]
            f0, f1 = fm[idx]
            n_ref[0, d, ci] = cat([jnp.where(lo, f[0:C, 0:LANES], f[C:2 * C, 0:LANES])
                                   for f in (f0, f1)]).astype(bf16)
            m_ref[0, d, ci] = cat([-jnp.where(lo, f[0:C, LANES:2 * LANES], f[C:2 * C, LANES:2 * LANES])
                                   for f in (f0, f1)]).astype(bf16)
            qd = qb[t].astype(f32) * eg[idx]
            qp_ref[0, d, sl, :] = (qd - cat([p_[:, LANES:2 * LANES] for p_ in qo[idx]])).astype(bf16)
            op_ref[0, d, sl, :] = cat([p_[:, 0:LANES] for p_ in qo[idx]]).astype(bf16)
            gt_ref[0, d, ci] = jnp.exp(glast[idx])
        return carry

    lax.fori_loop(0, nc // group, body, 0)


def _dn_intra(dq, dk, dv, gb, p, tb, group=4):
    B, L, _ = dq.shape
    nc = tb // DN_CHUNK
    nchunk = L // DN_CHUNK
    tok = lambda w: pl.BlockSpec((1, tb, w), lambda b, i: (b, i, 0))
    full = lambda a: pl.BlockSpec(a.shape, lambda b, i: (0,) * a.ndim)
    consts = [p['tri'], p['esel']]
    return pl.pallas_call(
        functools.partial(_dn_intra_kernel, nc=nc, group=group),
        grid=(B, L // tb),
        in_specs=[tok(DN_WIDTH), tok(DN_WIDTH), tok(DN_WIDTH), tok(LANES)] + [full(a) for a in consts],
        out_specs=[pl.BlockSpec((1, 2, nc, DN_CHUNK, DN_WIDTH), lambda b, i: (b, 0, i, 0, 0)),
                   pl.BlockSpec((1, 2, nc, DN_CHUNK, DN_WIDTH), lambda b, i: (b, 0, i, 0, 0)),
                   pl.BlockSpec((1, 2, tb, DN_WIDTH), lambda b, i: (b, 0, i, 0)),
                   pl.BlockSpec((1, 2, tb, DN_WIDTH), lambda b, i: (b, 0, i, 0)),
                   pl.BlockSpec((1, 2, nc, 1, DN_WIDTH), lambda b, i: (b, 0, i, 0, 0))],
        out_shape=[jax.ShapeDtypeStruct((B, 2, nchunk, DN_CHUNK, DN_WIDTH), bf16),
                   jax.ShapeDtypeStruct((B, 2, nchunk, DN_CHUNK, DN_WIDTH), bf16),
                   jax.ShapeDtypeStruct((B, 2, L, DN_WIDTH), bf16),
                   jax.ShapeDtypeStruct((B, 2, L, DN_WIDTH), bf16),
                   jax.ShapeDtypeStruct((B, 2, nchunk, 1, DN_WIDTH), f32)],
        scratch_shapes=[pltpu.VMEM((tb, 4 * DN_WIDTH), f32)],
        compiler_params=pltpu.CompilerParams(dimension_semantics=("parallel", "parallel"),
                                             vmem_limit_bytes=VMEM_LIMIT),
        name="dn_intra",
    )(dq, dk, dv, gb, *consts)


def _dn_scan_kernel(mf_ref, nf_ref, qf_ref, of_ref, gf_ref, mb_ref, nb_ref, qb_ref, ob_ref, gbk_ref,
                    yf_ref, yb_ref, s_ref, *, nc):
    C = DN_CHUNK

    @pl.when(pl.program_id(1) == 0)
    def _():
        s_ref[...] = jnp.zeros_like(s_ref)

    lo = lax.broadcasted_iota(jnp.int32, (C, LANES), 1) < HEAD_DIM
    dirs = ((mf_ref, nf_ref, qf_ref, of_ref, gf_ref, yf_ref), (mb_ref, nb_ref, qb_ref, ob_ref, gbk_ref, yb_ref))

    def step(j, carry):
        cis = (j, nc - 1 - j)
        sls = [pl.ds(pl.multiple_of(ci * C, C), C) for ci in cis]
        inst = [(d, h) for d in range(2) for h in range(2)]
        hs = lambda a, h: a[:, h * LANES:(h + 1) * LANES]
        m = [dirs[d][0][0, 0, cis[d]] for d in range(2)]
        q = [dirs[d][2][0, 0, sls[d], :] for d in range(2)]
        s = [s_ref[d, h] for d, h in inst]
        lhs = [jnp.concatenate([_bd2(hs(m[d], h), lo), hs(q[d], h)], axis=0) for d, h in inst]
        rr = [_dot(a, b_.astype(bf16)) for a, b_ in zip(lhs, s)]
        for idx, (d, h) in enumerate(inst):
            gt = dirs[d][4][0, 0, cis[d]]
            nn = dirs[d][1][0, 0, cis[d]]
            s_ref[d, h] = s[idx] * hs(gt, h) + rr[idx][0:2 * C] + _bd2(hs(nn, h), lo).astype(f32)
        for d in range(2):
            y = jnp.concatenate([rr[2 * d][2 * C:3 * C], rr[2 * d + 1][2 * C:3 * C]], axis=1)
            dirs[d][5][0, sls[d], :] = (y + dirs[d][3][0, 0, sls[d], :].astype(f32)).astype(bf16)
        return carry

    lax.fori_loop(0, nc, step, 0)


def _dn_scan(m, n, qp, op, gt, tb):
    B, _, nchunk, C, W = m.shape
    L = nchunk * C
    nc = tb // C
    ns = L // tb
    fwd5 = lambda last: pl.BlockSpec((1, 1, nc, last, W), lambda b, i: (b, 0, i, 0, 0))
    bwd5 = lambda last: pl.BlockSpec((1, 1, nc, last, W), lambda b, i: (b, 1, ns - 1 - i, 0, 0))
    fwd4 = pl.BlockSpec((1, 1, tb, W), lambda b, i: (b, 0, i, 0))
    bwd4 = pl.BlockSpec((1, 1, tb, W), lambda b, i: (b, 1, ns - 1 - i, 0))
    return pl.pallas_call(
        functools.partial(_dn_scan_kernel, nc=nc),
        grid=(B, ns),
        in_specs=[fwd5(C), fwd5(C), fwd4, fwd4, fwd5(1), bwd5(C), bwd5(C), bwd4, bwd4, bwd5(1)],
        out_specs=[pl.BlockSpec((1, tb, W), lambda b, i: (b, i, 0)),
                   pl.BlockSpec((1, tb, W), lambda b, i: (b, ns - 1 - i, 0))],
        out_shape=[jax.ShapeDtypeStruct((B, L, W), bf16), jax.ShapeDtypeStruct((B, L, W), bf16)],
        scratch_shapes=[pltpu.VMEM((2, 2, 2 * C, LANES), f32)],
        compiler_params=pltpu.CompilerParams(dimension_semantics=("parallel", "arbitrary"),
                                             vmem_limit_bytes=VMEM_LIMIT),
        name="dn_scan",
    )(m, n, qp, op, gt, m, n, qp, op, gt)


def _outproj_kernel(x_ref, ya_ref, ym_ref, yf_ref, yb_ref, dz_ref, lig_ref, lib_ref, lg_ref, lb_ref,
                    ng_ref, wa_ref, wd_ref, wm_ref, ones_ref, o_ref):
    xa = _layer_norm(x_ref[0], lig_ref[...], lib_ref[...])
    o = yf_ref[0].astype(f32) + yb_ref[0].astype(f32)
    ms = _dot((o * o).astype(bf16), ones_ref[...]) * (1.0 / HEAD_DIM)
    yd = o * lax.rsqrt(ms + RMS_EPS) * ng_ref[...] * _silu(dz_ref[0].astype(f32))
    y = _dot(ya_ref[0], wa_ref[...]) + _dot(yd.astype(bf16), wd_ref[...]) + _dot(ym_ref[0], wm_ref[...])
    o_ref[0] = _layer_norm(xa + y, lg_ref[...], lb_ref[...])


def _outproj(x, ya, ym, yf, yb, dz, p, tm):
    B, L, _ = x.shape
    tok = lambda w: pl.BlockSpec((1, tm, w), lambda b, i: (b, i, 0))
    full = lambda a: pl.BlockSpec(a.shape, lambda b, i: (0,) * a.ndim)
    consts = [p['ln_in_ga'], p['ln_in_ba'], p['ln_g'], p['ln_b'], p['norm_g'], p['wo_attn'], p['wo_dn'], p['wo_mem'],
              p['ones_bd']]
    return pl.pallas_call(
        _outproj_kernel,
        grid=(B, L // tm),
        in_specs=[tok(D_MODEL), tok(ATTN_WIDTH), tok(MEM_WIDTH), tok(DN_WIDTH), tok(DN_WIDTH), tok(DN_WIDTH)]
                 + [full(a) for a in consts],
        out_specs=tok(D_MODEL),
        out_shape=jax.ShapeDtypeStruct((B, L, D_MODEL), f32),
        compiler_params=pltpu.CompilerParams(dimension_semantics=("parallel", "parallel"),
                                             vmem_limit_bytes=VMEM_LIMIT),
        name="outproj",
    )(x, ya, ym, yf, yb, dz, *consts)


def _t5_bucket(rel):
    nb = REL_BUCKETS // 2
    max_exact = nb // 2
    n = jnp.abs(rel)
    large = max_exact + (jnp.log(jnp.maximum(n, 1).astype(f32) / max_exact)
                         / math.log(REL_MAX_DIST / max_exact) * (nb - max_exact)).astype(jnp.int32)
    large = jnp.minimum(large, nb - 1)
    return jnp.where(rel > 0, nb, 0) + jnp.where(n < max_exact, n, large)


def _prepare(ln_in_g, ln_in_b, rel_bias, w_in, attn_sink, dn_conv, dn_A_log, dn_dt_bias, dn_norm_g,
             w_mem_kv, w_out, ln_g, ln_b):
    offs = np.cumsum([0, ATTN_WIDTH, ATTN_KV_WIDTH, ATTN_KV_WIDTH, ATTN_WIDTH, DN_WIDTH, DN_WIDTH, DN_WIDTH,
                      DN_WIDTH, 2 * DN_HEADS, 2 * DN_HEADS, MEM_WIDTH, MEM_WIDTH])
    (w_aq, w_ak, w_av, w_az, w_dq, w_dk, w_dv, w_dz, w_da, w_db, w_mq, w_mz) = [
        w_in[:, int(offs[j]):int(offs[j + 1])] for j in range(12)]
    order = [kv * ATTN_GROUP + c for c in range(ATTN_GROUP) for kv in range(ATTN_KV_HEADS)]
    head_cols = lambda w: jnp.concatenate([w[:, h * HEAD_DIM:(h + 1) * HEAD_DIM] for h in order], axis=1)
    scale = HEAD_DIM ** -0.5
    p = {}
    p['w_attn'] = jnp.concatenate([head_cols(w_aq) * scale, w_ak, w_av, head_cols(w_az)], axis=1).astype(bf16)
    p['w_mem'] = jnp.concatenate([w_mq * scale, w_mz], axis=1).astype(bf16)
    p['w_dn'] = jnp.concatenate([w_dq, w_dk, w_dv, w_dz], axis=1).astype(bf16)
    p['w_ab'] = jnp.pad(jnp.concatenate([w_da, w_db], axis=1), ((0, 0), (0, LANES - 4 * DN_HEADS))).astype(bf16)
    p['conv'] = jnp.pad(dn_conv, ((0, SUBLANES - DN_CONV), (0, 0)))
    p['alog'] = jnp.pad(dn_A_log.reshape(1, 2 * DN_HEADS), ((0, 0), (0, LANES - 2 * DN_HEADS)))
    p['dtb'] = jnp.pad(dn_dt_bias.reshape(1, 2 * DN_HEADS), ((0, 0), (0, LANES - 2 * DN_HEADS)))
    p['ln_in_g'] = ln_in_g.reshape(1, D_MODEL)
    p['ln_in_b'] = ln_in_b.reshape(1, D_MODEL)
    p['ln_in_ga'] = DEEPNORM_ALPHA * p['ln_in_g']
    p['ln_in_ba'] = DEEPNORM_ALPHA * p['ln_in_b']
    p['ln_g'] = ln_g.reshape(1, D_MODEL)
    p['ln_b'] = ln_b.reshape(1, D_MODEL)
    p['norm_g'] = jnp.tile(dn_norm_g.reshape(1, HEAD_DIM), (1, DN_HEADS))
    p['w_memkv'] = w_mem_kv.astype(bf16)
    p['wo_attn'] = jnp.concatenate([w_out[h * HEAD_DIM:(h + 1) * HEAD_DIM] for h in order], axis=0).astype(bf16)
    p['wo_dn'] = w_out[ATTN_WIDTH:ATTN_WIDTH + DN_WIDTH].astype(bf16)
    p['wo_mem'] = w_out[ATTN_WIDTH + DN_WIDTH:].astype(bf16)
    p['sink'] = jnp.stack([attn_sink[h] for h in order])

    blk = np.arange(DN_WIDTH) // HEAD_DIM
    bd = (blk[:, None] == blk[None, :])
    p['ones_bd'] = jnp.asarray(bd, bf16)
    p['tri'] = jnp.asarray(np.tril(np.ones((DN_CHUNK, DN_CHUNK))), bf16)
    esel = np.zeros((LANES, 4 * DN_WIDTH), np.float32)
    for c in range(4 * DN_HEADS):
        esel[c, c * HEAD_DIM:(c + 1) * HEAD_DIM] = 1.0
    p['esel'] = jnp.asarray(esel, bf16)

    t = jnp.arange(BLOCK)[:, None]
    s = jnp.arange(3 * BLOCK)[None, :]
    rel = s - BLOCK - t
    onehot = jax.nn.one_hot(_t5_bucket(rel), REL_BUCKETS, dtype=f32)
    rb = jnp.stack([rel_bias[:, h] for h in order], axis=0)
    bias = jnp.einsum('tsk,hk->hts', onehot, rb, precision=lax.Precision.HIGHEST)
    bias = jnp.where((jnp.abs(rel) <= WINDOW)[None], bias, NEG)
    bias = bias.reshape(ATTN_HEADS * BLOCK, 3 * BLOCK)
    no_prev = jnp.where(s < BLOCK, NEG, 0.0)
    no_next = jnp.where(s >= 2 * BLOCK, NEG, 0.0)
    p['bias'] = jnp.stack([bias, bias + no_next, bias + no_prev, bias + no_prev + no_next])
    return p


def _trunk(x, mem, p, tm=512, tb_intra=512, tb_scan=512):
    L = x.shape[1]
    tm = min(tm, L)
    tb_intra = min(tb_intra, L)
    tb_scan = min(tb_scan, L)
    aq, ak, av, az, mq, mz, dq, dk, dv, dz, gb = _inproj(x, p, tm)
    mkv = _memkv(mem, p)
    ya, ym = _attn(aq, ak, av, az, mq, mz, mkv, p)
    m, n, qp, op, gt = _dn_intra(dq, dk, dv, gb, p, tb_intra)
    yf, yb = _dn_scan(m, n, qp, op, gt, tb_scan)
    return _outproj(x, ya, ym, yf, yb, dz, p, tm)


def kernel(x_prompt, x_sample, mem_prompt, mem_sample, ln_in_g, ln_in_b, rel_bias, w_in, attn_sink, dn_conv,
           dn_A_log, dn_dt_bias, dn_norm_g, w_mem_kv, w_out, ln_g, ln_b):
    p = _prepare(ln_in_g, ln_in_b, rel_bias, w_in[0], attn_sink[0], dn_conv[0], dn_A_log[0], dn_dt_bias[0],
                 dn_norm_g[0], w_mem_kv[0], w_out[0], ln_g[0], ln_b[0])
    return (_trunk(x_prompt, mem_prompt, p), _trunk(x_sample, mem_sample, p))
```

```python
import functools
import math

import jax
import jax.numpy as jnp
import numpy as np
from jax import lax
from jax.experimental import pallas as pl
from jax.experimental.pallas import tpu as pltpu

f32 = jnp.float32
bf16 = jnp.bfloat16

D_MODEL = 1024
HEAD_DIM = 64
ATTN_HEADS = 8
ATTN_KV_HEADS = 2
ATTN_GROUP = ATTN_HEADS // ATTN_KV_HEADS
ATTN_WIDTH = ATTN_HEADS * HEAD_DIM
ATTN_KV_WIDTH = ATTN_KV_HEADS * HEAD_DIM
WINDOW = 128
BLOCK = 128
REL_BUCKETS = 32
REL_MAX_DIST = 128
DN_HEADS = 4
DN_WIDTH = DN_HEADS * HEAD_DIM
DN_CONV = 5
DN_CONV_DIM = 3 * DN_WIDTH
DN_CHUNK = 64
MEM_HEADS = 4
MEM_WIDTH = MEM_HEADS * HEAD_DIM
DEPTH = 1
DEEPNORM_ALPHA = (2 * DEPTH) ** 0.25
LN_EPS = 1e-5
RMS_EPS = 1e-6
NEG = -1e30
LOG2E = 1.4426950408889634

LANES = 128
SUBLANES = 8
HALO = SUBLANES
VMEM_LIMIT = 56 * 1024 * 1024

NT = (((1,), (1,)), ((), ()))
TN = (((0,), (0,)), ((), ()))


def _dot(a, b):
    return jnp.dot(a, b, preferred_element_type=f32)


def _dot_nt(a, b):
    return lax.dot_general(a, b, NT, preferred_element_type=f32)


def _split3(x):
    hi = x.astype(bf16)
    r1 = x - hi.astype(f32)
    mid = r1.astype(bf16)
    lo = (r1 - mid.astype(f32)).astype(bf16)
    return hi, mid, lo


def _dot_exact_lhs(x, sel):
    hi, mid, lo = _split3(x)
    return _dot(hi, sel) + _dot(mid, sel) + _dot(lo, sel)


def _dot_exact_rhs(sel, x):
    hi, mid, lo = _split3(x)
    return _dot(sel, hi) + _dot(sel, mid) + _dot(sel, lo)


def _layer_norm(x, g, b):
    mu = jnp.mean(x, axis=-1, keepdims=True)
    xc = x - mu
    var = jnp.mean(xc * xc, axis=-1, keepdims=True)
    return xc * lax.rsqrt(var + LN_EPS) * g + b


def _silu(x):
    return x / (1.0 + jnp.exp(-x))


def _inproj_kernel(x_ref, xp_ref, xn_ref, lng_ref, lnb_ref, wattn_ref, wmem_ref, wdn_ref, wab_ref,
                   conv_ref, alog_ref, dtb_ref, ones_ref,
                   aq_ref, ak_ref, av_ref, az_ref, mq_ref, mz_ref, dq_ref, dk_ref, dv_ref, dz_ref, gb_ref,
                   hbuf, *, tm):
    i = pl.program_id(1)
    n = pl.num_programs(1)
    g = lng_ref[...]
    b = lnb_ref[...]
    xb = _layer_norm(x_ref[0], g, b).astype(bf16)

    wqkv = wdn_ref[:, 0:DN_CONV_DIM]
    hp = _dot(_layer_norm(xp_ref[0], g, b).astype(bf16), wqkv)
    hn = _dot(_layer_norm(xn_ref[0], g, b).astype(bf16), wqkv)
    hbuf[0:HALO, :] = jnp.where(i > 0, hp, 0.0)
    hbuf[HALO:HALO + tm, :] = _dot(xb, wqkv)
    hbuf[HALO + tm:HALO + tm + HALO, :] = jnp.where(i < n - 1, hn, 0.0)
    ab = _dot(xb, wab_ref[...])

    aq_ref[0] = _dot(xb, wattn_ref[:, 0:512]).astype(bf16)

    rows = tm + 2 * HALO
    half = DN_CONV // 2
    hfull = hbuf[...]
    c = conv_ref[half:half + 1, :] * hfull[HALO:HALO + tm, :]
    for j in range(DN_CONV):
        if j != half:
            c = c + conv_ref[j:j + 1, :] * pltpu.roll(hfull, (half - j) % rows, axis=0)[HALO:HALO + tm, :]

    kv = _dot(xb, wattn_ref[:, 512:768]).astype(bf16)
    ak_ref[0] = kv[:, 0:ATTN_KV_WIDTH]
    av_ref[0] = kv[:, ATTN_KV_WIDTH:2 * ATTN_KV_WIDTH]
    az_ref[0] = _dot(xb, wattn_ref[:, 768:1280]).astype(bf16)

    c = _silu(c)
    ones = ones_ref[...]

    def l2n(t, scale):
        ss = _dot((t * t).astype(bf16), ones)
        return t * (lax.rsqrt(ss + 1e-6) * scale)

    dq_ref[0] = l2n(c[:, 0:DN_WIDTH], HEAD_DIM ** -0.5).astype(bf16)
    mq_ref[0] = _dot(xb, wmem_ref[:, 0:256]).astype(bf16)
    dk_ref[0] = l2n(c[:, DN_WIDTH:2 * DN_WIDTH], 1.0).astype(bf16)
    mz_ref[0] = _dot(xb, wmem_ref[:, 256:512]).astype(bf16)
    dv_ref[0] = c[:, 2 * DN_WIDTH:3 * DN_WIDTH].astype(bf16)
    dz_ref[0] = _dot(xb, wdn_ref[:, DN_CONV_DIM:DN_CONV_DIM + DN_WIDTH]).astype(bf16)

    z = ab + dtb_ref[...]
    sp = jnp.maximum(z, 0.0) + jnp.log1p(jnp.exp(-jnp.abs(z)))
    gdec = -jnp.exp(alog_ref[...]) * sp
    beta = 1.0 / (1.0 + jnp.exp(-ab))
    lane = lax.broadcasted_iota(jnp.int32, ab.shape, 1)
    gb_ref[0] = jnp.where(lane < 2 * DN_HEADS, gdec, beta)


def _inproj(x, p, tm):
    B, L, _ = x.shape
    nt = L // tm
    r8 = tm // HALO
    full = lambda a: pl.BlockSpec(a.shape, lambda b, i: (0,) * a.ndim)
    consts = [p['ln_in_g'], p['ln_in_b'], p['w_attn'], p['w_mem'], p['w_dn'], p['w_ab'],
              p['conv'], p['alog'], p['dtb'], p['ones_bd']]
    tok = lambda w: pl.BlockSpec((1, tm, w), lambda b, i: (b, i, 0))
    widths = [512, 128, 128, 512, 256, 256, 256, 256, 256, 256]
    out_shape = [jax.ShapeDtypeStruct((B, L, w), bf16) for w in widths] + [jax.ShapeDtypeStruct((B, L, LANES), f32)]
    return pl.pallas_call(
        functools.partial(_inproj_kernel, tm=tm),
        grid=(B, nt),
        in_specs=[pl.BlockSpec((1, tm, D_MODEL), lambda b, i: (b, i, 0)),
                  pl.BlockSpec((1, HALO, D_MODEL), lambda b, i: (b, jnp.maximum(i * r8 - 1, 0), 0)),
                  pl.BlockSpec((1, HALO, D_MODEL), lambda b, i: (b, jnp.minimum((i + 1) * r8, L // HALO - 1), 0)),
                  ] + [full(a) for a in consts],
        out_specs=[tok(w) for w in widths] + [tok(LANES)],
        out_shape=out_shape,
        scratch_shapes=[pltpu.VMEM((tm + 2 * HALO, DN_CONV_DIM), f32)],
        compiler_params=pltpu.CompilerParams(dimension_semantics=("parallel", "parallel"),
                                             vmem_limit_bytes=VMEM_LIMIT),
        name="inproj",
    )(x, x, x, *consts)


def _memkv_kernel(m_ref, lng_ref, lnb_ref, wk_ref, wvt_ref, k_ref, vt_ref):
    mb = _layer_norm(m_ref[0], lng_ref[...], lnb_ref[...]).astype(bf16)
    k_ref[0] = _dot(mb, wk_ref[...]).astype(bf16)
    vt_ref[0] = _dot_nt(wvt_ref[...], mb).astype(bf16)


def _memkv(mem, p):
    B, M, _ = mem.shape
    return pl.pallas_call(
        _memkv_kernel,
        grid=(B,),
        in_specs=[pl.BlockSpec((1, M, D_MODEL), lambda b: (b, 0, 0)),
                  pl.BlockSpec((1, D_MODEL), lambda b: (0, 0)),
                  pl.BlockSpec((1, D_MODEL), lambda b: (0, 0)),
                  pl.BlockSpec((D_MODEL, MEM_WIDTH), lambda b: (0, 0)),
                  pl.BlockSpec((MEM_WIDTH, D_MODEL), lambda b: (0, 0))],
        out_specs=[pl.BlockSpec((1, M, MEM_WIDTH), lambda b: (b, 0, 0)),
                   pl.BlockSpec((1, MEM_WIDTH, M), lambda b: (b, 0, 0))],
        out_shape=[jax.ShapeDtypeStruct((B, M, MEM_WIDTH), bf16), jax.ShapeDtypeStruct((B, MEM_WIDTH, M), bf16)],
        compiler_params=pltpu.CompilerParams(dimension_semantics=("parallel",), vmem_limit_bytes=VMEM_LIMIT),
        name="memkv",
    )(mem, p['ln_in_g'], p['ln_in_b'], p['w_memk'], p['w_memvt'])


def _softmax_cols(s, extra=None):
    m = jnp.max(s, axis=0, keepdims=True)
    if extra is not None:
        m = jnp.maximum(m, extra)
    e = jnp.exp2(s - m)
    return e.astype(bf16), (None if extra is None else jnp.exp2(extra - m))


QB = 2


def _attn_kernel(sink_ref, aq_ref, ak_ref, av_ref, az_ref, mq_ref, mz_ref, mk_ref, mvt_ref, bias0_ref, bias1_ref,
                 ya_ref, ym_ref):
    j = pl.program_id(1)
    nb = QB * pl.num_programs(1)
    bias_refs = (bias0_ref, bias1_ref)
    lane = lax.broadcasted_iota(jnp.int32, (BLOCK, LANES), 1)
    low = lane < HEAD_DIM
    zero = jnp.zeros((BLOCK, LANES), bf16)

    def rows(ref, blk):
        return ref[0, pl.ds(pl.multiple_of(blk * BLOCK, BLOCK), BLOCK), :]

    def band(ref, i):
        return jnp.concatenate([rows(ref, jnp.maximum(i - 1, 0)), rows(ref, i), rows(ref, jnp.minimum(i + 1, nb - 1))],
                               axis=0)

    def stack_q(q):
        parts = []
        for c in range(ATTN_GROUP):
            t = q[:, c * LANES:(c + 1) * LANES]
            parts.append(jnp.where(low, t, zero))
            parts.append(jnp.where(low, zero, t))
        return jnp.concatenate(parts, axis=0)

    blks = [j * QB + t for t in range(QB)]
    kcat = [band(ak_ref, i) for i in blks]
    vcat = [band(av_ref, i) for i in blks]
    qs = [stack_q(aq_ref[0, t * BLOCK:(t + 1) * BLOCK, :]) for t in range(QB)]
    mq = mq_ref[0]
    nq = QB * BLOCK
    lane_m = lax.broadcasted_iota(jnp.int32, (nq, MEM_WIDTH), 1) // HEAD_DIM
    zm = jnp.zeros((nq, MEM_WIDTH), bf16)
    mqs = jnp.concatenate([jnp.where(lane_m == h, mq, zm) for h in range(MEM_HEADS)], axis=0)

    st = [_dot_nt(k, q) + b[0] for k, q, b in zip(kcat, qs, bias_refs)]
    smt = _dot_nt(mk_ref[0], mqs)
    ones_v = jnp.ones((2 * SUBLANES, 3 * BLOCK), bf16)
    vt = [jnp.concatenate([v.astype(f32).T.astype(bf16), ones_v], axis=0) for v in vcat]
    mvt = jnp.concatenate([mvt_ref[0], jnp.ones((2 * SUBLANES, mk_ref.shape[1]), bf16)], axis=0)

    soft = [[_softmax_cols(s[:, h * BLOCK:(h + 1) * BLOCK], sink_ref[h]) for h in range(ATTN_HEADS)] for s in st]
    softm = [_softmax_cols(smt[:, h * nq:(h + 1) * nq]) for h in range(MEM_HEADS)]

    ot = [_dot(v, jnp.concatenate([p_ for p_, _ in sf], axis=1)) for v, sf in zip(vt, soft)]
    omt = _dot(mvt, jnp.concatenate([p_ for p_, _ in softm], axis=1))

    half = HEAD_DIM
    vrows = ATTN_KV_WIDTH
    for t in range(QB):
        az = az_ref[0, t * BLOCK:(t + 1) * BLOCK, :].astype(f32)
        rinv = [1.0 / (ot[t][vrows:vrows + 1, h * BLOCK:(h + 1) * BLOCK] + soft[t][h][1]) for h in range(ATTN_HEADS)]
        for c in range(ATTN_GROUP):
            o0 = ot[t][0:half, (2 * c) * BLOCK:(2 * c + 1) * BLOCK] * rinv[2 * c]
            o1 = ot[t][half:2 * half, (2 * c + 1) * BLOCK:(2 * c + 2) * BLOCK] * rinv[2 * c + 1]
            y = jnp.concatenate([o0, o1], axis=0).T * _silu(az[:, c * LANES:(c + 1) * LANES])
            ya_ref[0, t * BLOCK:(t + 1) * BLOCK, c * LANES:(c + 1) * LANES] = y.astype(bf16)
    ymt = jnp.concatenate([omt[h * HEAD_DIM:(h + 1) * HEAD_DIM, h * nq:(h + 1) * nq]
                           * (1.0 / omt[MEM_WIDTH:MEM_WIDTH + 1, h * nq:(h + 1) * nq])
                           for h in range(MEM_HEADS)], axis=0)
    ym_ref[0] = (ymt.T * _silu(mz_ref[0].astype(f32))).astype(bf16)


def _attn(aq, ak, av, az, mq, mz, mk, mvt, p):
    B, L, _ = aq.shape
    assert L % (QB * BLOCK) == 0 and QB == 2
    ns = L // (QB * BLOCK)
    M = mk.shape[1]
    tok = lambda w: pl.BlockSpec((1, QB * BLOCK, w), lambda b, j: (b, j, 0))
    seq = lambda w: pl.BlockSpec((1, L, w), lambda b, j: (b, 0, 0))
    bias_shape = (1, 3 * BLOCK, ATTN_HEADS * BLOCK)
    return pl.pallas_call(
        _attn_kernel,
        grid=(B, ns),
        in_specs=[pl.BlockSpec(memory_space=pltpu.SMEM),
                  tok(ATTN_WIDTH), seq(ATTN_KV_WIDTH), seq(ATTN_KV_WIDTH), tok(ATTN_WIDTH),
                  tok(MEM_WIDTH), tok(MEM_WIDTH),
                  pl.BlockSpec((1, M, MEM_WIDTH), lambda b, j: (b, 0, 0)),
                  pl.BlockSpec((1, MEM_WIDTH, M), lambda b, j: (b, 0, 0)),
                  pl.BlockSpec(bias_shape, lambda b, j: (jnp.where(j == 0, 2, 0), 0, 0)),
                  pl.BlockSpec(bias_shape, lambda b, j: (jnp.where(j == ns - 1, 1, 0), 0, 0))],
        out_specs=[tok(ATTN_WIDTH), tok(MEM_WIDTH)],
        out_shape=[jax.ShapeDtypeStruct((B, L, ATTN_WIDTH), bf16), jax.ShapeDtypeStruct((B, L, MEM_WIDTH), bf16)],
        compiler_params=pltpu.CompilerParams(dimension_semantics=("parallel", "arbitrary"),
                                             vmem_limit_bytes=VMEM_LIMIT),
        name="attn",
    )(p['sink'], aq, ak, av, az, mq, mz, mk, mvt, p['bias'], p['bias'])


def _bd2(y, lo):
    z = jnp.zeros_like(y)
    return jnp.concatenate([jnp.where(lo, y, z), jnp.where(lo, z, y)], axis=0)


def _halves(x):
    return x[:, 0:LANES], x[:, LANES:2 * LANES]


def _dn_intra_kernel(q_ref, k_ref, v_ref, gb_ref, tri_ref, esel_ref,
                     m_ref, n_ref, qp_ref, op_ref, gt_ref, bc_ref, *, nc, group):
    C = DN_CHUNK
    W = DN_WIDTH
    ri = lax.broadcasted_iota(jnp.int32, (C, W), 0)
    lj = lax.broadcasted_iota(jnp.int32, (C, W), 1) % C
    eye = ri == lj
    eye_f = jnp.where(eye, 1.0, 0.0)
    masks = ((ri >= lj, ri > lj), (ri <= lj, ri < lj))
    lo = lax.broadcasted_iota(jnp.int32, (C, LANES), 1) < HEAD_DIM
    col = lax.broadcasted_iota(jnp.int32, (C, LANES), 1)
    is_bwd = (col >= DN_HEADS) & (col < 2 * DN_HEADS)
    is_beta = col >= 2 * DN_HEADS
    tri = tri_ref[...]

    srcs = []
    for c in range(nc):
        gbv = gb_ref[0, c * C:(c + 1) * C, :]
        gc = _dot_exact_rhs(tri, gbv)
        srcs.append(jnp.where(is_beta, gbv, jnp.where(is_bwd, gc[C - 1:C, :] - gc + gbv, gc)))
    hi, mid, _ = _split3(jnp.concatenate(srcs, axis=0))
    bc_ref[:, 0:2 * W] = _dot(hi, esel_ref[:, 0:2 * W]) + _dot(mid, esel_ref[:, 0:2 * W])
    bc_ref[:, 2 * W:4 * W] = _dot(hi, esel_ref[:, 2 * W:4 * W])

    def pmm(x, y):
        return [_dot(xh, _bd2(yh, lo)) for xh, yh in zip(_halves(x), _halves(y))]

    def cat(parts):
        return jnp.concatenate(parts, axis=1)

    def body(gi, carry):
        cis = [gi * group + t for t in range(group)]
        sls = [pl.ds(pl.multiple_of(ci * C, C), C) for ci in cis]
        qb = [q_ref[0, sl, :] for sl in sls]
        kb = [k_ref[0, sl, :] for sl in sls]
        vb = [v_ref[0, sl, :] for sl in sls]
        gram = [cat([_dot_nt(xh, _bd2(kh, lo)) for xh, kh in zip(_halves(jnp.concatenate([k, q], axis=0)), _halves(k))])
                for k, q in zip(kb, qb)]
        inst = [(t, d) for t in range(group) for d in range(2)]
        gq = [bc_ref[sls[t], d * W:(d + 1) * W] for t, d in inst]
        bq = [bc_ref[sls[t], (2 + d) * W:(3 + d) * W] for t, d in inst]
        glast = [g[C - 1:C, :] if d == 0 else g[0:1, :] for g, (t, d) in zip(gq, inst)]
        r = [jnp.sum(jnp.where(eye, g, 0.0), axis=0, keepdims=True) for g in gq]
        dec = [jnp.where(masks[d][0], jnp.exp(jnp.minimum(g - rr, 0.0)), 0.0) for g, rr, (t, d) in zip(gq, r, inst)]
        nn = [jnp.where(masks[d][1], -(gram[t][0:C] * b * dc), 0.0) for b, dc, (t, d) in zip(bq, dec, inst)]
        tm = [eye_f + a for a in nn]
        nb_ = [a.astype(bf16) for a in nn]
        x = [cat(pmm(a, a)) for a in nb_]
        for s_ in range(5):
            xb = [a.astype(bf16) for a in x]
            if s_ < 4:
                rr = [cat(pmm(jnp.concatenate([a.astype(bf16), b_], axis=0), b_)) for a, b_ in zip(tm, xb)]
                tm = [a + b_[0:C] for a, b_ in zip(tm, rr)]
                x = [b_[C:2 * C] for b_ in rr]
            else:
                tm = [a + cat(pmm(a.astype(bf16), b_)) for a, b_ in zip(tm, xb)]
        tb = [a.astype(bf16) for a in tm]
        eg = [jnp.exp(g) for g in gq]
        vbeta = [(vb[t].astype(f32) * b).astype(bf16) for b, (t, d) in zip(bq, inst)]
        kbg = [(kb[t].astype(f32) * b * e).astype(bf16) for b, e, (t, d) in zip(bq, eg, inst)]
        uw = [[_dot(th, jnp.concatenate([_bd2(vh, lo), _bd2(kh, lo)], axis=1)).astype(bf16)
               for th, vh, kh in zip(_halves(a), _halves(v_), _halves(k_))]
              for a, v_, k_ in zip(tb, vbeta, kbg)]
        kd = [(kb[t].astype(f32) * jnp.exp(gl - g)).astype(bf16) for g, gl, (t, d) in zip(gq, glast, inst)]
        fm = [[lax.dot_general(kh, uwp, TN, preferred_element_type=f32) for kh, uwp in zip(_halves(k_), uw_)]
              for k_, uw_ in zip(kd, uw)]
        qkb = [(gram[t][C:2 * C] * dc).astype(bf16) for dc, (t, d) in zip(dec, inst)]
        qo = [[_dot(qh, jnp.concatenate([_bd2(uwp[:, 0:LANES], lo), _bd2(uwp[:, LANES:2 * LANES], lo)], axis=1))
               for qh, uwp in zip(_halves(a), uw_)] for a, uw_ in zip(qkb, uw)]
        for idx, (t, d) in enumerate(inst):
            ci, sl = cis[t], sls[t]
            f0, f1 = fm[idx]
            n_ref[0, d, ci] = cat([jnp.where(lo, f[0:C, 0:LANES], f[C:2 * C, 0:LANES])
                                   for f in (f0, f1)]).astype(bf16)
            m_ref[0, d, ci] = cat([-jnp.where(lo, f[0:C, LANES:2 * LANES], f[C:2 * C, LANES:2 * LANES])
                                   for f in (f0, f1)]).astype(bf16)
            qd = qb[t].astype(f32) * eg[idx]
            qp_ref[0, d, sl, :] = (qd - cat([p_[:, LANES:2 * LANES] for p_ in qo[idx]])).astype(bf16)
            op_ref[0, d, sl, :] = cat([p_[:, 0:LANES] for p_ in qo[idx]]).astype(bf16)
            gt_ref[0, d, ci] = jnp.exp(glast[idx])
        return carry

    lax.fori_loop(0, nc // group, body, 0)


def _dn_intra(dq, dk, dv, gb, p, tb, group=4):
    B, L, _ = dq.shape
    nc = tb // DN_CHUNK
    nchunk = L // DN_CHUNK
    tok = lambda w: pl.BlockSpec((1, tb, w), lambda b, i: (b, i, 0))
    full = lambda a: pl.BlockSpec(a.shape, lambda b, i: (0,) * a.ndim)
    consts = [p['tri'], p['esel']]
    return pl.pallas_call(
        functools.partial(_dn_intra_kernel, nc=nc, group=group),
        grid=(B, L // tb),
        in_specs=[tok(DN_WIDTH), tok(DN_WIDTH), tok(DN_WIDTH), tok(LANES)] + [full(a) for a in consts],
        out_specs=[pl.BlockSpec((1, 2, nc, DN_CHUNK, DN_WIDTH), lambda b, i: (b, 0, i, 0, 0)),
                   pl.BlockSpec((1, 2, nc, DN_CHUNK, DN_WIDTH), lambda b, i: (b, 0, i, 0, 0)),
                   pl.BlockSpec((1, 2, tb, DN_WIDTH), lambda b, i: (b, 0, i, 0)),
                   pl.BlockSpec((1, 2, tb, DN_WIDTH), lambda b, i: (b, 0, i, 0)),
                   pl.BlockSpec((1, 2, nc, 1, DN_WIDTH), lambda b, i: (b, 0, i, 0, 0))],
        out_shape=[jax.ShapeDtypeStruct((B, 2, nchunk, DN_CHUNK, DN_WIDTH), bf16),
                   jax.ShapeDtypeStruct((B, 2, nchunk, DN_CHUNK, DN_WIDTH), bf16),
                   jax.ShapeDtypeStruct((B, 2, L, DN_WIDTH), bf16),
                   jax.ShapeDtypeStruct((B, 2, L, DN_WIDTH), bf16),
                   jax.ShapeDtypeStruct((B, 2, nchunk, 1, DN_WIDTH), f32)],
        scratch_shapes=[pltpu.VMEM((tb, 4 * DN_WIDTH), f32)],
        compiler_params=pltpu.CompilerParams(dimension_semantics=("parallel", "parallel"),
                                             vmem_limit_bytes=VMEM_LIMIT),
        name="dn_intra",
    )(dq, dk, dv, gb, *consts)


def _dn_scan_kernel(mf_ref, nf_ref, qf_ref, of_ref, gf_ref, mb_ref, nb_ref, qb_ref, ob_ref, gbk_ref,
                    yf_ref, yb_ref, s_ref, *, nc):
    C = DN_CHUNK

    @pl.when(pl.program_id(1) == 0)
    def _():
        s_ref[...] = jnp.zeros_like(s_ref)

    lo = lax.broadcasted_iota(jnp.int32, (C, LANES), 1) < HEAD_DIM
    dirs = ((mf_ref, nf_ref, qf_ref, of_ref, gf_ref, yf_ref), (mb_ref, nb_ref, qb_ref, ob_ref, gbk_ref, yb_ref))

    def step(j, carry):
        cis = (j, nc - 1 - j)
        sls = [pl.ds(pl.multiple_of(ci * C, C), C) for ci in cis]
        inst = [(d, h) for d in range(2) for h in range(2)]
        hs = lambda a, h: a[:, h * LANES:(h + 1) * LANES]
        m = [dirs[d][0][0, 0, cis[d]] for d in range(2)]
        q = [dirs[d][2][0, 0, sls[d], :] for d in range(2)]
        s = [s_ref[d, h] for d, h in inst]
        lhs = [jnp.concatenate([_bd2(hs(m[d], h), lo), hs(q[d], h)], axis=0) for d, h in inst]
        rr = [_dot(a, b_.astype(bf16)) for a, b_ in zip(lhs, s)]
        for idx, (d, h) in enumerate(inst):
            gt = dirs[d][4][0, 0, cis[d]]
            nn = dirs[d][1][0, 0, cis[d]]
            s_ref[d, h] = s[idx] * hs(gt, h) + rr[idx][0:2 * C] + _bd2(hs(nn, h), lo).astype(f32)
        for d in range(2):
            y = jnp.concatenate([rr[2 * d][2 * C:3 * C], rr[2 * d + 1][2 * C:3 * C]], axis=1)
            dirs[d][5][0, sls[d], :] = (y + dirs[d][3][0, 0, sls[d], :].astype(f32)).astype(bf16)
        return carry

    lax.fori_loop(0, nc, step, 0)


def _dn_scan(m, n, qp, op, gt, tb):
    B, _, nchunk, C, W = m.shape
    L = nchunk * C
    nc = tb // C
    ns = L // tb
    fwd5 = lambda last: pl.BlockSpec((1, 1, nc, last, W), lambda b, i: (b, 0, i, 0, 0))
    bwd5 = lambda last: pl.BlockSpec((1, 1, nc, last, W), lambda b, i: (b, 1, ns - 1 - i, 0, 0))
    fwd4 = pl.BlockSpec((1, 1, tb, W), lambda b, i: (b, 0, i, 0))
    bwd4 = pl.BlockSpec((1, 1, tb, W), lambda b, i: (b, 1, ns - 1 - i, 0))
    return pl.pallas_call(
        functools.partial(_dn_scan_kernel, nc=nc),
        grid=(B, ns),
        in_specs=[fwd5(C), fwd5(C), fwd4, fwd4, fwd5(1), bwd5(C), bwd5(C), bwd4, bwd4, bwd5(1)],
        out_specs=[pl.BlockSpec((1, tb, W), lambda b, i: (b, i, 0)),
                   pl.BlockSpec((1, tb, W), lambda b, i: (b, ns - 1 - i, 0))],
        out_shape=[jax.ShapeDtypeStruct((B, L, W), bf16), jax.ShapeDtypeStruct((B, L, W), bf16)],
        scratch_shapes=[pltpu.VMEM((2, 2, 2 * C, LANES), f32)],
        compiler_params=pltpu.CompilerParams(dimension_semantics=("parallel", "arbitrary"),
                                             vmem_limit_bytes=VMEM_LIMIT),
        name="dn_scan",
    )(m, n, qp, op, gt, m, n, qp, op, gt)


def _outproj_kernel(x_ref, ya_ref, ym_ref, yf_ref, yb_ref, dz_ref, lig_ref, lib_ref, lg_ref, lb_ref,
                    ng_ref, wa_ref, wd_ref, wm_ref, ones_ref, o_ref):
    xa = _layer_norm(x_ref[0], lig_ref[...], lib_ref[...])
    o = yf_ref[0].astype(f32) + yb_ref[0].astype(f32)
    ms = _dot((o * o).astype(bf16), ones_ref[...]) * (1.0 / HEAD_DIM)
    yd = o * lax.rsqrt(ms + RMS_EPS) * ng_ref[...] * _silu(dz_ref[0].astype(f32))
    y = _dot(ya_ref[0], wa_ref[...]) + _dot(yd.astype(bf16), wd_ref[...]) + _dot(ym_ref[0], wm_ref[...])
    o_ref[0] = _layer_norm(xa + y, lg_ref[...], lb_ref[...])


def _outproj(x, ya, ym, yf, yb, dz, p, tm):
    B, L, _ = x.shape
    tok = lambda w: pl.BlockSpec((1, tm, w), lambda b, i: (b, i, 0))
    full = lambda a: pl.BlockSpec(a.shape, lambda b, i: (0,) * a.ndim)
    consts = [p['ln_in_ga'], p['ln_in_ba'], p['ln_g'], p['ln_b'], p['norm_g'], p['wo_attn'], p['wo_dn'], p['wo_mem'],
              p['ones_bd']]
    return pl.pallas_call(
        _outproj_kernel,
        grid=(B, L // tm),
        in_specs=[tok(D_MODEL), tok(ATTN_WIDTH), tok(MEM_WIDTH), tok(DN_WIDTH), tok(DN_WIDTH), tok(DN_WIDTH)]
                 + [full(a) for a in consts],
        out_specs=tok(D_MODEL),
        out_shape=jax.ShapeDtypeStruct((B, L, D_MODEL), f32),
        compiler_params=pltpu.CompilerParams(dimension_semantics=("parallel", "parallel"),
                                             vmem_limit_bytes=VMEM_LIMIT),
        name="outproj",
    )(x, ya, ym, yf, yb, dz, *consts)


def _t5_bucket(rel):
    nb = REL_BUCKETS // 2
    max_exact = nb // 2
    n = jnp.abs(rel)
    large = max_exact + (jnp.log(jnp.maximum(n, 1).astype(f32) / max_exact)
                         / math.log(REL_MAX_DIST / max_exact) * (nb - max_exact)).astype(jnp.int32)
    large = jnp.minimum(large, nb - 1)
    return jnp.where(rel > 0, nb, 0) + jnp.where(n < max_exact, n, large)


def _prepare(ln_in_g, ln_in_b, rel_bias, w_in, attn_sink, dn_conv, dn_A_log, dn_dt_bias, dn_norm_g,
             w_mem_kv, w_out, ln_g, ln_b):
    offs = np.cumsum([0, ATTN_WIDTH, ATTN_KV_WIDTH, ATTN_KV_WIDTH, ATTN_WIDTH, DN_WIDTH, DN_WIDTH, DN_WIDTH,
                      DN_WIDTH, 2 * DN_HEADS, 2 * DN_HEADS, MEM_WIDTH, MEM_WIDTH])
    (w_aq, w_ak, w_av, w_az, w_dq, w_dk, w_dv, w_dz, w_da, w_db, w_mq, w_mz) = [
        w_in[:, int(offs[j]):int(offs[j + 1])] for j in range(12)]
    order = [kv * ATTN_GROUP + c for c in range(ATTN_GROUP) for kv in range(ATTN_KV_HEADS)]
    head_cols = lambda w: jnp.concatenate([w[:, h * HEAD_DIM:(h + 1) * HEAD_DIM] for h in order], axis=1)
    scale = HEAD_DIM ** -0.5 * LOG2E
    p = {}
    p['w_attn'] = jnp.concatenate([head_cols(w_aq) * scale, w_ak, w_av, head_cols(w_az)], axis=1).astype(bf16)
    p['w_mem'] = jnp.concatenate([w_mq * scale, w_mz], axis=1).astype(bf16)
    p['w_dn'] = jnp.concatenate([w_dq, w_dk, w_dv, w_dz], axis=1).astype(bf16)
    p['w_ab'] = jnp.pad(jnp.concatenate([w_da, w_db], axis=1), ((0, 0), (0, LANES - 4 * DN_HEADS))).astype(bf16)
    p['conv'] = jnp.pad(dn_conv, ((0, SUBLANES - DN_CONV), (0, 0)))
    p['alog'] = jnp.pad(dn_A_log.reshape(1, 2 * DN_HEADS), ((0, 0), (0, LANES - 2 * DN_HEADS)))
    p['dtb'] = jnp.pad(dn_dt_bias.reshape(1, 2 * DN_HEADS), ((0, 0), (0, LANES - 2 * DN_HEADS)))
    p['ln_in_g'] = ln_in_g.reshape(1, D_MODEL)
    p['ln_in_b'] = ln_in_b.reshape(1, D_MODEL)
    p['ln_in_ga'] = DEEPNORM_ALPHA * p['ln_in_g']
    p['ln_in_ba'] = DEEPNORM_ALPHA * p['ln_in_b']
    p['ln_g'] = ln_g.reshape(1, D_MODEL)
    p['ln_b'] = ln_b.reshape(1, D_MODEL)
    p['norm_g'] = jnp.tile(dn_norm_g.reshape(1, HEAD_DIM), (1, DN_HEADS))
    p['w_memk'] = w_mem_kv[:, 0:MEM_WIDTH].astype(bf16)
    p['w_memvt'] = w_mem_kv[:, MEM_WIDTH:2 * MEM_WIDTH].T.astype(bf16)
    p['wo_attn'] = jnp.concatenate([w_out[h * HEAD_DIM:(h + 1) * HEAD_DIM] for h in order], axis=0).astype(bf16)
    p['wo_dn'] = w_out[ATTN_WIDTH:ATTN_WIDTH + DN_WIDTH].astype(bf16)
    p['wo_mem'] = w_out[ATTN_WIDTH + DN_WIDTH:].astype(bf16)
    p['sink'] = jnp.stack([attn_sink[h] for h in order]) * LOG2E

    blk = np.arange(DN_WIDTH) // HEAD_DIM
    bd = (blk[:, None] == blk[None, :])
    p['ones_bd'] = jnp.asarray(bd, bf16)
    p['tri'] = jnp.asarray(np.tril(np.ones((DN_CHUNK, DN_CHUNK))), bf16)
    esel = np.zeros((LANES, 4 * DN_WIDTH), np.float32)
    for c in range(4 * DN_HEADS):
        esel[c, c * HEAD_DIM:(c + 1) * HEAD_DIM] = 1.0
    p['esel'] = jnp.asarray(esel, bf16)

    t = jnp.arange(BLOCK)[None, :]
    s = jnp.arange(3 * BLOCK)[:, None]
    rel = s - BLOCK - t
    onehot = jax.nn.one_hot(_t5_bucket(rel), REL_BUCKETS, dtype=f32)
    rb = jnp.stack([rel_bias[:, h] for h in order], axis=0)
    bias = jnp.einsum('stk,hk->sht', onehot, rb, precision=lax.Precision.HIGHEST)
    bias = jnp.where((jnp.abs(rel) <= WINDOW)[:, None, :], bias * LOG2E, NEG)
    bias = bias.reshape(3 * BLOCK, ATTN_HEADS * BLOCK)
    no_prev = jnp.where(s < BLOCK, NEG, 0.0)
    no_next = jnp.where(s >= 2 * BLOCK, NEG, 0.0)
    p['bias'] = jnp.stack([bias, bias + no_next, bias + no_prev])
    return p


def _trunk(x, mem, p, tm=512, tb_intra=512, tb_scan=512):
    L = x.shape[1]
    tm = min(tm, L)
    tb_intra = min(tb_intra, L)
    tb_scan = min(tb_scan, L)
    aq, ak, av, az, mq, mz, dq, dk, dv, dz, gb = _inproj(x, p, tm)
    mk, mvt = _memkv(mem, p)
    ya, ym = _attn(aq, ak, av, az, mq, mz, mk, mvt, p)
    m, n, qp, op, gt = _dn_intra(dq, dk, dv, gb, p, tb_intra)
    yf, yb = _dn_scan(m, n, qp, op, gt, tb_scan)
    return _outproj(x, ya, ym, yf, yb, dz, p, tm)


def kernel(x_prompt, x_sample, mem_prompt, mem_sample, ln_in_g, ln_in_b, rel_bias, w_in, attn_sink, dn_conv,
           dn_A_log, dn_dt_bias, dn_norm_g, w_mem_kv, w_out, ln_g, ln_b):
    p = _prepare(ln_in_g, ln_in_b, rel_bias, w_in[0], attn_sink[0], dn_conv[0], dn_A_log[0], dn_dt_bias[0],
                 dn_norm_g[0], w_mem_kv[0], w_out[0], ln_g[0], ln_b[0])
    return (_trunk(x_prompt, mem_prompt, p), _trunk(x_sample, mem_sample, p))
```

```python
import functools
import math

import jax
import jax.numpy as jnp
import numpy as np
from jax import lax
from jax.experimental import pallas as pl
from jax.experimental.pallas import tpu as pltpu

f32 = jnp.float32
bf16 = jnp.bfloat16

D_MODEL = 1024
HEAD_DIM = 64
ATTN_HEADS = 8
ATTN_KV_HEADS = 2
ATTN_GROUP = ATTN_HEADS // ATTN_KV_HEADS
ATTN_WIDTH = ATTN_HEADS * HEAD_DIM
ATTN_KV_WIDTH = ATTN_KV_HEADS * HEAD_DIM
WINDOW = 128
BLOCK = 128
REL_BUCKETS = 32
REL_MAX_DIST = 128
DN_HEADS = 4
DN_WIDTH = DN_HEADS * HEAD_DIM
DN_CONV = 5
DN_CONV_DIM = 3 * DN_WIDTH
DN_CHUNK = 64
MEM_HEADS = 4
MEM_WIDTH = MEM_HEADS * HEAD_DIM
DEPTH = 1
DEEPNORM_ALPHA = (2 * DEPTH) ** 0.25
LN_EPS = 1e-5
RMS_EPS = 1e-6
NEG = -1e30
LOG2E = 1.4426950408889634

LANES = 128
SUBLANES = 8
HALO = SUBLANES
VMEM_LIMIT = 56 * 1024 * 1024

NT = (((1,), (1,)), ((), ()))
TN = (((0,), (0,)), ((), ()))


def _dot(a, b):
    return jnp.dot(a, b, preferred_element_type=f32)


def _dot_nt(a, b):
    return lax.dot_general(a, b, NT, preferred_element_type=f32)


def _split3(x):
    hi = x.astype(bf16)
    r1 = x - hi.astype(f32)
    mid = r1.astype(bf16)
    lo = (r1 - mid.astype(f32)).astype(bf16)
    return hi, mid, lo


def _dot_exact_lhs(x, sel):
    hi, mid, lo = _split3(x)
    return _dot(hi, sel) + _dot(mid, sel) + _dot(lo, sel)


def _dot_exact_rhs(sel, x):
    hi, mid, lo = _split3(x)
    return _dot(sel, hi) + _dot(sel, mid) + _dot(sel, lo)


def _layer_norm(x, g, b):
    mu = jnp.mean(x, axis=-1, keepdims=True)
    xc = x - mu
    var = jnp.mean(xc * xc, axis=-1, keepdims=True)
    return xc * lax.rsqrt(var + LN_EPS) * g + b


def _silu(x):
    return x / (1.0 + jnp.exp(-x))


def _inproj_kernel(x_ref, xp_ref, xn_ref, lng_ref, lnb_ref, wattn_ref, wmem_ref, wdn_ref, wab_ref,
                   conv_ref, alog_ref, dtb_ref, ones_ref,
                   aq_ref, ak_ref, av_ref, az_ref, mq_ref, mz_ref, dq_ref, dk_ref, dv_ref, dz_ref, gb_ref,
                   hbuf, *, tm):
    i = pl.program_id(1)
    n = pl.num_programs(1)
    g = lng_ref[...]
    b = lnb_ref[...]
    xb = _layer_norm(x_ref[0], g, b).astype(bf16)

    wqkv = wdn_ref[:, 0:DN_CONV_DIM]
    hp = _dot(_layer_norm(xp_ref[0], g, b).astype(bf16), wqkv)
    hn = _dot(_layer_norm(xn_ref[0], g, b).astype(bf16), wqkv)
    hbuf[0:HALO, :] = jnp.where(i > 0, hp, 0.0)
    hbuf[HALO:HALO + tm, :] = _dot(xb, wqkv)
    hbuf[HALO + tm:HALO + tm + HALO, :] = jnp.where(i < n - 1, hn, 0.0)
    ab = _dot(xb, wab_ref[...])

    aq_ref[0] = _dot(xb, wattn_ref[:, 0:512]).astype(bf16)

    rows = tm + 2 * HALO
    half = DN_CONV // 2
    hfull = hbuf[...]
    c = conv_ref[half:half + 1, :] * hfull[HALO:HALO + tm, :]
    for j in range(DN_CONV):
        if j != half:
            c = c + conv_ref[j:j + 1, :] * pltpu.roll(hfull, (half - j) % rows, axis=0)[HALO:HALO + tm, :]

    kv = _dot(xb, wattn_ref[:, 512:768]).astype(bf16)
    ak_ref[0] = kv[:, 0:ATTN_KV_WIDTH]
    av_ref[0] = kv[:, ATTN_KV_WIDTH:2 * ATTN_KV_WIDTH]
    az_ref[0] = _dot(xb, wattn_ref[:, 768:1280]).astype(bf16)

    c = _silu(c)
    ones = ones_ref[...]

    def l2n(t, scale):
        ss = _dot((t * t).astype(bf16), ones)
        return t * (lax.rsqrt(ss + 1e-6) * scale)

    dq_ref[0] = l2n(c[:, 0:DN_WIDTH], HEAD_DIM ** -0.5).astype(bf16)
    mq_ref[0] = _dot(xb, wmem_ref[:, 0:256]).astype(bf16)
    dk_ref[0] = l2n(c[:, DN_WIDTH:2 * DN_WIDTH], 1.0).astype(bf16)
    mz_ref[0] = _dot(xb, wmem_ref[:, 256:512]).astype(bf16)
    dv_ref[0] = c[:, 2 * DN_WIDTH:3 * DN_WIDTH].astype(bf16)
    dz_ref[0] = _dot(xb, wdn_ref[:, DN_CONV_DIM:DN_CONV_DIM + DN_WIDTH]).astype(bf16)

    z = ab + dtb_ref[...]
    sp = jnp.maximum(z, 0.0) + jnp.log1p(jnp.exp(-jnp.abs(z)))
    gdec = -jnp.exp(alog_ref[...]) * sp
    beta = 1.0 / (1.0 + jnp.exp(-ab))
    lane = lax.broadcasted_iota(jnp.int32, ab.shape, 1)
    gb_ref[0] = jnp.where(lane < 2 * DN_HEADS, gdec, beta)


def _inproj(x, p, tm):
    B, L, _ = x.shape
    nt = L // tm
    r8 = tm // HALO
    full = lambda a: pl.BlockSpec(a.shape, lambda b, i: (0,) * a.ndim)
    consts = [p['ln_in_g'], p['ln_in_b'], p['w_attn'], p['w_mem'], p['w_dn'], p['w_ab'],
              p['conv'], p['alog'], p['dtb'], p['ones_bd']]
    tok = lambda w: pl.BlockSpec((1, tm, w), lambda b, i: (b, i, 0))
    widths = [512, 128, 128, 512, 256, 256, 256, 256, 256, 256]
    out_shape = [jax.ShapeDtypeStruct((B, L, w), bf16) for w in widths] + [jax.ShapeDtypeStruct((B, L, LANES), f32)]
    return pl.pallas_call(
        functools.partial(_inproj_kernel, tm=tm),
        grid=(B, nt),
        in_specs=[pl.BlockSpec((1, tm, D_MODEL), lambda b, i: (b, i, 0)),
                  pl.BlockSpec((1, HALO, D_MODEL), lambda b, i: (b, jnp.maximum(i * r8 - 1, 0), 0)),
                  pl.BlockSpec((1, HALO, D_MODEL), lambda b, i: (b, jnp.minimum((i + 1) * r8, L // HALO - 1), 0)),
                  ] + [full(a) for a in consts],
        out_specs=[tok(w) for w in widths] + [tok(LANES)],
        out_shape=out_shape,
        scratch_shapes=[pltpu.VMEM((tm + 2 * HALO, DN_CONV_DIM), f32)],
        compiler_params=pltpu.CompilerParams(dimension_semantics=("parallel", "parallel"),
                                             vmem_limit_bytes=VMEM_LIMIT),
        name="inproj",
    )(x, x, x, *consts)


def _memkv_kernel(m_ref, lng_ref, lnb_ref, wk_ref, wvt_ref, k_ref, vt_ref):
    mb = _layer_norm(m_ref[0], lng_ref[...], lnb_ref[...]).astype(bf16)
    k_ref[0] = _dot(mb, wk_ref[...]).astype(bf16)
    vt_ref[0] = _dot_nt(wvt_ref[...], mb).astype(bf16)


def _memkv(mem, p):
    B, M, _ = mem.shape
    return pl.pallas_call(
        _memkv_kernel,
        grid=(B,),
        in_specs=[pl.BlockSpec((1, M, D_MODEL), lambda b: (b, 0, 0)),
                  pl.BlockSpec((1, D_MODEL), lambda b: (0, 0)),
                  pl.BlockSpec((1, D_MODEL), lambda b: (0, 0)),
                  pl.BlockSpec((D_MODEL, MEM_WIDTH), lambda b: (0, 0)),
                  pl.BlockSpec((MEM_WIDTH, D_MODEL), lambda b: (0, 0))],
        out_specs=[pl.BlockSpec((1, M, MEM_WIDTH), lambda b: (b, 0, 0)),
                   pl.BlockSpec((1, MEM_WIDTH, M), lambda b: (b, 0, 0))],
        out_shape=[jax.ShapeDtypeStruct((B, M, MEM_WIDTH), bf16), jax.ShapeDtypeStruct((B, MEM_WIDTH, M), bf16)],
        compiler_params=pltpu.CompilerParams(dimension_semantics=("parallel",), vmem_limit_bytes=VMEM_LIMIT),
        name="memkv",
    )(mem, p['ln_in_g'], p['ln_in_b'], p['w_memk'], p['w_memvt'])


def _softmax_cols(s, extra=None):
    m = jnp.max(s, axis=0, keepdims=True)
    if extra is not None:
        m = jnp.maximum(m, extra)
    e = jnp.exp2(s - m)
    return e.astype(bf16), (None if extra is None else jnp.exp2(extra - m))


QB = 4


def _attn_kernel(sink_ref, aq_ref, ak_ref, av_ref, az_ref, mq_ref, mz_ref, mk_ref, mvt_ref,
                 bias_first_ref, bias_mid_ref, bias_last_ref, ya_ref, ym_ref):
    j = pl.program_id(1)
    nb = QB * pl.num_programs(1)
    bias_refs = [bias_first_ref] + [bias_mid_ref] * (QB - 2) + [bias_last_ref]
    lane = lax.broadcasted_iota(jnp.int32, (BLOCK, LANES), 1)
    low = lane < HEAD_DIM
    zero = jnp.zeros((BLOCK, LANES), bf16)

    def rows(ref, blk):
        return ref[0, pl.ds(pl.multiple_of(blk * BLOCK, BLOCK), BLOCK), :]

    def band(ref, i):
        return jnp.concatenate([rows(ref, jnp.maximum(i - 1, 0)), rows(ref, i), rows(ref, jnp.minimum(i + 1, nb - 1))],
                               axis=0)

    def stack_q(q):
        parts = []
        for c in range(ATTN_GROUP):
            t = q[:, c * LANES:(c + 1) * LANES]
            parts.append(jnp.where(low, t, zero))
            parts.append(jnp.where(low, zero, t))
        return jnp.concatenate(parts, axis=0)

    blks = [j * QB + t for t in range(QB)]
    kcat = [band(ak_ref, i) for i in blks]
    vcat = [band(av_ref, i) for i in blks]
    qs = [stack_q(aq_ref[0, t * BLOCK:(t + 1) * BLOCK, :]) for t in range(QB)]
    mq = mq_ref[0]
    nq = QB * BLOCK
    lane_m = lax.broadcasted_iota(jnp.int32, (nq, MEM_WIDTH), 1) // HEAD_DIM
    zm = jnp.zeros((nq, MEM_WIDTH), bf16)
    mqs = jnp.concatenate([jnp.where(lane_m == h, mq, zm) for h in range(MEM_HEADS)], axis=0)

    st = [_dot_nt(k, q) + b[0] for k, q, b in zip(kcat, qs, bias_refs)]
    smt = _dot_nt(mk_ref[0], mqs)
    ones_v = jnp.ones((2 * SUBLANES, 3 * BLOCK), bf16)
    vt = [jnp.concatenate([v.astype(f32).T.astype(bf16), ones_v], axis=0) for v in vcat]
    mvt = jnp.concatenate([mvt_ref[0], jnp.ones((2 * SUBLANES, mk_ref.shape[1]), bf16)], axis=0)

    soft = [[_softmax_cols(s[:, h * BLOCK:(h + 1) * BLOCK], sink_ref[h]) for h in range(ATTN_HEADS)] for s in st]
    softm = [_softmax_cols(smt[:, h * nq:(h + 1) * nq]) for h in range(MEM_HEADS)]

    ot = [_dot(v, jnp.concatenate([p_ for p_, _ in sf], axis=1)) for v, sf in zip(vt, soft)]
    omt = _dot(mvt, jnp.concatenate([p_ for p_, _ in softm], axis=1))

    half = HEAD_DIM
    vrows = ATTN_KV_WIDTH
    for t in range(QB):
        az = az_ref[0, t * BLOCK:(t + 1) * BLOCK, :].astype(f32)
        rinv = [1.0 / (ot[t][vrows:vrows + 1, h * BLOCK:(h + 1) * BLOCK] + soft[t][h][1]) for h in range(ATTN_HEADS)]
        for c in range(ATTN_GROUP):
            o0 = ot[t][0:half, (2 * c) * BLOCK:(2 * c + 1) * BLOCK] * rinv[2 * c]
            o1 = ot[t][half:2 * half, (2 * c + 1) * BLOCK:(2 * c + 2) * BLOCK] * rinv[2 * c + 1]
            y = jnp.concatenate([o0, o1], axis=0).T * _silu(az[:, c * LANES:(c + 1) * LANES])
            ya_ref[0, t * BLOCK:(t + 1) * BLOCK, c * LANES:(c + 1) * LANES] = y.astype(bf16)
    ymt = jnp.concatenate([omt[h * HEAD_DIM:(h + 1) * HEAD_DIM, h * nq:(h + 1) * nq]
                           * (1.0 / omt[MEM_WIDTH:MEM_WIDTH + 1, h * nq:(h + 1) * nq])
                           for h in range(MEM_HEADS)], axis=0)
    ym_ref[0] = (ymt.T * _silu(mz_ref[0].astype(f32))).astype(bf16)


def _attn(aq, ak, av, az, mq, mz, mk, mvt, p):
    B, L, _ = aq.shape
    assert L % (QB * BLOCK) == 0 and QB >= 2
    ns = L // (QB * BLOCK)
    M = mk.shape[1]
    tok = lambda w: pl.BlockSpec((1, QB * BLOCK, w), lambda b, j: (b, j, 0))
    seq = lambda w: pl.BlockSpec((1, L, w), lambda b, j: (b, 0, 0))
    bias_shape = (1, 3 * BLOCK, ATTN_HEADS * BLOCK)
    return pl.pallas_call(
        _attn_kernel,
        grid=(B, ns),
        in_specs=[pl.BlockSpec(memory_space=pltpu.SMEM),
                  tok(ATTN_WIDTH), seq(ATTN_KV_WIDTH), seq(ATTN_KV_WIDTH), tok(ATTN_WIDTH),
                  tok(MEM_WIDTH), tok(MEM_WIDTH),
                  pl.BlockSpec((1, M, MEM_WIDTH), lambda b, j: (b, 0, 0)),
                  pl.BlockSpec((1, MEM_WIDTH, M), lambda b, j: (b, 0, 0)),
                  pl.BlockSpec(bias_shape, lambda b, j: (jnp.where(j == 0, 2, 0), 0, 0)),
                  pl.BlockSpec(bias_shape, lambda b, j: (0, 0, 0)),
                  pl.BlockSpec(bias_shape, lambda b, j: (jnp.where(j == ns - 1, 1, 0), 0, 0))],
        out_specs=[tok(ATTN_WIDTH), tok(MEM_WIDTH)],
        out_shape=[jax.ShapeDtypeStruct((B, L, ATTN_WIDTH), bf16), jax.ShapeDtypeStruct((B, L, MEM_WIDTH), bf16)],
        compiler_params=pltpu.CompilerParams(dimension_semantics=("parallel", "arbitrary"),
                                             vmem_limit_bytes=VMEM_LIMIT),
        name="attn",
    )(p['sink'], aq, ak, av, az, mq, mz, mk, mvt, p['bias'], p['bias'], p['bias'])


def _bd2(y, lo):
    z = jnp.zeros_like(y)
    return jnp.concatenate([jnp.where(lo, y, z), jnp.where(lo, z, y)], axis=0)


def _halves(x):
    return x[:, 0:LANES], x[:, LANES:2 * LANES]


def _dn_intra_kernel(q_ref, k_ref, v_ref, gb_ref, tri_ref, esel_ref,
                     m_ref, n_ref, qp_ref, op_ref, gt_ref, bc_ref, *, nc, group):
    C = DN_CHUNK
    W = DN_WIDTH
    ri = lax.broadcasted_iota(jnp.int32, (C, W), 0)
    lj = lax.broadcasted_iota(jnp.int32, (C, W), 1) % C
    eye = ri == lj
    eye_f = jnp.where(eye, 1.0, 0.0)
    masks = ((ri >= lj, ri > lj), (ri <= lj, ri < lj))
    lo = lax.broadcasted_iota(jnp.int32, (C, LANES), 1) < HEAD_DIM
    col = lax.broadcasted_iota(jnp.int32, (C, LANES), 1)
    is_bwd = (col >= DN_HEADS) & (col < 2 * DN_HEADS)
    is_beta = col >= 2 * DN_HEADS
    tri = tri_ref[...]

    srcs = []
    for c in range(nc):
        gbv = gb_ref[0, c * C:(c + 1) * C, :]
        gc = _dot_exact_rhs(tri, gbv)
        srcs.append(jnp.where(is_beta, gbv, jnp.where(is_bwd, gc[C - 1:C, :] - gc + gbv, gc)))
    hi, mid, _ = _split3(jnp.concatenate(srcs, axis=0))
    bc_ref[:, 0:2 * W] = _dot(hi, esel_ref[:, 0:2 * W]) + _dot(mid, esel_ref[:, 0:2 * W])
    bc_ref[:, 2 * W:4 * W] = _dot(hi, esel_ref[:, 2 * W:4 * W])

    def pmm(x, y):
        return [_dot(xh, _bd2(yh, lo)) for xh, yh in zip(_halves(x), _halves(y))]

    def cat(parts):
        return jnp.concatenate(parts, axis=1)

    def body(gi, carry):
        cis = [gi * group + t for t in range(group)]
        sls = [pl.ds(pl.multiple_of(ci * C, C), C) for ci in cis]
        qb = [q_ref[0, sl, :] for sl in sls]
        kb = [k_ref[0, sl, :] for sl in sls]
        vb = [v_ref[0, sl, :] for sl in sls]
        gram = [cat([_dot_nt(xh, _bd2(kh, lo)) for xh, kh in zip(_halves(jnp.concatenate([k, q], axis=0)), _halves(k))])
                for k, q in zip(kb, qb)]
        inst = [(t, d) for t in range(group) for d in range(2)]
        gq = [bc_ref[sls[t], d * W:(d + 1) * W] for t, d in inst]
        bq = [bc_ref[sls[t], (2 + d) * W:(3 + d) * W] for t, d in inst]
        glast = [g[C - 1:C, :] if d == 0 else g[0:1, :] for g, (t, d) in zip(gq, inst)]
        r = [jnp.sum(jnp.where(eye, g, 0.0), axis=0, keepdims=True) for g in gq]
        dec = [jnp.where(masks[d][0], jnp.exp(jnp.minimum(g - rr, 0.0)), 0.0) for g, rr, (t, d) in zip(gq, r, inst)]
        nn = [jnp.where(masks[d][1], -(gram[t][0:C] * b * dc), 0.0) for b, dc, (t, d) in zip(bq, dec, inst)]
        tm = [eye_f + a for a in nn]
        nb_ = [a.astype(bf16) for a in nn]
        x = [cat(pmm(a, a)) for a in nb_]
        for s_ in range(5):
            xb = [a.astype(bf16) for a in x]
            if s_ < 4:
                rr = [cat(pmm(jnp.concatenate([a.astype(bf16), b_], axis=0), b_)) for a, b_ in zip(tm, xb)]
                tm = [a + b_[0:C] for a, b_ in zip(tm, rr)]
                x = [b_[C:2 * C] for b_ in rr]
            else:
                tm = [a + cat(pmm(a.astype(bf16), b_)) for a, b_ in zip(tm, xb)]
        tb = [a.astype(bf16) for a in tm]
        eg = [jnp.exp(g) for g in gq]
        vbeta = [(vb[t].astype(f32) * b).astype(bf16) for b, (t, d) in zip(bq, inst)]
        kbg = [(kb[t].astype(f32) * b * e).astype(bf16) for b, e, (t, d) in zip(bq, eg, inst)]
        uw = [[_dot(th, jnp.concatenate([_bd2(vh, lo), _bd2(kh, lo)], axis=1)).astype(bf16)
               for th, vh, kh in zip(_halves(a), _halves(v_), _halves(k_))]
              for a, v_, k_ in zip(tb, vbeta, kbg)]
        kd = [(kb[t].astype(f32) * jnp.exp(gl - g)).astype(bf16) for g, gl, (t, d) in zip(gq, glast, inst)]
        fm = [[lax.dot_general(kh, uwp, TN, preferred_element_type=f32) for kh, uwp in zip(_halves(k_), uw_)]
              for k_, uw_ in zip(kd, uw)]
        qkb = [(gram[t][C:2 * C] * dc).astype(bf16) for dc, (t, d) in zip(dec, inst)]
        qo = [[_dot(qh, jnp.concatenate([_bd2(uwp[:, 0:LANES], lo), _bd2(uwp[:, LANES:2 * LANES], lo)], axis=1))
               for qh, uwp in zip(_halves(a), uw_)] for a, uw_ in zip(qkb, uw)]
        for idx, (t, d) in enumerate(inst):
            ci, sl = cis[t], sls[t]
            f0, f1 = fm[idx]
            n_ref[0, d, ci] = cat([jnp.where(lo, f[0:C, 0:LANES], f[C:2 * C, 0:LANES])
                                   for f in (f0, f1)]).astype(bf16)
            m_ref[0, d, ci] = cat([-jnp.where(lo, f[0:C, LANES:2 * LANES], f[C:2 * C, LANES:2 * LANES])
                                   for f in (f0, f1)]).astype(bf16)
            qd = qb[t].astype(f32) * eg[idx]
            qp_ref[0, d, sl, :] = (qd - cat([p_[:, LANES:2 * LANES] for p_ in qo[idx]])).astype(bf16)
            op_ref[0, d, sl, :] = cat([p_[:, 0:LANES] for p_ in qo[idx]]).astype(bf16)
            gt_ref[0, d, ci] = jnp.exp(glast[idx])
        return carry

    lax.fori_loop(0, nc // group, body, 0)


def _dn_intra(dq, dk, dv, gb, p, tb, group=4):
    B, L, _ = dq.shape
    nc = tb // DN_CHUNK
    nchunk = L // DN_CHUNK
    tok = lambda w: pl.BlockSpec((1, tb, w), lambda b, i: (b, i, 0))
    full = lambda a: pl.BlockSpec(a.shape, lambda b, i: (0,) * a.ndim)
    consts = [p['tri'], p['esel']]
    return pl.pallas_call(
        functools.partial(_dn_intra_kernel, nc=nc, group=group),
        grid=(B, L // tb),
        in_specs=[tok(DN_WIDTH), tok(DN_WIDTH), tok(DN_WIDTH), tok(LANES)] + [full(a) for a in consts],
        out_specs=[pl.BlockSpec((1, 2, nc, DN_CHUNK, DN_WIDTH), lambda b, i: (b, 0, i, 0, 0)),
                   pl.BlockSpec((1, 2, nc, DN_CHUNK, DN_WIDTH), lambda b, i: (b, 0, i, 0, 0)),
                   pl.BlockSpec((1, 2, tb, DN_WIDTH), lambda b, i: (b, 0, i, 0)),
                   pl.BlockSpec((1, 2, tb, DN_WIDTH), lambda b, i: (b, 0, i, 0)),
                   pl.BlockSpec((1, 2, nc, 1, DN_WIDTH), lambda b, i: (b, 0, i, 0, 0))],
        out_shape=[jax.ShapeDtypeStruct((B, 2, nchunk, DN_CHUNK, DN_WIDTH), bf16),
                   jax.ShapeDtypeStruct((B, 2, nchunk, DN_CHUNK, DN_WIDTH), bf16),
                   jax.ShapeDtypeStruct((B, 2, L, DN_WIDTH), bf16),
                   jax.ShapeDtypeStruct((B, 2, L, DN_WIDTH), bf16),
                   jax.ShapeDtypeStruct((B, 2, nchunk, 1, DN_WIDTH), f32)],
        scratch_shapes=[pltpu.VMEM((tb, 4 * DN_WIDTH), f32)],
        compiler_params=pltpu.CompilerParams(dimension_semantics=("parallel", "parallel"),
                                             vmem_limit_bytes=VMEM_LIMIT),
        name="dn_intra",
    )(dq, dk, dv, gb, *consts)


def _dn_scan_kernel(mf_ref, nf_ref, qf_ref, of_ref, gf_ref, mb_ref, nb_ref, qb_ref, ob_ref, gbk_ref,
                    yf_ref, yb_ref, s_ref, *, nc, nbat):
    C = DN_CHUNK

    @pl.when(pl.program_id(1) == 0)
    def _():
        s_ref[...] = jnp.zeros_like(s_ref)

    lo = lax.broadcasted_iota(jnp.int32, (C, LANES), 1) < HEAD_DIM
    dirs = ((mf_ref, nf_ref, qf_ref, of_ref, gf_ref, yf_ref), (mb_ref, nb_ref, qb_ref, ob_ref, gbk_ref, yb_ref))

    def step(j, carry):
        cis = (j, nc - 1 - j)
        sls = [pl.ds(pl.multiple_of(ci * C, C), C) for ci in cis]
        seqs = [(b, d) for b in range(nbat) for d in range(2)]
        inst = [(b, d, h) for b, d in seqs for h in range(2)]
        hs = lambda a, h: a[:, h * LANES:(h + 1) * LANES]
        m = {(b, d): dirs[d][0][b, 0, cis[d]] for b, d in seqs}
        q = {(b, d): dirs[d][2][b, 0, sls[d], :] for b, d in seqs}
        s = [s_ref[b, d, h] for b, d, h in inst]
        lhs = [jnp.concatenate([_bd2(hs(m[b, d], h), lo), hs(q[b, d], h)], axis=0) for b, d, h in inst]
        rr = [_dot(a, b_.astype(bf16)) for a, b_ in zip(lhs, s)]
        for idx, (b, d, h) in enumerate(inst):
            gt = dirs[d][4][b, 0, cis[d]]
            nn = dirs[d][1][b, 0, cis[d]]
            s_ref[b, d, h] = s[idx] * hs(gt, h) + rr[idx][0:2 * C] + _bd2(hs(nn, h), lo).astype(f32)
        for k, (b, d) in enumerate(seqs):
            y = jnp.concatenate([rr[2 * k][2 * C:3 * C], rr[2 * k + 1][2 * C:3 * C]], axis=1)
            dirs[d][5][b, sls[d], :] = (y + dirs[d][3][b, 0, sls[d], :].astype(f32)).astype(bf16)
        return carry

    lax.fori_loop(0, nc, step, 0)


def _dn_scan(m, n, qp, op, gt, tb, nbat):
    B, _, nchunk, C, W = m.shape
    L = nchunk * C
    nc = tb // C
    ns = L // tb
    assert B % nbat == 0
    fwd5 = lambda last: pl.BlockSpec((nbat, 1, nc, last, W), lambda b, i: (b, 0, i, 0, 0))
    bwd5 = lambda last: pl.BlockSpec((nbat, 1, nc, last, W), lambda b, i: (b, 1, ns - 1 - i, 0, 0))
    fwd4 = pl.BlockSpec((nbat, 1, tb, W), lambda b, i: (b, 0, i, 0))
    bwd4 = pl.BlockSpec((nbat, 1, tb, W), lambda b, i: (b, 1, ns - 1 - i, 0))
    return pl.pallas_call(
        functools.partial(_dn_scan_kernel, nc=nc, nbat=nbat),
        grid=(B // nbat, ns),
        in_specs=[fwd5(C), fwd5(C), fwd4, fwd4, fwd5(1), bwd5(C), bwd5(C), bwd4, bwd4, bwd5(1)],
        out_specs=[pl.BlockSpec((nbat, tb, W), lambda b, i: (b, i, 0)),
                   pl.BlockSpec((nbat, tb, W), lambda b, i: (b, ns - 1 - i, 0))],
        out_shape=[jax.ShapeDtypeStruct((B, L, W), bf16), jax.ShapeDtypeStruct((B, L, W), bf16)],
        scratch_shapes=[pltpu.VMEM((nbat, 2, 2, 2 * C, LANES), f32)],
        compiler_params=pltpu.CompilerParams(dimension_semantics=("parallel", "arbitrary"),
                                             vmem_limit_bytes=VMEM_LIMIT),
        name="dn_scan",
    )(m, n, qp, op, gt, m, n, qp, op, gt)


def _outproj_kernel(x_ref, ya_ref, ym_ref, yf_ref, yb_ref, dz_ref, lig_ref, lib_ref, lg_ref, lb_ref,
                    ng_ref, wa_ref, wd_ref, wm_ref, ones_ref, o_ref):
    xa = _layer_norm(x_ref[0], lig_ref[...], lib_ref[...])
    o = yf_ref[0].astype(f32) + yb_ref[0].astype(f32)
    ms = _dot((o * o).astype(bf16), ones_ref[...]) * (1.0 / HEAD_DIM)
    yd = o * lax.rsqrt(ms + RMS_EPS) * ng_ref[...] * _silu(dz_ref[0].astype(f32))
    y = _dot(ya_ref[0], wa_ref[...]) + _dot(yd.astype(bf16), wd_ref[...]) + _dot(ym_ref[0], wm_ref[...])
    o_ref[0] = _layer_norm(xa + y, lg_ref[...], lb_ref[...])


def _outproj(x, ya, ym, yf, yb, dz, p, tm):
    B, L, _ = x.shape
    tok = lambda w: pl.BlockSpec((1, tm, w), lambda b, i: (b, i, 0))
    full = lambda a: pl.BlockSpec(a.shape, lambda b, i: (0,) * a.ndim)
    consts = [p['ln_in_ga'], p['ln_in_ba'], p['ln_g'], p['ln_b'], p['norm_g'], p['wo_attn'], p['wo_dn'], p['wo_mem'],
              p['ones_bd']]
    return pl.pallas_call(
        _outproj_kernel,
        grid=(B, L // tm),
        in_specs=[tok(D_MODEL), tok(ATTN_WIDTH), tok(MEM_WIDTH), tok(DN_WIDTH), tok(DN_WIDTH), tok(DN_WIDTH)]
                 + [full(a) for a in consts],
        out_specs=tok(D_MODEL),
        out_shape=jax.ShapeDtypeStruct((B, L, D_MODEL), f32),
        compiler_params=pltpu.CompilerParams(dimension_semantics=("parallel", "parallel"),
                                             vmem_limit_bytes=VMEM_LIMIT),
        name="outproj",
    )(x, ya, ym, yf, yb, dz, *consts)


def _t5_bucket(rel):
    nb = REL_BUCKETS // 2
    max_exact = nb // 2
    n = jnp.abs(rel)
    large = max_exact + (jnp.log(jnp.maximum(n, 1).astype(f32) / max_exact)
                         / math.log(REL_MAX_DIST / max_exact) * (nb - max_exact)).astype(jnp.int32)
    large = jnp.minimum(large, nb - 1)
    return jnp.where(rel > 0, nb, 0) + jnp.where(n < max_exact, n, large)


def _prepare(ln_in_g, ln_in_b, rel_bias, w_in, attn_sink, dn_conv, dn_A_log, dn_dt_bias, dn_norm_g,
             w_mem_kv, w_out, ln_g, ln_b):
    offs = np.cumsum([0, ATTN_WIDTH, ATTN_KV_WIDTH, ATTN_KV_WIDTH, ATTN_WIDTH, DN_WIDTH, DN_WIDTH, DN_WIDTH,
                      DN_WIDTH, 2 * DN_HEADS, 2 * DN_HEADS, MEM_WIDTH, MEM_WIDTH])
    (w_aq, w_ak, w_av, w_az, w_dq, w_dk, w_dv, w_dz, w_da, w_db, w_mq, w_mz) = [
        w_in[:, int(offs[j]):int(offs[j + 1])] for j in range(12)]
    order = [kv * ATTN_GROUP + c for c in range(ATTN_GROUP) for kv in range(ATTN_KV_HEADS)]
    head_cols = lambda w: jnp.concatenate([w[:, h * HEAD_DIM:(h + 1) * HEAD_DIM] for h in order], axis=1)
    scale = HEAD_DIM ** -0.5 * LOG2E
    p = {}
    p['w_attn'] = jnp.concatenate([head_cols(w_aq) * scale, w_ak, w_av, head_cols(w_az)], axis=1).astype(bf16)
    p['w_mem'] = jnp.concatenate([w_mq * scale, w_mz], axis=1).astype(bf16)
    p['w_dn'] = jnp.concatenate([w_dq, w_dk, w_dv, w_dz], axis=1).astype(bf16)
    p['w_ab'] = jnp.pad(jnp.concatenate([w_da, w_db], axis=1), ((0, 0), (0, LANES - 4 * DN_HEADS))).astype(bf16)
    p['conv'] = jnp.pad(dn_conv, ((0, SUBLANES - DN_CONV), (0, 0)))
    p['alog'] = jnp.pad(dn_A_log.reshape(1, 2 * DN_HEADS), ((0, 0), (0, LANES - 2 * DN_HEADS)))
    p['dtb'] = jnp.pad(dn_dt_bias.reshape(1, 2 * DN_HEADS), ((0, 0), (0, LANES - 2 * DN_HEADS)))
    p['ln_in_g'] = ln_in_g.reshape(1, D_MODEL)
    p['ln_in_b'] = ln_in_b.reshape(1, D_MODEL)
    p['ln_in_ga'] = DEEPNORM_ALPHA * p['ln_in_g']
    p['ln_in_ba'] = DEEPNORM_ALPHA * p['ln_in_b']
    p['ln_g'] = ln_g.reshape(1, D_MODEL)
    p['ln_b'] = ln_b.reshape(1, D_MODEL)
    p['norm_g'] = jnp.tile(dn_norm_g.reshape(1, HEAD_DIM), (1, DN_HEADS))
    p['w_memk'] = w_mem_kv[:, 0:MEM_WIDTH].astype(bf16)
    p['w_memvt'] = w_mem_kv[:, MEM_WIDTH:2 * MEM_WIDTH].T.astype(bf16)
    p['wo_attn'] = jnp.concatenate([w_out[h * HEAD_DIM:(h + 1) * HEAD_DIM] for h in order], axis=0).astype(bf16)
    p['wo_dn'] = w_out[ATTN_WIDTH:ATTN_WIDTH + DN_WIDTH].astype(bf16)
    p['wo_mem'] = w_out[ATTN_WIDTH + DN_WIDTH:].astype(bf16)
    p['sink'] = jnp.stack([attn_sink[h] for h in order]) * LOG2E

    blk = np.arange(DN_WIDTH) // HEAD_DIM
    bd = (blk[:, None] == blk[None, :])
    p['ones_bd'] = jnp.asarray(bd, bf16)
    p['tri'] = jnp.asarray(np.tril(np.ones((DN_CHUNK, DN_CHUNK))), bf16)
    esel = np.zeros((LANES, 4 * DN_WIDTH), np.float32)
    for c in range(4 * DN_HEADS):
        esel[c, c * HEAD_DIM:(c + 1) * HEAD_DIM] = 1.0
    p['esel'] = jnp.asarray(esel, bf16)

    t = jnp.arange(BLOCK)[None, :]
    s = jnp.arange(3 * BLOCK)[:, None]
    rel = s - BLOCK - t
    onehot = jax.nn.one_hot(_t5_bucket(rel), REL_BUCKETS, dtype=f32)
    rb = jnp.stack([rel_bias[:, h] for h in order], axis=0)
    bias = jnp.einsum('stk,hk->sht', onehot, rb, precision=lax.Precision.HIGHEST)
    bias = jnp.where((jnp.abs(rel) <= WINDOW)[:, None, :], bias * LOG2E, NEG)
    bias = bias.reshape(3 * BLOCK, ATTN_HEADS * BLOCK)
    no_prev = jnp.where(s < BLOCK, NEG, 0.0)
    no_next = jnp.where(s >= 2 * BLOCK, NEG, 0.0)
    p['bias'] = jnp.stack([bias, bias + no_next, bias + no_prev])
    return p


SCAN_SEQS = 4


def _trunk(x, mem, p, tm_in=1024, tm=1024, tb_intra=512, tb_scan=1024):
    B, L, _ = x.shape
    tm_in = min(tm_in, L)
    tm = min(tm, L)
    tb_intra = min(tb_intra, L)
    tb_scan = min(tb_scan, L)
    nbat = math.gcd(B, SCAN_SEQS)
    aq, ak, av, az, mq, mz, dq, dk, dv, dz, gb = _inproj(x, p, tm_in)
    mk, mvt = _memkv(mem, p)
    ya, ym = _attn(aq, ak, av, az, mq, mz, mk, mvt, p)
    m, n, qp, op, gt = _dn_intra(dq, dk, dv, gb, p, tb_intra)
    yf, yb = _dn_scan(m, n, qp, op, gt, tb_scan, nbat)
    return _outproj(x, ya, ym, yf, yb, dz, p, tm)


def kernel(x_prompt, x_sample, mem_prompt, mem_sample, ln_in_g, ln_in_b, rel_bias, w_in, attn_sink, dn_conv,
           dn_A_log, dn_dt_bias, dn_norm_g, w_mem_kv, w_out, ln_g, ln_b):
    p = _prepare(ln_in_g, ln_in_b, rel_bias, w_in[0], attn_sink[0], dn_conv[0], dn_A_log[0], dn_dt_bias[0],
                 dn_norm_g[0], w_mem_kv[0], w_out[0], ln_g[0], ln_b[0])
    return (_trunk(x_prompt, mem_prompt, p), _trunk(x_sample, mem_sample, p))
```

```python
import functools
import math

import jax
import jax.numpy as jnp
import numpy as np
from jax import lax
from jax.experimental import pallas as pl
from jax.experimental.pallas import tpu as pltpu

f32 = jnp.float32
bf16 = jnp.bfloat16

D_MODEL = 1024
HEAD_DIM = 64
ATTN_HEADS = 8
ATTN_KV_HEADS = 2
ATTN_GROUP = ATTN_HEADS // ATTN_KV_HEADS
ATTN_WIDTH = ATTN_HEADS * HEAD_DIM
ATTN_KV_WIDTH = ATTN_KV_HEADS * HEAD_DIM
WINDOW = 128
BLOCK = 128
REL_BUCKETS = 32
REL_MAX_DIST = 128
DN_HEADS = 4
DN_WIDTH = DN_HEADS * HEAD_DIM
DN_CONV = 5
DN_CONV_DIM = 3 * DN_WIDTH
DN_CHUNK = 64
MEM_HEADS = 4
MEM_WIDTH = MEM_HEADS * HEAD_DIM
DEPTH = 1
DEEPNORM_ALPHA = (2 * DEPTH) ** 0.25
LN_EPS = 1e-5
RMS_EPS = 1e-6
NEG = -1e30
LOG2E = 1.4426950408889634

LANES = 128
SUBLANES = 8
HALO = SUBLANES
VMEM_LIMIT = 56 * 1024 * 1024

NT = (((1,), (1,)), ((), ()))
TN = (((0,), (0,)), ((), ()))


def _dot(a, b):
    return jnp.dot(a, b, preferred_element_type=f32)


def _dot_nt(a, b):
    return lax.dot_general(a, b, NT, preferred_element_type=f32)


def _split3(x):
    hi = x.astype(bf16)
    r1 = x - hi.astype(f32)
    mid = r1.astype(bf16)
    lo = (r1 - mid.astype(f32)).astype(bf16)
    return hi, mid, lo


def _dot_exact_lhs(x, sel):
    hi, mid, lo = _split3(x)
    return _dot(hi, sel) + _dot(mid, sel) + _dot(lo, sel)


def _dot_exact_rhs(sel, x):
    hi, mid, lo = _split3(x)
    return _dot(sel, hi) + _dot(sel, mid) + _dot(sel, lo)


def _layer_norm(x, g, b):
    mu = jnp.mean(x, axis=-1, keepdims=True)
    xc = x - mu
    var = jnp.mean(xc * xc, axis=-1, keepdims=True)
    return xc * lax.rsqrt(var + LN_EPS) * g + b


def _silu(x):
    return x / (1.0 + jnp.exp(-x))


def _inproj_kernel(x_ref, xp_ref, xn_ref, lng_ref, lnb_ref, wattn_ref, wmem_ref, wdn_ref, wab_ref,
                   conv_ref, alog_ref, dtb_ref, ones_ref,
                   aq_ref, ak_ref, av_ref, az_ref, mq_ref, mz_ref, dq_ref, dk_ref, dv_ref, dz_ref, gb_ref, xa_ref,
                   hbuf, *, tm):
    i = pl.program_id(1)
    n = pl.num_programs(1)
    g = lng_ref[...]
    b = lnb_ref[...]
    xln = _layer_norm(x_ref[0], g, b)
    xa_ref[0] = DEEPNORM_ALPHA * xln
    xb = xln.astype(bf16)

    wqkv = wdn_ref[:, 0:DN_CONV_DIM]
    hp = _dot(_layer_norm(xp_ref[0], g, b).astype(bf16), wqkv)
    hn = _dot(_layer_norm(xn_ref[0], g, b).astype(bf16), wqkv)
    hbuf[0:HALO, :] = jnp.where(i > 0, hp, 0.0)
    hbuf[HALO:HALO + tm, :] = _dot(xb, wqkv)
    hbuf[HALO + tm:HALO + tm + HALO, :] = jnp.where(i < n - 1, hn, 0.0)
    ab = _dot(xb, wab_ref[...])

    aq_ref[0] = _dot(xb, wattn_ref[:, 0:512]).astype(bf16)

    rows = tm + 2 * HALO
    half = DN_CONV // 2
    hfull = hbuf[...]
    c = conv_ref[half:half + 1, :] * hfull[HALO:HALO + tm, :]
    for j in range(DN_CONV):
        if j != half:
            c = c + conv_ref[j:j + 1, :] * pltpu.roll(hfull, (half - j) % rows, axis=0)[HALO:HALO + tm, :]

    kv = _dot(xb, wattn_ref[:, 512:768]).astype(bf16)
    ak_ref[0] = kv[:, 0:ATTN_KV_WIDTH]
    av_ref[0] = kv[:, ATTN_KV_WIDTH:2 * ATTN_KV_WIDTH]
    az_ref[0] = _silu(_dot(xb, wattn_ref[:, 768:1280])).astype(bf16)

    c = _silu(c)
    ones = ones_ref[...]

    def l2n(t, scale):
        ss = _dot((t * t).astype(bf16), ones)
        return t * (lax.rsqrt(ss + 1e-6) * scale)

    dq_ref[0] = l2n(c[:, 0:DN_WIDTH], HEAD_DIM ** -0.5).astype(bf16)
    mq_ref[0] = _dot(xb, wmem_ref[:, 0:256]).astype(bf16)
    dk_ref[0] = l2n(c[:, DN_WIDTH:2 * DN_WIDTH], 1.0).astype(bf16)
    mz_ref[0] = _silu(_dot(xb, wmem_ref[:, 256:512])).astype(bf16)
    dv_ref[0] = c[:, 2 * DN_WIDTH:3 * DN_WIDTH].astype(bf16)
    dz_ref[0] = _silu(_dot(xb, wdn_ref[:, DN_CONV_DIM:DN_CONV_DIM + DN_WIDTH])).astype(bf16)

    z = ab + dtb_ref[...]
    sp = jnp.maximum(z, 0.0) + jnp.log1p(jnp.exp(-jnp.abs(z)))
    gdec = -jnp.exp(alog_ref[...]) * sp
    beta = 1.0 / (1.0 + jnp.exp(-ab))
    lane = lax.broadcasted_iota(jnp.int32, ab.shape, 1)
    gb_ref[0] = jnp.where(lane < 2 * DN_HEADS, gdec, beta)


def _inproj(x, p, tm):
    B, L, _ = x.shape
    nt = L // tm
    r8 = tm // HALO
    full = lambda a: pl.BlockSpec(a.shape, lambda b, i: (0,) * a.ndim)
    consts = [p['ln_in_g'], p['ln_in_b'], p['w_attn'], p['w_mem'], p['w_dn'], p['w_ab'],
              p['conv'], p['alog'], p['dtb'], p['ones_bd']]
    tok = lambda w: pl.BlockSpec((1, tm, w), lambda b, i: (b, i, 0))
    widths = [512, 128, 128, 512, 256, 256, 256, 256, 256, 256]
    out_shape = ([jax.ShapeDtypeStruct((B, L, w), bf16) for w in widths]
                 + [jax.ShapeDtypeStruct((B, L, LANES), f32), jax.ShapeDtypeStruct((B, L, D_MODEL), f32)])
    return pl.pallas_call(
        functools.partial(_inproj_kernel, tm=tm),
        grid=(B, nt),
        in_specs=[pl.BlockSpec((1, tm, D_MODEL), lambda b, i: (b, i, 0)),
                  pl.BlockSpec((1, HALO, D_MODEL), lambda b, i: (b, jnp.maximum(i * r8 - 1, 0), 0)),
                  pl.BlockSpec((1, HALO, D_MODEL), lambda b, i: (b, jnp.minimum((i + 1) * r8, L // HALO - 1), 0)),
                  ] + [full(a) for a in consts],
        out_specs=[tok(w) for w in widths] + [tok(LANES), tok(D_MODEL)],
        out_shape=out_shape,
        scratch_shapes=[pltpu.VMEM((tm + 2 * HALO, DN_CONV_DIM), f32)],
        compiler_params=pltpu.CompilerParams(dimension_semantics=("parallel", "parallel"),
                                             vmem_limit_bytes=VMEM_LIMIT),
        name="inproj",
    )(x, x, x, *consts)


def _memkv_kernel(m_ref, lng_ref, lnb_ref, wk_ref, wvt_ref, k_ref, vt_ref):
    mb = _layer_norm(m_ref[0], lng_ref[...], lnb_ref[...]).astype(bf16)
    k_ref[0] = _dot(mb, wk_ref[...]).astype(bf16)
    vt_ref[0] = _dot_nt(wvt_ref[...], mb).astype(bf16)


def _memkv(mem, p):
    B, M, _ = mem.shape
    return pl.pallas_call(
        _memkv_kernel,
        grid=(B,),
        in_specs=[pl.BlockSpec((1, M, D_MODEL), lambda b: (b, 0, 0)),
                  pl.BlockSpec((1, D_MODEL), lambda b: (0, 0)),
                  pl.BlockSpec((1, D_MODEL), lambda b: (0, 0)),
                  pl.BlockSpec((D_MODEL, MEM_WIDTH), lambda b: (0, 0)),
                  pl.BlockSpec((MEM_WIDTH, D_MODEL), lambda b: (0, 0))],
        out_specs=[pl.BlockSpec((1, M, MEM_WIDTH), lambda b: (b, 0, 0)),
                   pl.BlockSpec((1, MEM_WIDTH, M), lambda b: (b, 0, 0))],
        out_shape=[jax.ShapeDtypeStruct((B, M, MEM_WIDTH), bf16), jax.ShapeDtypeStruct((B, MEM_WIDTH, M), bf16)],
        compiler_params=pltpu.CompilerParams(dimension_semantics=("parallel",), vmem_limit_bytes=VMEM_LIMIT),
        name="memkv",
    )(mem, p['ln_in_g'], p['ln_in_b'], p['w_memk'], p['w_memvt'])


def _softmax_cols(s, extra=None):
    m = jnp.max(s, axis=0, keepdims=True)
    if extra is not None:
        m = jnp.maximum(m, extra)
    e = jnp.exp2(s - m)
    return e.astype(bf16), (None if extra is None else jnp.exp2(extra - m))


QB = 4


def _attn_kernel(sink_ref, aq_ref, ak_ref, av_ref, az_ref, mq_ref, mz_ref, mk_ref, mvt_ref,
                 bias_first_ref, bias_mid_ref, bias_last_ref, ya_ref, ym_ref):
    j = pl.program_id(1)
    nb = QB * pl.num_programs(1)
    bias_refs = [bias_first_ref] + [bias_mid_ref] * (QB - 2) + [bias_last_ref]
    lane = lax.broadcasted_iota(jnp.int32, (BLOCK, LANES), 1)
    low = lane < HEAD_DIM
    zero = jnp.zeros((BLOCK, LANES), bf16)

    def rows(ref, blk):
        return ref[0, pl.ds(pl.multiple_of(blk * BLOCK, BLOCK), BLOCK), :]

    def band(ref, i):
        return jnp.concatenate([rows(ref, jnp.maximum(i - 1, 0)), rows(ref, i), rows(ref, jnp.minimum(i + 1, nb - 1))],
                               axis=0)

    def stack_q(q):
        parts = []
        for c in range(ATTN_GROUP):
            t = q[:, c * LANES:(c + 1) * LANES]
            parts.append(jnp.where(low, t, zero))
            parts.append(jnp.where(low, zero, t))
        return jnp.concatenate(parts, axis=0)

    blks = [j * QB + t for t in range(QB)]
    kcat = [band(ak_ref, i) for i in blks]
    vcat = [band(av_ref, i) for i in blks]
    qs = [stack_q(aq_ref[0, t * BLOCK:(t + 1) * BLOCK, :]) for t in range(QB)]
    mq = mq_ref[0]
    nq = QB * BLOCK
    lane_m = lax.broadcasted_iota(jnp.int32, (nq, MEM_WIDTH), 1) // HEAD_DIM
    zm = jnp.zeros((nq, MEM_WIDTH), bf16)
    mqs = jnp.concatenate([jnp.where(lane_m == h, mq, zm) for h in range(MEM_HEADS)], axis=0)

    st = [_dot_nt(k, q) + b[0] for k, q, b in zip(kcat, qs, bias_refs)]
    smt = _dot_nt(mk_ref[0], mqs)
    ones_v = jnp.ones((2 * SUBLANES, 3 * BLOCK), bf16)
    vt = [jnp.concatenate([v.astype(f32).T.astype(bf16), ones_v], axis=0) for v in vcat]
    mvt = jnp.concatenate([mvt_ref[0], jnp.ones((2 * SUBLANES, mk_ref.shape[1]), bf16)], axis=0)

    soft = [[_softmax_cols(s[:, h * BLOCK:(h + 1) * BLOCK], sink_ref[h]) for h in range(ATTN_HEADS)] for s in st]
    softm = [_softmax_cols(smt[:, h * nq:(h + 1) * nq]) for h in range(MEM_HEADS)]

    ot = [_dot(v, jnp.concatenate([p_ for p_, _ in sf], axis=1)) for v, sf in zip(vt, soft)]
    omt = _dot(mvt, jnp.concatenate([p_ for p_, _ in softm], axis=1))

    half = HEAD_DIM
    vrows = ATTN_KV_WIDTH
    for t in range(QB):
        az = az_ref[0, t * BLOCK:(t + 1) * BLOCK, :].astype(f32)
        rinv = [1.0 / (ot[t][vrows:vrows + 1, h * BLOCK:(h + 1) * BLOCK] + soft[t][h][1]) for h in range(ATTN_HEADS)]
        for c in range(ATTN_GROUP):
            o0 = ot[t][0:half, (2 * c) * BLOCK:(2 * c + 1) * BLOCK] * rinv[2 * c]
            o1 = ot[t][half:2 * half, (2 * c + 1) * BLOCK:(2 * c + 2) * BLOCK] * rinv[2 * c + 1]
            y = jnp.concatenate([o0, o1], axis=0).T * az[:, c * LANES:(c + 1) * LANES]
            ya_ref[0, t * BLOCK:(t + 1) * BLOCK, c * LANES:(c + 1) * LANES] = y.astype(bf16)
    ymt = jnp.concatenate([omt[h * HEAD_DIM:(h + 1) * HEAD_DIM, h * nq:(h + 1) * nq]
                           * (1.0 / omt[MEM_WIDTH:MEM_WIDTH + 1, h * nq:(h + 1) * nq])
                           for h in range(MEM_HEADS)], axis=0)
    ym_ref[0] = (ymt.T * mz_ref[0].astype(f32)).astype(bf16)


def _attn(aq, ak, av, az, mq, mz, mk, mvt, p):
    B, L, _ = aq.shape
    assert L % (QB * BLOCK) == 0 and QB >= 2
    ns = L // (QB * BLOCK)
    M = mk.shape[1]
    tok = lambda w: pl.BlockSpec((1, QB * BLOCK, w), lambda b, j: (b, j, 0))
    seq = lambda w: pl.BlockSpec((1, L, w), lambda b, j: (b, 0, 0))
    bias_shape = (1, 3 * BLOCK, ATTN_HEADS * BLOCK)
    return pl.pallas_call(
        _attn_kernel,
        grid=(B, ns),
        in_specs=[pl.BlockSpec(memory_space=pltpu.SMEM),
                  tok(ATTN_WIDTH), seq(ATTN_KV_WIDTH), seq(ATTN_KV_WIDTH), tok(ATTN_WIDTH),
                  tok(MEM_WIDTH), tok(MEM_WIDTH),
                  pl.BlockSpec((1, M, MEM_WIDTH), lambda b, j: (b, 0, 0)),
                  pl.BlockSpec((1, MEM_WIDTH, M), lambda b, j: (b, 0, 0)),
                  pl.BlockSpec(bias_shape, lambda b, j: (jnp.where(j == 0, 2, 0), 0, 0)),
                  pl.BlockSpec(bias_shape, lambda b, j: (0, 0, 0)),
                  pl.BlockSpec(bias_shape, lambda b, j: (jnp.where(j == ns - 1, 1, 0), 0, 0))],
        out_specs=[tok(ATTN_WIDTH), tok(MEM_WIDTH)],
        out_shape=[jax.ShapeDtypeStruct((B, L, ATTN_WIDTH), bf16), jax.ShapeDtypeStruct((B, L, MEM_WIDTH), bf16)],
        compiler_params=pltpu.CompilerParams(dimension_semantics=("parallel", "arbitrary"),
                                             vmem_limit_bytes=VMEM_LIMIT),
        name="attn",
    )(p['sink'], aq, ak, av, az, mq, mz, mk, mvt, p['bias'], p['bias'], p['bias'])


def _bd2(y, lo):
    z = jnp.zeros_like(y)
    return jnp.concatenate([jnp.where(lo, y, z), jnp.where(lo, z, y)], axis=0)


def _halves(x):
    return x[:, 0:LANES], x[:, LANES:2 * LANES]


def _dn_intra_kernel(q_ref, k_ref, v_ref, gb_ref, tri_ref, esel_ref,
                     m_ref, n_ref, qp_ref, op_ref, gt_ref, bc_ref, *, nc, group):
    C = DN_CHUNK
    W = DN_WIDTH
    ri = lax.broadcasted_iota(jnp.int32, (C, W), 0)
    lj = lax.broadcasted_iota(jnp.int32, (C, W), 1) % C
    eye = ri == lj
    eye_f = jnp.where(eye, 1.0, 0.0)
    masks = ((ri >= lj, ri > lj), (ri <= lj, ri < lj))
    lo = lax.broadcasted_iota(jnp.int32, (C, LANES), 1) < HEAD_DIM
    col = lax.broadcasted_iota(jnp.int32, (C, LANES), 1)
    is_bwd = (col >= DN_HEADS) & (col < 2 * DN_HEADS)
    is_beta = col >= 2 * DN_HEADS
    tri = tri_ref[...]

    srcs = []
    for c in range(nc):
        gbv = gb_ref[0, c * C:(c + 1) * C, :]
        gc = _dot_exact_rhs(tri, gbv)
        srcs.append(jnp.where(is_beta, gbv, jnp.where(is_bwd, gc[C - 1:C, :] - gc + gbv, gc)))
    hi, mid, _ = _split3(jnp.concatenate(srcs, axis=0))
    bc_ref[:, 0:2 * W] = _dot(hi, esel_ref[:, 0:2 * W]) + _dot(mid, esel_ref[:, 0:2 * W])
    bc_ref[:, 2 * W:4 * W] = _dot(hi, esel_ref[:, 2 * W:4 * W])

    def pmm(x, y):
        return [_dot(xh, _bd2(yh, lo)) for xh, yh in zip(_halves(x), _halves(y))]

    def cat(parts):
        return jnp.concatenate(parts, axis=1)

    def body(gi, carry):
        cis = [gi * group + t for t in range(group)]
        sls = [pl.ds(pl.multiple_of(ci * C, C), C) for ci in cis]
        qb = [q_ref[0, sl, :] for sl in sls]
        kb = [k_ref[0, sl, :] for sl in sls]
        vb = [v_ref[0, sl, :] for sl in sls]
        gram = [cat([_dot_nt(xh, _bd2(kh, lo)) for xh, kh in zip(_halves(jnp.concatenate([k, q], axis=0)), _halves(k))])
                for k, q in zip(kb, qb)]
        inst = [(t, d) for t in range(group) for d in range(2)]
        gq = [bc_ref[sls[t], d * W:(d + 1) * W] for t, d in inst]
        bq = [bc_ref[sls[t], (2 + d) * W:(3 + d) * W] for t, d in inst]
        glast = [g[C - 1:C, :] if d == 0 else g[0:1, :] for g, (t, d) in zip(gq, inst)]
        r = [jnp.sum(jnp.where(eye, g, 0.0), axis=0, keepdims=True) for g in gq]
        dec = [jnp.where(masks[d][0], jnp.exp(jnp.minimum(g - rr, 0.0)), 0.0) for g, rr, (t, d) in zip(gq, r, inst)]
        nn = [jnp.where(masks[d][1], -(gram[t][0:C] * b * dc), 0.0) for b, dc, (t, d) in zip(bq, dec, inst)]
        tm = [eye_f + a for a in nn]
        nb_ = [a.astype(bf16) for a in nn]
        x = [cat(pmm(a, a)) for a in nb_]
        for s_ in range(5):
            xb = [a.astype(bf16) for a in x]
            if s_ < 4:
                rr = [cat(pmm(jnp.concatenate([a.astype(bf16), b_], axis=0), b_)) for a, b_ in zip(tm, xb)]
                tm = [a + b_[0:C] for a, b_ in zip(tm, rr)]
                x = [b_[C:2 * C] for b_ in rr]
            else:
                tm = [a + cat(pmm(a.astype(bf16), b_)) for a, b_ in zip(tm, xb)]
        tb = [a.astype(bf16) for a in tm]
        eg = [jnp.exp(g) for g in gq]
        vbeta = [(vb[t].astype(f32) * b).astype(bf16) for b, (t, d) in zip(bq, inst)]
        kbg = [(kb[t].astype(f32) * b * e).astype(bf16) for b, e, (t, d) in zip(bq, eg, inst)]
        uw = [[_dot(th, jnp.concatenate([_bd2(vh, lo), _bd2(kh, lo)], axis=1)).astype(bf16)
               for th, vh, kh in zip(_halves(a), _halves(v_), _halves(k_))]
              for a, v_, k_ in zip(tb, vbeta, kbg)]
        kd = [(kb[t].astype(f32) * jnp.exp(gl - g)).astype(bf16) for g, gl, (t, d) in zip(gq, glast, inst)]
        fm = [[lax.dot_general(kh, uwp, TN, preferred_element_type=f32) for kh, uwp in zip(_halves(k_), uw_)]
              for k_, uw_ in zip(kd, uw)]
        qkb = [(gram[t][C:2 * C] * dc).astype(bf16) for dc, (t, d) in zip(dec, inst)]
        qo = [[_dot(qh, jnp.concatenate([_bd2(uwp[:, 0:LANES], lo), _bd2(uwp[:, LANES:2 * LANES], lo)], axis=1))
               for qh, uwp in zip(_halves(a), uw_)] for a, uw_ in zip(qkb, uw)]
        for idx, (t, d) in enumerate(inst):
            ci, sl = cis[t], sls[t]
            f0, f1 = fm[idx]
            n_ref[0, d, ci] = cat([jnp.where(lo, f[0:C, 0:LANES], f[C:2 * C, 0:LANES])
                                   for f in (f0, f1)]).astype(bf16)
            m_ref[0, d, ci] = cat([-jnp.where(lo, f[0:C, LANES:2 * LANES], f[C:2 * C, LANES:2 * LANES])
                                   for f in (f0, f1)]).astype(bf16)
            qd = qb[t].astype(f32) * eg[idx]
            qp_ref[0, d, sl, :] = (qd - cat([p_[:, LANES:2 * LANES] for p_ in qo[idx]])).astype(bf16)
            op_ref[0, d, sl, :] = cat([p_[:, 0:LANES] for p_ in qo[idx]]).astype(bf16)
            gt_ref[0, d, ci] = jnp.exp(glast[idx])
        return carry

    lax.fori_loop(0, nc // group, body, 0)


def _dn_intra(dq, dk, dv, gb, p, tb, group=4):
    B, L, _ = dq.shape
    nc = tb // DN_CHUNK
    nchunk = L // DN_CHUNK
    tok = lambda w: pl.BlockSpec((1, tb, w), lambda b, i: (b, i, 0))
    full = lambda a: pl.BlockSpec(a.shape, lambda b, i: (0,) * a.ndim)
    consts = [p['tri'], p['esel']]
    return pl.pallas_call(
        functools.partial(_dn_intra_kernel, nc=nc, group=group),
        grid=(B, L // tb),
        in_specs=[tok(DN_WIDTH), tok(DN_WIDTH), tok(DN_WIDTH), tok(LANES)] + [full(a) for a in consts],
        out_specs=[pl.BlockSpec((1, 2, nc, DN_CHUNK, DN_WIDTH), lambda b, i: (b, 0, i, 0, 0)),
                   pl.BlockSpec((1, 2, nc, DN_CHUNK, DN_WIDTH), lambda b, i: (b, 0, i, 0, 0)),
                   pl.BlockSpec((1, 2, tb, DN_WIDTH), lambda b, i: (b, 0, i, 0)),
                   pl.BlockSpec((1, 2, tb, DN_WIDTH), lambda b, i: (b, 0, i, 0)),
                   pl.BlockSpec((1, 2, nc, 1, DN_WIDTH), lambda b, i: (b, 0, i, 0, 0))],
        out_shape=[jax.ShapeDtypeStruct((B, 2, nchunk, DN_CHUNK, DN_WIDTH), bf16),
                   jax.ShapeDtypeStruct((B, 2, nchunk, DN_CHUNK, DN_WIDTH), bf16),
                   jax.ShapeDtypeStruct((B, 2, L, DN_WIDTH), bf16),
                   jax.ShapeDtypeStruct((B, 2, L, DN_WIDTH), bf16),
                   jax.ShapeDtypeStruct((B, 2, nchunk, 1, DN_WIDTH), f32)],
        scratch_shapes=[pltpu.VMEM((tb, 4 * DN_WIDTH), f32)],
        compiler_params=pltpu.CompilerParams(dimension_semantics=("parallel", "parallel"),
                                             vmem_limit_bytes=VMEM_LIMIT),
        name="dn_intra",
    )(dq, dk, dv, gb, *consts)


def _dn_scan_kernel(mf_ref, nf_ref, qf_ref, of_ref, gf_ref, mb_ref, nb_ref, qb_ref, ob_ref, gbk_ref,
                    yf_ref, yb_ref, s_ref, *, nc, nbat):
    C = DN_CHUNK

    @pl.when(pl.program_id(1) == 0)
    def _():
        s_ref[...] = jnp.zeros_like(s_ref)

    lo = lax.broadcasted_iota(jnp.int32, (C, LANES), 1) < HEAD_DIM
    dirs = ((mf_ref, nf_ref, qf_ref, of_ref, gf_ref, yf_ref), (mb_ref, nb_ref, qb_ref, ob_ref, gbk_ref, yb_ref))

    def step(j, carry):
        cis = (j, nc - 1 - j)
        sls = [pl.ds(pl.multiple_of(ci * C, C), C) for ci in cis]
        seqs = [(b, d) for b in range(nbat) for d in range(2)]
        inst = [(b, d, h) for b, d in seqs for h in range(2)]
        hs = lambda a, h: a[:, h * LANES:(h + 1) * LANES]
        m = {(b, d): dirs[d][0][b, 0, cis[d]] for b, d in seqs}
        q = {(b, d): dirs[d][2][b, 0, sls[d], :] for b, d in seqs}
        s = [s_ref[b, d, h] for b, d, h in inst]
        lhs = [jnp.concatenate([_bd2(hs(m[b, d], h), lo), hs(q[b, d], h)], axis=0) for b, d, h in inst]
        rr = [_dot(a, b_.astype(bf16)) for a, b_ in zip(lhs, s)]
        for idx, (b, d, h) in enumerate(inst):
            gt = dirs[d][4][b, 0, cis[d]]
            nn = dirs[d][1][b, 0, cis[d]]
            s_ref[b, d, h] = s[idx] * hs(gt, h) + rr[idx][0:2 * C] + _bd2(hs(nn, h), lo).astype(f32)
        for k, (b, d) in enumerate(seqs):
            y = jnp.concatenate([rr[2 * k][2 * C:3 * C], rr[2 * k + 1][2 * C:3 * C]], axis=1)
            dirs[d][5][b, sls[d], :] = (y + dirs[d][3][b, 0, sls[d], :].astype(f32)).astype(bf16)
        return carry

    lax.fori_loop(0, nc, step, 0)


def _dn_scan(m, n, qp, op, gt, tb, nbat):
    B, _, nchunk, C, W = m.shape
    L = nchunk * C
    nc = tb // C
    ns = L // tb
    assert B % nbat == 0
    fwd5 = lambda last: pl.BlockSpec((nbat, 1, nc, last, W), lambda b, i: (b, 0, i, 0, 0))
    bwd5 = lambda last: pl.BlockSpec((nbat, 1, nc, last, W), lambda b, i: (b, 1, ns - 1 - i, 0, 0))
    fwd4 = pl.BlockSpec((nbat, 1, tb, W), lambda b, i: (b, 0, i, 0))
    bwd4 = pl.BlockSpec((nbat, 1, tb, W), lambda b, i: (b, 1, ns - 1 - i, 0))
    return pl.pallas_call(
        functools.partial(_dn_scan_kernel, nc=nc, nbat=nbat),
        grid=(B // nbat, ns),
        in_specs=[fwd5(C), fwd5(C), fwd4, fwd4, fwd5(1), bwd5(C), bwd5(C), bwd4, bwd4, bwd5(1)],
        out_specs=[pl.BlockSpec((nbat, tb, W), lambda b, i: (b, i, 0)),
                   pl.BlockSpec((nbat, tb, W), lambda b, i: (b, ns - 1 - i, 0))],
        out_shape=[jax.ShapeDtypeStruct((B, L, W), bf16), jax.ShapeDtypeStruct((B, L, W), bf16)],
        scratch_shapes=[pltpu.VMEM((nbat, 2, 2, 2 * C, LANES), f32)],
        compiler_params=pltpu.CompilerParams(dimension_semantics=("parallel", "arbitrary"),
                                             vmem_limit_bytes=VMEM_LIMIT),
        name="dn_scan",
    )(m, n, qp, op, gt, m, n, qp, op, gt)


def _outproj_kernel(xa_ref, ya_ref, ym_ref, yf_ref, yb_ref, dz_ref, lg_ref, lb_ref,
                    ng_ref, wa_ref, wd_ref, wm_ref, ones_ref, o_ref):
    o = yf_ref[0].astype(f32) + yb_ref[0].astype(f32)
    ms = _dot((o * o).astype(bf16), ones_ref[...]) * (1.0 / HEAD_DIM)
    yd = o * lax.rsqrt(ms + RMS_EPS) * ng_ref[...] * dz_ref[0].astype(f32)
    y = _dot(ya_ref[0], wa_ref[...]) + _dot(yd.astype(bf16), wd_ref[...]) + _dot(ym_ref[0], wm_ref[...])
    o_ref[0] = _layer_norm(xa_ref[0] + y, lg_ref[...], lb_ref[...])


def _outproj(xa, ya, ym, yf, yb, dz, p, tm):
    B, L, _ = xa.shape
    tok = lambda w: pl.BlockSpec((1, tm, w), lambda b, i: (b, i, 0))
    full = lambda a: pl.BlockSpec(a.shape, lambda b, i: (0,) * a.ndim)
    consts = [p['ln_g'], p['ln_b'], p['norm_g'], p['wo_attn'], p['wo_dn'], p['wo_mem'], p['ones_bd']]
    return pl.pallas_call(
        _outproj_kernel,
        grid=(B, L // tm),
        in_specs=[tok(D_MODEL), tok(ATTN_WIDTH), tok(MEM_WIDTH), tok(DN_WIDTH), tok(DN_WIDTH), tok(DN_WIDTH)]
                 + [full(a) for a in consts],
        out_specs=tok(D_MODEL),
        out_shape=jax.ShapeDtypeStruct((B, L, D_MODEL), f32),
        compiler_params=pltpu.CompilerParams(dimension_semantics=("parallel", "parallel"),
                                             vmem_limit_bytes=VMEM_LIMIT),
        name="outproj",
    )(xa, ya, ym, yf, yb, dz, *consts)


def _t5_bucket(rel):
    nb = REL_BUCKETS // 2
    max_exact = nb // 2
    n = jnp.abs(rel)
    large = max_exact + (jnp.log(jnp.maximum(n, 1).astype(f32) / max_exact)
                         / math.log(REL_MAX_DIST / max_exact) * (nb - max_exact)).astype(jnp.int32)
    large = jnp.minimum(large, nb - 1)
    return jnp.where(rel > 0, nb, 0) + jnp.where(n < max_exact, n, large)


def _prepare(ln_in_g, ln_in_b, rel_bias, w_in, attn_sink, dn_conv, dn_A_log, dn_dt_bias, dn_norm_g,
             w_mem_kv, w_out, ln_g, ln_b):
    offs = np.cumsum([0, ATTN_WIDTH, ATTN_KV_WIDTH, ATTN_KV_WIDTH, ATTN_WIDTH, DN_WIDTH, DN_WIDTH, DN_WIDTH,
                      DN_WIDTH, 2 * DN_HEADS, 2 * DN_HEADS, MEM_WIDTH, MEM_WIDTH])
    (w_aq, w_ak, w_av, w_az, w_dq, w_dk, w_dv, w_dz, w_da, w_db, w_mq, w_mz) = [
        w_in[:, int(offs[j]):int(offs[j + 1])] for j in range(12)]
    order = [kv * ATTN_GROUP + c for c in range(ATTN_GROUP) for kv in range(ATTN_KV_HEADS)]
    head_cols = lambda w: jnp.concatenate([w[:, h * HEAD_DIM:(h + 1) * HEAD_DIM] for h in order], axis=1)
    scale = HEAD_DIM ** -0.5 * LOG2E
    p = {}
    p['w_attn'] = jnp.concatenate([head_cols(w_aq) * scale, w_ak, w_av, head_cols(w_az)], axis=1).astype(bf16)
    p['w_mem'] = jnp.concatenate([w_mq * scale, w_mz], axis=1).astype(bf16)
    p['w_dn'] = jnp.concatenate([w_dq, w_dk, w_dv, w_dz], axis=1).astype(bf16)
    p['w_ab'] = jnp.pad(jnp.concatenate([w_da, w_db], axis=1), ((0, 0), (0, LANES - 4 * DN_HEADS))).astype(bf16)
    p['conv'] = jnp.pad(dn_conv, ((0, SUBLANES - DN_CONV), (0, 0)))
    p['alog'] = jnp.pad(dn_A_log.reshape(1, 2 * DN_HEADS), ((0, 0), (0, LANES - 2 * DN_HEADS)))
    p['dtb'] = jnp.pad(dn_dt_bias.reshape(1, 2 * DN_HEADS), ((0, 0), (0, LANES - 2 * DN_HEADS)))
    p['ln_in_g'] = ln_in_g.reshape(1, D_MODEL)
    p['ln_in_b'] = ln_in_b.reshape(1, D_MODEL)
    p['ln_g'] = ln_g.reshape(1, D_MODEL)
    p['ln_b'] = ln_b.reshape(1, D_MODEL)
    p['norm_g'] = jnp.tile(dn_norm_g.reshape(1, HEAD_DIM), (1, DN_HEADS))
    p['w_memk'] = w_mem_kv[:, 0:MEM_WIDTH].astype(bf16)
    p['w_memvt'] = w_mem_kv[:, MEM_WIDTH:2 * MEM_WIDTH].T.astype(bf16)
    p['wo_attn'] = jnp.concatenate([w_out[h * HEAD_DIM:(h + 1) * HEAD_DIM] for h in order], axis=0).astype(bf16)
    p['wo_dn'] = w_out[ATTN_WIDTH:ATTN_WIDTH + DN_WIDTH].astype(bf16)
    p['wo_mem'] = w_out[ATTN_WIDTH + DN_WIDTH:].astype(bf16)
    p['sink'] = jnp.stack([attn_sink[h] for h in order]) * LOG2E

    blk = np.arange(DN_WIDTH) // HEAD_DIM
    bd = (blk[:, None] == blk[None, :])
    p['ones_bd'] = jnp.asarray(bd, bf16)
    p['tri'] = jnp.asarray(np.tril(np.ones((DN_CHUNK, DN_CHUNK))), bf16)
    esel = np.zeros((LANES, 4 * DN_WIDTH), np.float32)
    for c in range(4 * DN_HEADS):
        esel[c, c * HEAD_DIM:(c + 1) * HEAD_DIM] = 1.0
    p['esel'] = jnp.asarray(esel, bf16)

    t = jnp.arange(BLOCK)[None, :]
    s = jnp.arange(3 * BLOCK)[:, None]
    rel = s - BLOCK - t
    onehot = jax.nn.one_hot(_t5_bucket(rel), REL_BUCKETS, dtype=f32)
    rb = jnp.stack([rel_bias[:, h] for h in order], axis=0)
    bias = jnp.einsum('stk,hk->sht', onehot, rb, precision=lax.Precision.HIGHEST)
    bias = jnp.where((jnp.abs(rel) <= WINDOW)[:, None, :], bias * LOG2E, NEG)
    bias = bias.reshape(3 * BLOCK, ATTN_HEADS * BLOCK)
    no_prev = jnp.where(s < BLOCK, NEG, 0.0)
    no_next = jnp.where(s >= 2 * BLOCK, NEG, 0.0)
    p['bias'] = jnp.stack([bias, bias + no_next, bias + no_prev])
    return p


SCAN_SEQS = 4


def _trunk(x, mem, p, tm_in=1024, tm=1024, tb_intra=512, tb_scan=1024):
    B, L, _ = x.shape
    tm_in = min(tm_in, L)
    tm = min(tm, L)
    tb_intra = min(tb_intra, L)
    tb_scan = min(tb_scan, L)
    nbat = math.gcd(B, SCAN_SEQS)
    aq, ak, av, az, mq, mz, dq, dk, dv, dz, gb, xa = _inproj(x, p, tm_in)
    mk, mvt = _memkv(mem, p)
    ya, ym = _attn(aq, ak, av, az, mq, mz, mk, mvt, p)
    m, n, qp, op, gt = _dn_intra(dq, dk, dv, gb, p, tb_intra)
    yf, yb = _dn_scan(m, n, qp, op, gt, tb_scan, nbat)
    return _outproj(xa, ya, ym, yf, yb, dz, p, tm)


def kernel(x_prompt, x_sample, mem_prompt, mem_sample, ln_in_g, ln_in_b, rel_bias, w_in, attn_sink, dn_conv,
           dn_A_log, dn_dt_bias, dn_norm_g, w_mem_kv, w_out, ln_g, ln_b):
    p = _prepare(ln_in_g, ln_in_b, rel_bias, w_in[0], attn_sink[0], dn_conv[0], dn_A_log[0], dn_dt_bias[0],
                 dn_norm_g[0], w_mem_kv[0], w_out[0], ln_g[0], ln_b[0])
    return (_trunk(x_prompt, mem_prompt, p), _trunk(x_sample, mem_sample, p))
```

```python
import functools
import math

import jax
import jax.numpy as jnp
import numpy as np
from jax import lax
from jax.experimental import pallas as pl
from jax.experimental.pallas import tpu as pltpu

f32 = jnp.float32
bf16 = jnp.bfloat16

D_MODEL = 1024
HEAD_DIM = 64
ATTN_HEADS = 8
ATTN_KV_HEADS = 2
ATTN_GROUP = ATTN_HEADS // ATTN_KV_HEADS
ATTN_WIDTH = ATTN_HEADS * HEAD_DIM
ATTN_KV_WIDTH = ATTN_KV_HEADS * HEAD_DIM
WINDOW = 128
BLOCK = 128
REL_BUCKETS = 32
REL_MAX_DIST = 128
DN_HEADS = 4
DN_WIDTH = DN_HEADS * HEAD_DIM
DN_CONV = 5
DN_CONV_DIM = 3 * DN_WIDTH
DN_CHUNK = 64
MEM_HEADS = 4
MEM_WIDTH = MEM_HEADS * HEAD_DIM
DEPTH = 1
DEEPNORM_ALPHA = (2 * DEPTH) ** 0.25
LN_EPS = 1e-5
RMS_EPS = 1e-6
NEG = -1e30
LOG2E = 1.4426950408889634

LANES = 128
SUBLANES = 8
HALO = SUBLANES
VMEM_LIMIT = 56 * 1024 * 1024

NT = (((1,), (1,)), ((), ()))
TN = (((0,), (0,)), ((), ()))


def _dot(a, b):
    return jnp.dot(a, b, preferred_element_type=f32)


def _dot_nt(a, b):
    return lax.dot_general(a, b, NT, preferred_element_type=f32)


def _split3(x):
    hi = x.astype(bf16)
    r1 = x - hi.astype(f32)
    mid = r1.astype(bf16)
    lo = (r1 - mid.astype(f32)).astype(bf16)
    return hi, mid, lo


def _dot_exact_lhs(x, sel):
    hi, mid, lo = _split3(x)
    return _dot(hi, sel) + _dot(mid, sel) + _dot(lo, sel)


def _dot_exact_rhs(sel, x):
    hi, mid, lo = _split3(x)
    return _dot(sel, hi) + _dot(sel, mid) + _dot(sel, lo)


def _layer_norm(x, g, b):
    mu = jnp.mean(x, axis=-1, keepdims=True)
    xc = x - mu
    var = jnp.mean(xc * xc, axis=-1, keepdims=True)
    return xc * lax.rsqrt(var + LN_EPS) * g + b


def _silu(x):
    return x / (1.0 + jnp.exp(-x))


def _inproj_kernel(x_ref, xp_ref, xn_ref, lng_ref, lnb_ref, wattn_ref, wmem_ref, wdn_ref, wab_ref,
                   conv_ref, alog_ref, dtb_ref, ones_ref,
                   aq_ref, ak_ref, av_ref, az_ref, mq_ref, mz_ref, dq_ref, dk_ref, dv_ref, dz_ref, gb_ref, xa_ref,
                   hbuf, *, tm, nsub):
    i = pl.program_id(1)
    n = pl.num_programs(1)
    g = lng_ref[...]
    b = lnb_ref[...]
    ts = tm // nsub
    rows = ts + 2 * HALO
    half = DN_CONV // 2
    ones = ones_ref[...]

    def l2n(t, scale):
        ss = _dot((t * t).astype(bf16), ones)
        return t * (lax.rsqrt(ss + 1e-6) * scale)

    def sub_tile(k):
        r0 = k * ts
        rs = slice(r0, r0 + ts)
        st = {}

        def ln():
            xln = _layer_norm(x_ref[0, rs, :], g, b)
            xa_ref[0, rs, :] = DEEPNORM_ALPHA * xln
            st['xb'] = xln.astype(bf16)
            prev = xp_ref[0] if k == 0 else x_ref[0, r0 - HALO:r0, :]
            nxt = xn_ref[0] if k == nsub - 1 else x_ref[0, r0 + ts:r0 + ts + HALO, :]
            st['xsb'] = jnp.concatenate([_layer_norm(prev, g, b), xln, _layer_norm(nxt, g, b)], axis=0).astype(bf16)

        def proj_qkv():
            hbuf[k] = _dot(st['xsb'], wdn_ref[:, 0:DN_CONV_DIM])
            if k == 0:
                hbuf[k, 0:HALO, :] = jnp.where(i > 0, hbuf[k, 0:HALO, :], 0.0)
            if k == nsub - 1:
                hbuf[k, HALO + ts:rows, :] = jnp.where(i < n - 1, hbuf[k, HALO + ts:rows, :], 0.0)
            st['ab'] = _dot(st['xb'], wab_ref[...])

        def proj_aq():
            aq_ref[0, rs, :] = _dot(st['xb'], wattn_ref[:, 0:512]).astype(bf16)

        def conv():
            hfull = hbuf[k]
            c = conv_ref[half:half + 1, :] * hfull[HALO:HALO + ts, :]
            for j in range(DN_CONV):
                if j != half:
                    c = c + conv_ref[j:j + 1, :] * pltpu.roll(hfull, (half - j) % rows, axis=0)[HALO:HALO + ts, :]
            st['c'] = c

        def proj_kv_az():
            kv = _dot(st['xb'], wattn_ref[:, 512:768]).astype(bf16)
            ak_ref[0, rs, :] = kv[:, 0:ATTN_KV_WIDTH]
            av_ref[0, rs, :] = kv[:, ATTN_KV_WIDTH:2 * ATTN_KV_WIDTH]
            az_ref[0, rs, :] = _silu(_dot(st['xb'], wattn_ref[:, 768:1280])).astype(bf16)

        def act_q():
            st['c'] = _silu(st['c'])
            dq_ref[0, rs, :] = l2n(st['c'][:, 0:DN_WIDTH], HEAD_DIM ** -0.5).astype(bf16)

        def proj_dz():
            dz_ref[0, rs, :] = _silu(_dot(st['xb'], wdn_ref[:, DN_CONV_DIM:DN_CONV_DIM + DN_WIDTH])).astype(bf16)

        def act_k():
            dk_ref[0, rs, :] = l2n(st['c'][:, DN_WIDTH:2 * DN_WIDTH], 1.0).astype(bf16)

        def proj_mz():
            mz_ref[0, rs, :] = _silu(_dot(st['xb'], wmem_ref[:, 256:512])).astype(bf16)

        def act_v_gb():
            dv_ref[0, rs, :] = st['c'][:, 2 * DN_WIDTH:3 * DN_WIDTH].astype(bf16)
            ab = st['ab']
            z = ab + dtb_ref[...]
            sp = jnp.maximum(z, 0.0) + jnp.log1p(jnp.exp(-jnp.abs(z)))
            gdec = -jnp.exp(alog_ref[...]) * sp
            beta = 1.0 / (1.0 + jnp.exp(-ab))
            lane = lax.broadcasted_iota(jnp.int32, ab.shape, 1)
            gb_ref[0, rs, :] = jnp.where(lane < 2 * DN_HEADS, gdec, beta)

        def proj_mq():
            mq_ref[0, rs, :] = _dot(st['xb'], wmem_ref[:, 0:256]).astype(bf16)

        return [ln, proj_qkv, proj_aq, conv, proj_kv_az, act_q, proj_dz, act_k, proj_mz, act_v_gb, proj_mq]

    subs = [sub_tile(k) for k in range(nsub)]
    nst = len(subs[0])
    for s in range(nst + SUB_LAG * (nsub - 1)):
        for k in range(nsub):
            if 0 <= s - SUB_LAG * k < nst:
                subs[k][s - SUB_LAG * k]()


SUB_TILE = 512
SUB_LAG = 2


def _inproj(x, p, tm):
    B, L, _ = x.shape
    nsub = max(tm // SUB_TILE, 1)
    nt = L // tm
    r8 = tm // HALO
    full = lambda a: pl.BlockSpec(a.shape, lambda b, i: (0,) * a.ndim)
    consts = [p['ln_in_g'], p['ln_in_b'], p['w_attn'], p['w_mem'], p['w_dn'], p['w_ab'],
              p['conv'], p['alog'], p['dtb'], p['ones_bd']]
    tok = lambda w: pl.BlockSpec((1, tm, w), lambda b, i: (b, i, 0))
    widths = [512, 128, 128, 512, 256, 256, 256, 256, 256, 256]
    out_shape = ([jax.ShapeDtypeStruct((B, L, w), bf16) for w in widths]
                 + [jax.ShapeDtypeStruct((B, L, LANES), f32), jax.ShapeDtypeStruct((B, L, D_MODEL), f32)])
    return pl.pallas_call(
        functools.partial(_inproj_kernel, tm=tm, nsub=nsub),
        grid=(B, nt),
        in_specs=[pl.BlockSpec((1, tm, D_MODEL), lambda b, i: (b, i, 0)),
                  pl.BlockSpec((1, HALO, D_MODEL), lambda b, i: (b, jnp.maximum(i * r8 - 1, 0), 0)),
                  pl.BlockSpec((1, HALO, D_MODEL), lambda b, i: (b, jnp.minimum((i + 1) * r8, L // HALO - 1), 0)),
                  ] + [full(a) for a in consts],
        out_specs=[tok(w) for w in widths] + [tok(LANES), tok(D_MODEL)],
        out_shape=out_shape,
        scratch_shapes=[pltpu.VMEM((nsub, tm // nsub + 2 * HALO, DN_CONV_DIM), f32)],
        compiler_params=pltpu.CompilerParams(dimension_semantics=("parallel", "parallel"),
                                             vmem_limit_bytes=VMEM_LIMIT),
        name="inproj",
    )(x, x, x, *consts)


def _memkv_kernel(m_ref, lng_ref, lnb_ref, wk_ref, wvt_ref, k_ref, vt_ref):
    mb = _layer_norm(m_ref[0], lng_ref[...], lnb_ref[...]).astype(bf16)
    k_ref[0] = _dot(mb, wk_ref[...]).astype(bf16)
    vt_ref[0] = _dot_nt(wvt_ref[...], mb).astype(bf16)


def _memkv(mem, p):
    B, M, _ = mem.shape
    return pl.pallas_call(
        _memkv_kernel,
        grid=(B,),
        in_specs=[pl.BlockSpec((1, M, D_MODEL), lambda b: (b, 0, 0)),
                  pl.BlockSpec((1, D_MODEL), lambda b: (0, 0)),
                  pl.BlockSpec((1, D_MODEL), lambda b: (0, 0)),
                  pl.BlockSpec((D_MODEL, MEM_WIDTH), lambda b: (0, 0)),
                  pl.BlockSpec((MEM_WIDTH, D_MODEL), lambda b: (0, 0))],
        out_specs=[pl.BlockSpec((1, M, MEM_WIDTH), lambda b: (b, 0, 0)),
                   pl.BlockSpec((1, MEM_WIDTH, M), lambda b: (b, 0, 0))],
        out_shape=[jax.ShapeDtypeStruct((B, M, MEM_WIDTH), bf16), jax.ShapeDtypeStruct((B, MEM_WIDTH, M), bf16)],
        compiler_params=pltpu.CompilerParams(dimension_semantics=("parallel",), vmem_limit_bytes=VMEM_LIMIT),
        name="memkv",
    )(mem, p['ln_in_g'], p['ln_in_b'], p['w_memk'], p['w_memvt'])


def _softmax_cols(s, extra=None):
    m = jnp.max(s, axis=0, keepdims=True)
    if extra is not None:
        m = jnp.maximum(m, extra)
    e = jnp.exp2(s - m)
    return e.astype(bf16), (None if extra is None else jnp.exp2(extra - m))


QB = 4


def _attn_kernel(sink_ref, aq_ref, ak_ref, av_ref, az_ref, mq_ref, mz_ref, mk_ref, mvt_ref,
                 bias_first_ref, bias_mid_ref, bias_last_ref, ya_ref, ym_ref):
    j = pl.program_id(1)
    nb = QB * pl.num_programs(1)
    bias_refs = [bias_first_ref] + [bias_mid_ref] * (QB - 2) + [bias_last_ref]
    lane = lax.broadcasted_iota(jnp.int32, (BLOCK, LANES), 1)
    low = lane < HEAD_DIM
    zero = jnp.zeros((BLOCK, LANES), bf16)

    def rows(ref, blk):
        return ref[0, pl.ds(pl.multiple_of(blk * BLOCK, BLOCK), BLOCK), :]

    def band(ref, i):
        return jnp.concatenate([rows(ref, jnp.maximum(i - 1, 0)), rows(ref, i), rows(ref, jnp.minimum(i + 1, nb - 1))],
                               axis=0)

    def stack_q(q):
        parts = []
        for c in range(ATTN_GROUP):
            t = q[:, c * LANES:(c + 1) * LANES]
            parts.append(jnp.where(low, t, zero))
            parts.append(jnp.where(low, zero, t))
        return jnp.concatenate(parts, axis=0)

    blks = [j * QB + t for t in range(QB)]
    kcat = [band(ak_ref, i) for i in blks]
    vcat = [band(av_ref, i) for i in blks]
    qs = [stack_q(aq_ref[0, t * BLOCK:(t + 1) * BLOCK, :]) for t in range(QB)]
    mq = mq_ref[0]
    nq = QB * BLOCK
    lane_m = lax.broadcasted_iota(jnp.int32, (nq, MEM_WIDTH), 1) // HEAD_DIM
    zm = jnp.zeros((nq, MEM_WIDTH), bf16)
    mqs = jnp.concatenate([jnp.where(lane_m == h, mq, zm) for h in range(MEM_HEADS)], axis=0)

    st = [_dot_nt(k, q) + b[0] for k, q, b in zip(kcat, qs, bias_refs)]
    smt = _dot_nt(mk_ref[0], mqs)
    ones_v = jnp.ones((2 * SUBLANES, 3 * BLOCK), bf16)
    vt = [jnp.concatenate([v.astype(f32).T.astype(bf16), ones_v], axis=0) for v in vcat]
    mvt = jnp.concatenate([mvt_ref[0], jnp.ones((2 * SUBLANES, mk_ref.shape[1]), bf16)], axis=0)

    soft = [[_softmax_cols(s[:, h * BLOCK:(h + 1) * BLOCK], sink_ref[h]) for h in range(ATTN_HEADS)] for s in st]
    softm = [_softmax_cols(smt[:, h * nq:(h + 1) * nq]) for h in range(MEM_HEADS)]

    ot = [_dot(v, jnp.concatenate([p_ for p_, _ in sf], axis=1)) for v, sf in zip(vt, soft)]
    omt = _dot(mvt, jnp.concatenate([p_ for p_, _ in softm], axis=1))

    half = HEAD_DIM
    vrows = ATTN_KV_WIDTH
    for t in range(QB):
        az = az_ref[0, t * BLOCK:(t + 1) * BLOCK, :].astype(f32)
        rinv = [1.0 / (ot[t][vrows:vrows + 1, h * BLOCK:(h + 1) * BLOCK] + soft[t][h][1]) for h in range(ATTN_HEADS)]
        for c in range(ATTN_GROUP):
            o0 = ot[t][0:half, (2 * c) * BLOCK:(2 * c + 1) * BLOCK] * rinv[2 * c]
            o1 = ot[t][half:2 * half, (2 * c + 1) * BLOCK:(2 * c + 2) * BLOCK] * rinv[2 * c + 1]
            y = jnp.concatenate([o0, o1], axis=0).T * az[:, c * LANES:(c + 1) * LANES]
            ya_ref[0, t * BLOCK:(t + 1) * BLOCK, c * LANES:(c + 1) * LANES] = y.astype(bf16)
    ymt = jnp.concatenate([omt[h * HEAD_DIM:(h + 1) * HEAD_DIM, h * nq:(h + 1) * nq]
                           * (1.0 / omt[MEM_WIDTH:MEM_WIDTH + 1, h * nq:(h + 1) * nq])
                           for h in range(MEM_HEADS)], axis=0)
    ym_ref[0] = (ymt.T * mz_ref[0].astype(f32)).astype(bf16)


def _attn(aq, ak, av, az, mq, mz, mk, mvt, p):
    B, L, _ = aq.shape
    assert L % (QB * BLOCK) == 0 and QB >= 2
    ns = L // (QB * BLOCK)
    M = mk.shape[1]
    tok = lambda w: pl.BlockSpec((1, QB * BLOCK, w), lambda b, j: (b, j, 0))
    seq = lambda w: pl.BlockSpec((1, L, w), lambda b, j: (b, 0, 0))
    bias_shape = (1, 3 * BLOCK, ATTN_HEADS * BLOCK)
    return pl.pallas_call(
        _attn_kernel,
        grid=(B, ns),
        in_specs=[pl.BlockSpec(memory_space=pltpu.SMEM),
                  tok(ATTN_WIDTH), seq(ATTN_KV_WIDTH), seq(ATTN_KV_WIDTH), tok(ATTN_WIDTH),
                  tok(MEM_WIDTH), tok(MEM_WIDTH),
                  pl.BlockSpec((1, M, MEM_WIDTH), lambda b, j: (b, 0, 0)),
                  pl.BlockSpec((1, MEM_WIDTH, M), lambda b, j: (b, 0, 0)),
                  pl.BlockSpec(bias_shape, lambda b, j: (jnp.where(j == 0, 2, 0), 0, 0)),
                  pl.BlockSpec(bias_shape, lambda b, j: (0, 0, 0)),
                  pl.BlockSpec(bias_shape, lambda b, j: (jnp.where(j == ns - 1, 1, 0), 0, 0))],
        out_specs=[tok(ATTN_WIDTH), tok(MEM_WIDTH)],
        out_shape=[jax.ShapeDtypeStruct((B, L, ATTN_WIDTH), bf16), jax.ShapeDtypeStruct((B, L, MEM_WIDTH), bf16)],
        compiler_params=pltpu.CompilerParams(dimension_semantics=("parallel", "arbitrary"),
                                             vmem_limit_bytes=VMEM_LIMIT),
        name="attn",
    )(p['sink'], aq, ak, av, az, mq, mz, mk, mvt, p['bias'], p['bias'], p['bias'])


def _bd2(y, lo):
    z = jnp.zeros_like(y)
    return jnp.concatenate([jnp.where(lo, y, z), jnp.where(lo, z, y)], axis=0)


def _halves(x):
    return x[:, 0:LANES], x[:, LANES:2 * LANES]


def _dn_intra_kernel(q_ref, k_ref, v_ref, gb_ref, tri_ref, esel_ref,
                     m_ref, n_ref, qp_ref, op_ref, gt_ref, bc_ref, *, nc, group):
    C = DN_CHUNK
    W = DN_WIDTH
    ri = lax.broadcasted_iota(jnp.int32, (C, W), 0)
    lj = lax.broadcasted_iota(jnp.int32, (C, W), 1) % C
    eye = ri == lj
    eye_f = jnp.where(eye, 1.0, 0.0)
    masks = ((ri >= lj, ri > lj), (ri <= lj, ri < lj))
    lo = lax.broadcasted_iota(jnp.int32, (C, LANES), 1) < HEAD_DIM
    col = lax.broadcasted_iota(jnp.int32, (C, LANES), 1)
    is_bwd = (col >= DN_HEADS) & (col < 2 * DN_HEADS)
    is_beta = col >= 2 * DN_HEADS
    tri = tri_ref[...]

    srcs = []
    for c in range(nc):
        gbv = gb_ref[0, c * C:(c + 1) * C, :]
        gc = _dot_exact_rhs(tri, gbv)
        srcs.append(jnp.where(is_beta, gbv, jnp.where(is_bwd, gc[C - 1:C, :] - gc + gbv, gc)))
    hi, mid, _ = _split3(jnp.concatenate(srcs, axis=0))
    bc_ref[:, 0:2 * W] = _dot(hi, esel_ref[:, 0:2 * W]) + _dot(mid, esel_ref[:, 0:2 * W])
    bc_ref[:, 2 * W:4 * W] = _dot(hi, esel_ref[:, 2 * W:4 * W])

    def pmm(x, y):
        return [_dot(xh, _bd2(yh, lo)) for xh, yh in zip(_halves(x), _halves(y))]

    def cat(parts):
        return jnp.concatenate(parts, axis=1)

    def body(gi, carry):
        cis = [gi * group + t for t in range(group)]
        sls = [pl.ds(pl.multiple_of(ci * C, C), C) for ci in cis]
        qb = [q_ref[0, sl, :] for sl in sls]
        kb = [k_ref[0, sl, :] for sl in sls]
        vb = [v_ref[0, sl, :] for sl in sls]
        gram = [cat([_dot_nt(xh, _bd2(kh, lo)) for xh, kh in zip(_halves(jnp.concatenate([k, q], axis=0)), _halves(k))])
                for k, q in zip(kb, qb)]
        inst = [(t, d) for t in range(group) for d in range(2)]
        gq = [bc_ref[sls[t], d * W:(d + 1) * W] for t, d in inst]
        bq = [bc_ref[sls[t], (2 + d) * W:(3 + d) * W] for t, d in inst]
        glast = [g[C - 1:C, :] if d == 0 else g[0:1, :] for g, (t, d) in zip(gq, inst)]
        r = [jnp.sum(jnp.where(eye, g, 0.0), axis=0, keepdims=True) for g in gq]
        dec = [jnp.where(masks[d][0], jnp.exp(jnp.minimum(g - rr, 0.0)), 0.0) for g, rr, (t, d) in zip(gq, r, inst)]
        nn = [jnp.where(masks[d][1], -(gram[t][0:C] * b * dc), 0.0) for b, dc, (t, d) in zip(bq, dec, inst)]
        tm = [eye_f + a for a in nn]
        nb_ = [a.astype(bf16) for a in nn]
        x = [cat(pmm(a, a)) for a in nb_]
        for s_ in range(5):
            xb = [a.astype(bf16) for a in x]
            if s_ < 4:
                rr = [cat(pmm(jnp.concatenate([a.astype(bf16), b_], axis=0), b_)) for a, b_ in zip(tm, xb)]
                tm = [a + b_[0:C] for a, b_ in zip(tm, rr)]
                x = [b_[C:2 * C] for b_ in rr]
            else:
                tm = [a + cat(pmm(a.astype(bf16), b_)) for a, b_ in zip(tm, xb)]
        tb = [a.astype(bf16) for a in tm]
        eg = [jnp.exp(g) for g in gq]
        vbeta = [(vb[t].astype(f32) * b).astype(bf16) for b, (t, d) in zip(bq, inst)]
        kbg = [(kb[t].astype(f32) * b * e).astype(bf16) for b, e, (t, d) in zip(bq, eg, inst)]
        uw = [[_dot(th, jnp.concatenate([_bd2(vh, lo), _bd2(kh, lo)], axis=1)).astype(bf16)
               for th, vh, kh in zip(_halves(a), _halves(v_), _halves(k_))]
              for a, v_, k_ in zip(tb, vbeta, kbg)]
        kd = [(kb[t].astype(f32) * jnp.exp(gl - g)).astype(bf16) for g, gl, (t, d) in zip(gq, glast, inst)]
        fm = [[lax.dot_general(kh, uwp, TN, preferred_element_type=f32) for kh, uwp in zip(_halves(k_), uw_)]
              for k_, uw_ in zip(kd, uw)]
        qkb = [(gram[t][C:2 * C] * dc).astype(bf16) for dc, (t, d) in zip(dec, inst)]
        qo = [[_dot(qh, jnp.concatenate([_bd2(uwp[:, 0:LANES], lo), _bd2(uwp[:, LANES:2 * LANES], lo)], axis=1))
               for qh, uwp in zip(_halves(a), uw_)] for a, uw_ in zip(qkb, uw)]
        for idx, (t, d) in enumerate(inst):
            ci, sl = cis[t], sls[t]
            f0, f1 = fm[idx]
            n_ref[0, d, ci] = cat([jnp.where(lo, f[0:C, 0:LANES], f[C:2 * C, 0:LANES])
                                   for f in (f0, f1)]).astype(bf16)
            m_ref[0, d, ci] = cat([-jnp.where(lo, f[0:C, LANES:2 * LANES], f[C:2 * C, LANES:2 * LANES])
                                   for f in (f0, f1)]).astype(bf16)
            qd = qb[t].astype(f32) * eg[idx]
            qp_ref[0, d, sl, :] = (qd - cat([p_[:, LANES:2 * LANES] for p_ in qo[idx]])).astype(bf16)
            op_ref[0, d, sl, :] = cat([p_[:, 0:LANES] for p_ in qo[idx]]).astype(bf16)
            gt_ref[0, d, ci] = jnp.exp(glast[idx])
        return carry

    lax.fori_loop(0, nc // group, body, 0)


def _dn_intra(dq, dk, dv, gb, p, tb, group=4):
    B, L, _ = dq.shape
    nc = tb // DN_CHUNK
    nchunk = L // DN_CHUNK
    tok = lambda w: pl.BlockSpec((1, tb, w), lambda b, i: (b, i, 0))
    full = lambda a: pl.BlockSpec(a.shape, lambda b, i: (0,) * a.ndim)
    consts = [p['tri'], p['esel']]
    return pl.pallas_call(
        functools.partial(_dn_intra_kernel, nc=nc, group=group),
        grid=(B, L // tb),
        in_specs=[tok(DN_WIDTH), tok(DN_WIDTH), tok(DN_WIDTH), tok(LANES)] + [full(a) for a in consts],
        out_specs=[pl.BlockSpec((1, 2, nc, DN_CHUNK, DN_WIDTH), lambda b, i: (b, 0, i, 0, 0)),
                   pl.BlockSpec((1, 2, nc, DN_CHUNK, DN_WIDTH), lambda b, i: (b, 0, i, 0, 0)),
                   pl.BlockSpec((1, 2, tb, DN_WIDTH), lambda b, i: (b, 0, i, 0)),
                   pl.BlockSpec((1, 2, tb, DN_WIDTH), lambda b, i: (b, 0, i, 0)),
                   pl.BlockSpec((1, 2, nc, 1, DN_WIDTH), lambda b, i: (b, 0, i, 0, 0))],
        out_shape=[jax.ShapeDtypeStruct((B, 2, nchunk, DN_CHUNK, DN_WIDTH), bf16),
                   jax.ShapeDtypeStruct((B, 2, nchunk, DN_CHUNK, DN_WIDTH), bf16),
                   jax.ShapeDtypeStruct((B, 2, L, DN_WIDTH), bf16),
                   jax.ShapeDtypeStruct((B, 2, L, DN_WIDTH), bf16),
                   jax.ShapeDtypeStruct((B, 2, nchunk, 1, DN_WIDTH), f32)],
        scratch_shapes=[pltpu.VMEM((tb, 4 * DN_WIDTH), f32)],
        compiler_params=pltpu.CompilerParams(dimension_semantics=("parallel", "parallel"),
                                             vmem_limit_bytes=VMEM_LIMIT),
        name="dn_intra",
    )(dq, dk, dv, gb, *consts)


def _dn_scan_kernel(mf_ref, nf_ref, qf_ref, of_ref, gf_ref, mb_ref, nb_ref, qb_ref, ob_ref, gbk_ref,
                    yf_ref, yb_ref, s_ref, *, nc, nbat):
    C = DN_CHUNK

    @pl.when(pl.program_id(1) == 0)
    def _():
        s_ref[...] = jnp.zeros_like(s_ref)

    lo = lax.broadcasted_iota(jnp.int32, (C, LANES), 1) < HEAD_DIM
    dirs = ((mf_ref, nf_ref, qf_ref, of_ref, gf_ref, yf_ref), (mb_ref, nb_ref, qb_ref, ob_ref, gbk_ref, yb_ref))

    def step(j, carry):
        cis = (j, nc - 1 - j)
        sls = [pl.ds(pl.multiple_of(ci * C, C), C) for ci in cis]
        seqs = [(b, d) for b in range(nbat) for d in range(2)]
        inst = [(b, d, h) for b, d in seqs for h in range(2)]
        hs = lambda a, h: a[:, h * LANES:(h + 1) * LANES]
        m = {(b, d): dirs[d][0][b, 0, cis[d]] for b, d in seqs}
        q = {(b, d): dirs[d][2][b, 0, sls[d], :] for b, d in seqs}
        s = [s_ref[b, d, h] for b, d, h in inst]
        lhs = [jnp.concatenate([_bd2(hs(m[b, d], h), lo), hs(q[b, d], h)], axis=0) for b, d, h in inst]
        rr = [_dot(a, b_.astype(bf16)) for a, b_ in zip(lhs, s)]
        for idx, (b, d, h) in enumerate(inst):
            gt = dirs[d][4][b, 0, cis[d]]
            nn = dirs[d][1][b, 0, cis[d]]
            s_ref[b, d, h] = s[idx] * hs(gt, h) + rr[idx][0:2 * C] + _bd2(hs(nn, h), lo).astype(f32)
        for k, (b, d) in enumerate(seqs):
            y = jnp.concatenate([rr[2 * k][2 * C:3 * C], rr[2 * k + 1][2 * C:3 * C]], axis=1)
            dirs[d][5][b, sls[d], :] = (y + dirs[d][3][b, 0, sls[d], :].astype(f32)).astype(bf16)
        return carry

    lax.fori_loop(0, nc, step, 0)


def _dn_scan(m, n, qp, op, gt, tb, nbat):
    B, _, nchunk, C, W = m.shape
    L = nchunk * C
    nc = tb // C
    ns = L // tb
    assert B % nbat == 0
    fwd5 = lambda last: pl.BlockSpec((nbat, 1, nc, last, W), lambda b, i: (b, 0, i, 0, 0))
    bwd5 = lambda last: pl.BlockSpec((nbat, 1, nc, last, W), lambda b, i: (b, 1, ns - 1 - i, 0, 0))
    fwd4 = pl.BlockSpec((nbat, 1, tb, W), lambda b, i: (b, 0, i, 0))
    bwd4 = pl.BlockSpec((nbat, 1, tb, W), lambda b, i: (b, 1, ns - 1 - i, 0))
    return pl.pallas_call(
        functools.partial(_dn_scan_kernel, nc=nc, nbat=nbat),
        grid=(B // nbat, ns),
        in_specs=[fwd5(C), fwd5(C), fwd4, fwd4, fwd5(1), bwd5(C), bwd5(C), bwd4, bwd4, bwd5(1)],
        out_specs=[pl.BlockSpec((nbat, tb, W), lambda b, i: (b, i, 0)),
                   pl.BlockSpec((nbat, tb, W), lambda b, i: (b, ns - 1 - i, 0))],
        out_shape=[jax.ShapeDtypeStruct((B, L, W), bf16), jax.ShapeDtypeStruct((B, L, W), bf16)],
        scratch_shapes=[pltpu.VMEM((nbat, 2, 2, 2 * C, LANES), f32)],
        compiler_params=pltpu.CompilerParams(dimension_semantics=("parallel", "arbitrary"),
                                             vmem_limit_bytes=VMEM_LIMIT),
        name="dn_scan",
    )(m, n, qp, op, gt, m, n, qp, op, gt)


def _outproj_kernel(xa_ref, ya_ref, ym_ref, yf_ref, yb_ref, dz_ref, lg_ref, lb_ref,
                    ng_ref, wa_ref, wd_ref, wm_ref, ones_ref, o_ref):
    o = yf_ref[0].astype(f32) + yb_ref[0].astype(f32)
    ms = _dot((o * o).astype(bf16), ones_ref[...]) * (1.0 / HEAD_DIM)
    yd = o * lax.rsqrt(ms + RMS_EPS) * ng_ref[...] * dz_ref[0].astype(f32)
    y = _dot(ya_ref[0], wa_ref[...]) + _dot(yd.astype(bf16), wd_ref[...]) + _dot(ym_ref[0], wm_ref[...])
    o_ref[0] = _layer_norm(xa_ref[0] + y, lg_ref[...], lb_ref[...])


def _outproj(xa, ya, ym, yf, yb, dz, p, tm):
    B, L, _ = xa.shape
    tok = lambda w: pl.BlockSpec((1, tm, w), lambda b, i: (b, i, 0))
    full = lambda a: pl.BlockSpec(a.shape, lambda b, i: (0,) * a.ndim)
    consts = [p['ln_g'], p['ln_b'], p['norm_g'], p['wo_attn'], p['wo_dn'], p['wo_mem'], p['ones_bd']]
    return pl.pallas_call(
        _outproj_kernel,
        grid=(B, L // tm),
        in_specs=[tok(D_MODEL), tok(ATTN_WIDTH), tok(MEM_WIDTH), tok(DN_WIDTH), tok(DN_WIDTH), tok(DN_WIDTH)]
                 + [full(a) for a in consts],
        out_specs=tok(D_MODEL),
        out_shape=jax.ShapeDtypeStruct((B, L, D_MODEL), f32),
        compiler_params=pltpu.CompilerParams(dimension_semantics=("parallel", "parallel"),
                                             vmem_limit_bytes=VMEM_LIMIT),
        name="outproj",
    )(xa, ya, ym, yf, yb, dz, *consts)


def _t5_bucket(rel):
    nb = REL_BUCKETS // 2
    max_exact = nb // 2
    n = jnp.abs(rel)
    large = max_exact + (jnp.log(jnp.maximum(n, 1).astype(f32) / max_exact)
                         / math.log(REL_MAX_DIST / max_exact) * (nb - max_exact)).astype(jnp.int32)
    large = jnp.minimum(large, nb - 1)
    return jnp.where(rel > 0, nb, 0) + jnp.where(n < max_exact, n, large)


def _prepare(ln_in_g, ln_in_b, rel_bias, w_in, attn_sink, dn_conv, dn_A_log, dn_dt_bias, dn_norm_g,
             w_mem_kv, w_out, ln_g, ln_b):
    offs = np.cumsum([0, ATTN_WIDTH, ATTN_KV_WIDTH, ATTN_KV_WIDTH, ATTN_WIDTH, DN_WIDTH, DN_WIDTH, DN_WIDTH,
                      DN_WIDTH, 2 * DN_HEADS, 2 * DN_HEADS, MEM_WIDTH, MEM_WIDTH])
    (w_aq, w_ak, w_av, w_az, w_dq, w_dk, w_dv, w_dz, w_da, w_db, w_mq, w_mz) = [
        w_in[:, int(offs[j]):int(offs[j + 1])] for j in range(12)]
    order = [kv * ATTN_GROUP + c for c in range(ATTN_GROUP) for kv in range(ATTN_KV_HEADS)]
    head_cols = lambda w: jnp.concatenate([w[:, h * HEAD_DIM:(h + 1) * HEAD_DIM] for h in order], axis=1)
    scale = HEAD_DIM ** -0.5 * LOG2E
    p = {}
    p['w_attn'] = jnp.concatenate([head_cols(w_aq) * scale, w_ak, w_av, head_cols(w_az)], axis=1).astype(bf16)
    p['w_mem'] = jnp.concatenate([w_mq * scale, w_mz], axis=1).astype(bf16)
    p['w_dn'] = jnp.concatenate([w_dq, w_dk, w_dv, w_dz], axis=1).astype(bf16)
    p['w_ab'] = jnp.pad(jnp.concatenate([w_da, w_db], axis=1), ((0, 0), (0, LANES - 4 * DN_HEADS))).astype(bf16)
    p['conv'] = jnp.pad(dn_conv, ((0, SUBLANES - DN_CONV), (0, 0)))
    p['alog'] = jnp.pad(dn_A_log.reshape(1, 2 * DN_HEADS), ((0, 0), (0, LANES - 2 * DN_HEADS)))
    p['dtb'] = jnp.pad(dn_dt_bias.reshape(1, 2 * DN_HEADS), ((0, 0), (0, LANES - 2 * DN_HEADS)))
    p['ln_in_g'] = ln_in_g.reshape(1, D_MODEL)
    p['ln_in_b'] = ln_in_b.reshape(1, D_MODEL)
    p['ln_g'] = ln_g.reshape(1, D_MODEL)
    p['ln_b'] = ln_b.reshape(1, D_MODEL)
    p['norm_g'] = jnp.tile(dn_norm_g.reshape(1, HEAD_DIM), (1, DN_HEADS))
    p['w_memk'] = w_mem_kv[:, 0:MEM_WIDTH].astype(bf16)
    p['w_memvt'] = w_mem_kv[:, MEM_WIDTH:2 * MEM_WIDTH].T.astype(bf16)
    p['wo_attn'] = jnp.concatenate([w_out[h * HEAD_DIM:(h + 1) * HEAD_DIM] for h in order], axis=0).astype(bf16)
    p['wo_dn'] = w_out[ATTN_WIDTH:ATTN_WIDTH + DN_WIDTH].astype(bf16)
    p['wo_mem'] = w_out[ATTN_WIDTH + DN_WIDTH:].astype(bf16)
    p['sink'] = jnp.stack([attn_sink[h] for h in order]) * LOG2E

    blk = np.arange(DN_WIDTH) // HEAD_DIM
    bd = (blk[:, None] == blk[None, :])
    p['ones_bd'] = jnp.asarray(bd, bf16)
    p['tri'] = jnp.asarray(np.tril(np.ones((DN_CHUNK, DN_CHUNK))), bf16)
    esel = np.zeros((LANES, 4 * DN_WIDTH), np.float32)
    for c in range(4 * DN_HEADS):
        esel[c, c * HEAD_DIM:(c + 1) * HEAD_DIM] = 1.0
    p['esel'] = jnp.asarray(esel, bf16)

    t = jnp.arange(BLOCK)[None, :]
    s = jnp.arange(3 * BLOCK)[:, None]
    rel = s - BLOCK - t
    onehot = jax.nn.one_hot(_t5_bucket(rel), REL_BUCKETS, dtype=f32)
    rb = jnp.stack([rel_bias[:, h] for h in order], axis=0)
    bias = jnp.einsum('stk,hk->sht', onehot, rb, precision=lax.Precision.HIGHEST)
    bias = jnp.where((jnp.abs(rel) <= WINDOW)[:, None, :], bias * LOG2E, NEG)
    bias = bias.reshape(3 * BLOCK, ATTN_HEADS * BLOCK)
    no_prev = jnp.where(s < BLOCK, NEG, 0.0)
    no_next = jnp.where(s >= 2 * BLOCK, NEG, 0.0)
    p['bias'] = jnp.stack([bias, bias + no_next, bias + no_prev])
    return p


SCAN_SEQS = 4


def _trunk(x, mem, p, tm_in=1024, tm=1024, tb_intra=512, tb_scan=1024):
    B, L, _ = x.shape
    tm_in = min(tm_in, L)
    tm = min(tm, L)
    tb_intra = min(tb_intra, L)
    tb_scan = min(tb_scan, L)
    nbat = math.gcd(B, SCAN_SEQS)
    aq, ak, av, az, mq, mz, dq, dk, dv, dz, gb, xa = _inproj(x, p, tm_in)
    mk, mvt = _memkv(mem, p)
    ya, ym = _attn(aq, ak, av, az, mq, mz, mk, mvt, p)
    m, n, qp, op, gt = _dn_intra(dq, dk, dv, gb, p, tb_intra)
    yf, yb = _dn_scan(m, n, qp, op, gt, tb_scan, nbat)
    return _outproj(xa, ya, ym, yf, yb, dz, p, tm)


def kernel(x_prompt, x_sample, mem_prompt, mem_sample, ln_in_g, ln_in_b, rel_bias, w_in, attn_sink, dn_conv,
           dn_A_log, dn_dt_bias, dn_norm_g, w_mem_kv, w_out, ln_g, ln_b):
    p = _prepare(ln_in_g, ln_in_b, rel_bias, w_in[0], attn_sink[0], dn_conv[0], dn_A_log[0], dn_dt_bias[0],
                 dn_norm_g[0], w_mem_kv[0], w_out[0], ln_g[0], ln_b[0])
    return (_trunk(x_prompt, mem_prompt, p), _trunk(x_sample, mem_sample, p))
```

```python
import functools
import math

import jax
import jax.numpy as jnp
import numpy as np
from jax import lax
from jax.experimental import pallas as pl
from jax.experimental.pallas import tpu as pltpu

f32 = jnp.float32
bf16 = jnp.bfloat16

D_MODEL = 1024
HEAD_DIM = 64
ATTN_HEADS = 8
ATTN_KV_HEADS = 2
ATTN_GROUP = ATTN_HEADS // ATTN_KV_HEADS
ATTN_WIDTH = ATTN_HEADS * HEAD_DIM
ATTN_KV_WIDTH = ATTN_KV_HEADS * HEAD_DIM
WINDOW = 128
BLOCK = 128
REL_BUCKETS = 32
REL_MAX_DIST = 128
DN_HEADS = 4
DN_WIDTH = DN_HEADS * HEAD_DIM
DN_CONV = 5
DN_CONV_DIM = 3 * DN_WIDTH
DN_CHUNK = 64
MEM_HEADS = 4
MEM_WIDTH = MEM_HEADS * HEAD_DIM
DEPTH = 1
DEEPNORM_ALPHA = (2 * DEPTH) ** 0.25
LN_EPS = 1e-5
RMS_EPS = 1e-6
NEG = -1e30
LOG2E = 1.4426950408889634

LANES = 128
SUBLANES = 8
HALO = SUBLANES
VMEM_LIMIT = 56 * 1024 * 1024

NT = (((1,), (1,)), ((), ()))
TN = (((0,), (0,)), ((), ()))


def _dot(a, b):
    return jnp.dot(a, b, preferred_element_type=f32)


def _dot_nt(a, b):
    return lax.dot_general(a, b, NT, preferred_element_type=f32)


def _split3(x):
    hi = x.astype(bf16)
    r1 = x - hi.astype(f32)
    mid = r1.astype(bf16)
    lo = (r1 - mid.astype(f32)).astype(bf16)
    return hi, mid, lo


def _dot_exact_lhs(x, sel):
    hi, mid, lo = _split3(x)
    return _dot(hi, sel) + _dot(mid, sel) + _dot(lo, sel)


def _dot_exact_rhs(sel, x):
    hi, mid, lo = _split3(x)
    return _dot(sel, hi) + _dot(sel, mid) + _dot(sel, lo)


def _layer_norm(x, g, b):
    mu = jnp.mean(x, axis=-1, keepdims=True)
    xc = x - mu
    var = jnp.mean(xc * xc, axis=-1, keepdims=True)
    return xc * lax.rsqrt(var + LN_EPS) * g + b


def _silu(x):
    return x / (1.0 + jnp.exp(-x))


def _inproj_kernel(x_ref, xp_ref, xn_ref, lng_ref, lnb_ref, wattn_ref, wmem_ref, wdn_ref, wab_ref,
                   conv_ref, alog_ref, dtb_ref, ones_ref,
                   aq_ref, ak_ref, av_ref, az_ref, mq_ref, mz_ref, dq_ref, dk_ref, dv_ref, dz_ref, gb_ref, xa_ref,
                   hbuf, *, tm, nsub):
    i = pl.program_id(1)
    n = pl.num_programs(1)
    g = lng_ref[...]
    b = lnb_ref[...]
    ts = tm // nsub
    rows = ts + 2 * HALO
    half = DN_CONV // 2
    ones = ones_ref[...]

    def l2n(t, scale):
        ss = _dot((t * t).astype(bf16), ones)
        return t * (lax.rsqrt(ss + 1e-6) * scale)

    def sub_tile(k):
        r0 = k * ts
        rs = slice(r0, r0 + ts)
        st = {}

        def ln():
            xln = _layer_norm(x_ref[0, rs, :], g, b)
            xa_ref[0, rs, :] = DEEPNORM_ALPHA * xln
            st['xb'] = xln.astype(bf16)
            prev = xp_ref[0] if k == 0 else x_ref[0, r0 - HALO:r0, :]
            nxt = xn_ref[0] if k == nsub - 1 else x_ref[0, r0 + ts:r0 + ts + HALO, :]
            st['xsb'] = jnp.concatenate([_layer_norm(prev, g, b), xln, _layer_norm(nxt, g, b)], axis=0).astype(bf16)

        def proj_qkv():
            hbuf[k] = _dot(st['xsb'], wdn_ref[:, 0:DN_CONV_DIM])
            if k == 0:
                hbuf[k, 0:HALO, :] = jnp.where(i > 0, hbuf[k, 0:HALO, :], 0.0)
            if k == nsub - 1:
                hbuf[k, HALO + ts:rows, :] = jnp.where(i < n - 1, hbuf[k, HALO + ts:rows, :], 0.0)
            st['ab'] = _dot(st['xb'], wab_ref[...])

        def proj_aq():
            aq_ref[0, rs, :] = _dot(st['xb'], wattn_ref[:, 0:512]).astype(bf16)

        def conv():
            hfull = hbuf[k]
            c = conv_ref[half:half + 1, :] * hfull[HALO:HALO + ts, :]
            for j in range(DN_CONV):
                if j != half:
                    c = c + conv_ref[j:j + 1, :] * pltpu.roll(hfull, (half - j) % rows, axis=0)[HALO:HALO + ts, :]
            st['c'] = c

        def proj_kv_az():
            kv = _dot(st['xb'], wattn_ref[:, 512:768]).astype(bf16)
            ak_ref[0, rs, :] = kv[:, 0:ATTN_KV_WIDTH]
            av_ref[0, rs, :] = kv[:, ATTN_KV_WIDTH:2 * ATTN_KV_WIDTH]
            az_ref[0, rs, :] = _silu(_dot(st['xb'], wattn_ref[:, 768:1280])).astype(bf16)

        def act_q():
            st['c'] = _silu(st['c'])
            dq_ref[0, rs, :] = l2n(st['c'][:, 0:DN_WIDTH], HEAD_DIM ** -0.5).astype(bf16)

        def proj_dz():
            dz_ref[0, rs, :] = _silu(_dot(st['xb'], wdn_ref[:, DN_CONV_DIM:DN_CONV_DIM + DN_WIDTH])).astype(bf16)

        def act_k():
            dk_ref[0, rs, :] = l2n(st['c'][:, DN_WIDTH:2 * DN_WIDTH], 1.0).astype(bf16)

        def proj_mz():
            mz_ref[0, rs, :] = _silu(_dot(st['xb'], wmem_ref[:, 256:512])).astype(bf16)

        def act_v_gb():
            dv_ref[0, rs, :] = st['c'][:, 2 * DN_WIDTH:3 * DN_WIDTH].astype(bf16)
            ab = st['ab']
            z = ab + dtb_ref[...]
            sp = jnp.maximum(z, 0.0) + jnp.log1p(jnp.exp(-jnp.abs(z)))
            gdec = -jnp.exp(alog_ref[...]) * sp
            beta = 1.0 / (1.0 + jnp.exp(-ab))
            lane = lax.broadcasted_iota(jnp.int32, ab.shape, 1)
            gb_ref[0, rs, :] = jnp.where(lane < 2 * DN_HEADS, gdec, beta)

        def proj_mq():
            mq_ref[0, rs, :] = _dot(st['xb'], wmem_ref[:, 0:256]).astype(bf16)

        return [ln, proj_qkv, proj_aq, conv, proj_kv_az, act_q, proj_dz, act_k, proj_mz, act_v_gb, proj_mq]

    subs = [sub_tile(k) for k in range(nsub)]
    nst = len(subs[0])
    for s in range(nst + SUB_LAG * (nsub - 1)):
        for k in range(nsub):
            if 0 <= s - SUB_LAG * k < nst:
                subs[k][s - SUB_LAG * k]()


SUB_TILE = 512
SUB_LAG = 2


def _inproj(x, p, tm):
    B, L, _ = x.shape
    nsub = max(tm // SUB_TILE, 1)
    nt = L // tm
    r8 = tm // HALO
    full = lambda a: pl.BlockSpec(a.shape, lambda b, i: (0,) * a.ndim)
    consts = [p['ln_in_g'], p['ln_in_b'], p['w_attn'], p['w_mem'], p['w_dn'], p['w_ab'],
              p['conv'], p['alog'], p['dtb'], p['ones_bd']]
    tok = lambda w: pl.BlockSpec((1, tm, w), lambda b, i: (b, i, 0))
    widths = [512, 128, 128, 512, 256, 256, 256, 256, 256, 256]
    out_shape = ([jax.ShapeDtypeStruct((B, L, w), bf16) for w in widths]
                 + [jax.ShapeDtypeStruct((B, L, LANES), f32), jax.ShapeDtypeStruct((B, L, D_MODEL), f32)])
    return pl.pallas_call(
        functools.partial(_inproj_kernel, tm=tm, nsub=nsub),
        grid=(B, nt),
        in_specs=[pl.BlockSpec((1, tm, D_MODEL), lambda b, i: (b, i, 0)),
                  pl.BlockSpec((1, HALO, D_MODEL), lambda b, i: (b, jnp.maximum(i * r8 - 1, 0), 0)),
                  pl.BlockSpec((1, HALO, D_MODEL), lambda b, i: (b, jnp.minimum((i + 1) * r8, L // HALO - 1), 0)),
                  ] + [full(a) for a in consts],
        out_specs=[tok(w) for w in widths] + [tok(LANES), tok(D_MODEL)],
        out_shape=out_shape,
        scratch_shapes=[pltpu.VMEM((nsub, tm // nsub + 2 * HALO, DN_CONV_DIM), f32)],
        compiler_params=pltpu.CompilerParams(dimension_semantics=("parallel", "parallel"),
                                             vmem_limit_bytes=VMEM_LIMIT),
        name="inproj",
    )(x, x, x, *consts)


def _memkv_kernel(m_ref, lng_ref, lnb_ref, wk_ref, wvt_ref, k_ref, vt_ref):
    mb = _layer_norm(m_ref[0], lng_ref[...], lnb_ref[...]).astype(bf16)
    k_ref[0] = _dot(mb, wk_ref[...]).astype(bf16)
    vt_ref[0] = _dot_nt(wvt_ref[...], mb).astype(bf16)


def _memkv(mem, p):
    B, M, _ = mem.shape
    return pl.pallas_call(
        _memkv_kernel,
        grid=(B,),
        in_specs=[pl.BlockSpec((1, M, D_MODEL), lambda b: (b, 0, 0)),
                  pl.BlockSpec((1, D_MODEL), lambda b: (0, 0)),
                  pl.BlockSpec((1, D_MODEL), lambda b: (0, 0)),
                  pl.BlockSpec((D_MODEL, MEM_WIDTH), lambda b: (0, 0)),
                  pl.BlockSpec((MEM_WIDTH, D_MODEL), lambda b: (0, 0))],
        out_specs=[pl.BlockSpec((1, M, MEM_WIDTH), lambda b: (b, 0, 0)),
                   pl.BlockSpec((1, MEM_WIDTH, M), lambda b: (b, 0, 0))],
        out_shape=[jax.ShapeDtypeStruct((B, M, MEM_WIDTH), bf16), jax.ShapeDtypeStruct((B, MEM_WIDTH, M), bf16)],
        compiler_params=pltpu.CompilerParams(dimension_semantics=("parallel",), vmem_limit_bytes=VMEM_LIMIT),
        name="memkv",
    )(mem, p['ln_in_g'], p['ln_in_b'], p['w_memk'], p['w_memvt'])


def _softmax_cols(s, extra=None):
    m = jnp.max(s, axis=0, keepdims=True)
    if extra is not None:
        m = jnp.maximum(m, extra)
    e = jnp.exp2(s - m)
    return e.astype(bf16), (None if extra is None else jnp.exp2(extra - m))


QB = 4


def _attn_kernel(sink_ref, aq_ref, ak_ref, av_ref, az_ref, mq_ref, mz_ref, mk_ref, mvt_ref,
                 bias_first_ref, bias_mid_ref, bias_last_ref, ya_ref, ym_ref):
    j = pl.program_id(1)
    nb = QB * pl.num_programs(1)
    bias_refs = [bias_first_ref] + [bias_mid_ref] * (QB - 2) + [bias_last_ref]
    lane = lax.broadcasted_iota(jnp.int32, (BLOCK, LANES), 1)
    low = lane < HEAD_DIM
    zero = jnp.zeros((BLOCK, LANES), bf16)

    def rows(ref, blk):
        return ref[0, pl.ds(pl.multiple_of(blk * BLOCK, BLOCK), BLOCK), :]

    def band(ref, i):
        return jnp.concatenate([rows(ref, jnp.maximum(i - 1, 0)), rows(ref, i), rows(ref, jnp.minimum(i + 1, nb - 1))],
                               axis=0)

    def stack_q(q):
        parts = []
        for c in range(ATTN_GROUP):
            t = q[:, c * LANES:(c + 1) * LANES]
            parts.append(jnp.where(low, t, zero))
            parts.append(jnp.where(low, zero, t))
        return jnp.concatenate(parts, axis=0)

    blks = [j * QB + t for t in range(QB)]
    kcat = [band(ak_ref, i) for i in blks]
    vcat = [band(av_ref, i) for i in blks]
    qs = [stack_q(aq_ref[0, t * BLOCK:(t + 1) * BLOCK, :]) for t in range(QB)]
    mq = mq_ref[0]
    nq = QB * BLOCK
    lane_m = lax.broadcasted_iota(jnp.int32, (nq, MEM_WIDTH), 1) // HEAD_DIM
    zm = jnp.zeros((nq, MEM_WIDTH), bf16)
    mqs = jnp.concatenate([jnp.where(lane_m == h, mq, zm) for h in range(MEM_HEADS)], axis=0)

    st = [_dot_nt(k, q) + b[0] for k, q, b in zip(kcat, qs, bias_refs)]
    smt = _dot_nt(mk_ref[0], mqs)
    ones_v = jnp.ones((2 * SUBLANES, 3 * BLOCK), bf16)
    vt = [jnp.concatenate([v.astype(f32).T.astype(bf16), ones_v], axis=0) for v in vcat]
    mvt = jnp.concatenate([mvt_ref[0], jnp.ones((2 * SUBLANES, mk_ref.shape[1]), bf16)], axis=0)

    soft = [[_softmax_cols(s[:, h * BLOCK:(h + 1) * BLOCK], sink_ref[h]) for h in range(ATTN_HEADS)] for s in st]
    softm = [_softmax_cols(smt[:, h * nq:(h + 1) * nq]) for h in range(MEM_HEADS)]

    ot = [_dot(v, jnp.concatenate([p_ for p_, _ in sf], axis=1)) for v, sf in zip(vt, soft)]
    omt = _dot(mvt, jnp.concatenate([p_ for p_, _ in softm], axis=1))

    half = HEAD_DIM
    vrows = ATTN_KV_WIDTH
    for t in range(QB):
        az = az_ref[0, t * BLOCK:(t + 1) * BLOCK, :].astype(f32)
        rinv = [1.0 / (ot[t][vrows:vrows + 1, h * BLOCK:(h + 1) * BLOCK] + soft[t][h][1]) for h in range(ATTN_HEADS)]
        for c in range(ATTN_GROUP):
            o0 = ot[t][0:half, (2 * c) * BLOCK:(2 * c + 1) * BLOCK] * rinv[2 * c]
            o1 = ot[t][half:2 * half, (2 * c + 1) * BLOCK:(2 * c + 2) * BLOCK] * rinv[2 * c + 1]
            y = jnp.concatenate([o0, o1], axis=0).T * az[:, c * LANES:(c + 1) * LANES]
            ya_ref[0, t * BLOCK:(t + 1) * BLOCK, c * LANES:(c + 1) * LANES] = y.astype(bf16)
    ymt = jnp.concatenate([omt[h * HEAD_DIM:(h + 1) * HEAD_DIM, h * nq:(h + 1) * nq]
                           * (1.0 / omt[MEM_WIDTH:MEM_WIDTH + 1, h * nq:(h + 1) * nq])
                           for h in range(MEM_HEADS)], axis=0)
    ym_ref[0] = (ymt.T * mz_ref[0].astype(f32)).astype(bf16)


def _attn(aq, ak, av, az, mq, mz, mk, mvt, p):
    B, L, _ = aq.shape
    assert L % (QB * BLOCK) == 0 and QB >= 2
    ns = L // (QB * BLOCK)
    M = mk.shape[1]
    tok = lambda w: pl.BlockSpec((1, QB * BLOCK, w), lambda b, j: (b, j, 0))
    seq = lambda w: pl.BlockSpec((1, L, w), lambda b, j: (b, 0, 0))
    bias_shape = (1, 3 * BLOCK, ATTN_HEADS * BLOCK)
    return pl.pallas_call(
        _attn_kernel,
        grid=(B, ns),
        in_specs=[pl.BlockSpec(memory_space=pltpu.SMEM),
                  tok(ATTN_WIDTH), seq(ATTN_KV_WIDTH), seq(ATTN_KV_WIDTH), tok(ATTN_WIDTH),
                  tok(MEM_WIDTH), tok(MEM_WIDTH),
                  pl.BlockSpec((1, M, MEM_WIDTH), lambda b, j: (b, 0, 0)),
                  pl.BlockSpec((1, MEM_WIDTH, M), lambda b, j: (b, 0, 0)),
                  pl.BlockSpec(bias_shape, lambda b, j: (jnp.where(j == 0, 2, 0), 0, 0)),
                  pl.BlockSpec(bias_shape, lambda b, j: (0, 0, 0)),
                  pl.BlockSpec(bias_shape, lambda b, j: (jnp.where(j == ns - 1, 1, 0), 0, 0))],
        out_specs=[tok(ATTN_WIDTH), tok(MEM_WIDTH)],
        out_shape=[jax.ShapeDtypeStruct((B, L, ATTN_WIDTH), bf16), jax.ShapeDtypeStruct((B, L, MEM_WIDTH), bf16)],
        compiler_params=pltpu.CompilerParams(dimension_semantics=("parallel", "arbitrary"),
                                             vmem_limit_bytes=VMEM_LIMIT),
        name="attn",
    )(p['sink'], aq, ak, av, az, mq, mz, mk, mvt, p['bias'], p['bias'], p['bias'])


def _bd2(y, lo):
    z = jnp.zeros_like(y)
    return jnp.concatenate([jnp.where(lo, y, z), jnp.where(lo, z, y)], axis=0)


def _halves(x):
    return x[:, 0:LANES], x[:, LANES:2 * LANES]


def _dn_intra_kernel(q_ref, k_ref, v_ref, gb_ref, tri_ref, esel_ref,
                     m_ref, n_ref, qp_ref, op_ref, gt_ref, bc_ref, *, nc, group):
    C = DN_CHUNK
    W = DN_WIDTH
    ri = lax.broadcasted_iota(jnp.int32, (C, W), 0)
    lj = lax.broadcasted_iota(jnp.int32, (C, W), 1) % C
    eye = ri == lj
    eye_f = jnp.where(eye, 1.0, 0.0)
    masks = ((ri >= lj, ri > lj), (ri <= lj, ri < lj))
    lvl_masks = [((ri // (2 * s)) == (lj // (2 * s))) & ((ri // s) != (lj // s))
                 for s in (2 ** k for k in range(C.bit_length() - 1))]
    lo = lax.broadcasted_iota(jnp.int32, (C, LANES), 1) < HEAD_DIM
    col = lax.broadcasted_iota(jnp.int32, (C, LANES), 1)
    is_bwd = (col >= DN_HEADS) & (col < 2 * DN_HEADS)
    is_beta = col >= 2 * DN_HEADS
    tri = tri_ref[...]

    srcs = []
    for c in range(nc):
        gbv = gb_ref[0, c * C:(c + 1) * C, :]
        gc = _dot_exact_rhs(tri, gbv)
        srcs.append(jnp.where(is_beta, gbv, jnp.where(is_bwd, gc[C - 1:C, :] - gc + gbv, gc)))
    hi, mid, _ = _split3(jnp.concatenate(srcs, axis=0))
    bc_ref[:, 0:2 * W] = _dot(hi, esel_ref[:, 0:2 * W]) + _dot(mid, esel_ref[:, 0:2 * W])
    bc_ref[:, 2 * W:4 * W] = _dot(hi, esel_ref[:, 2 * W:4 * W])

    def pmm(x, y):
        return [_dot(xh, _bd2(yh, lo)) for xh, yh in zip(_halves(x), _halves(y))]

    def cat(parts):
        return jnp.concatenate(parts, axis=1)

    def body(gi, carry):
        cis = [gi * group + t for t in range(group)]
        sls = [pl.ds(pl.multiple_of(ci * C, C), C) for ci in cis]
        qb = [q_ref[0, sl, :] for sl in sls]
        kb = [k_ref[0, sl, :] for sl in sls]
        vb = [v_ref[0, sl, :] for sl in sls]
        gram = [cat([_dot_nt(xh, _bd2(kh, lo)) for xh, kh in zip(_halves(jnp.concatenate([k, q], axis=0)), _halves(k))])
                for k, q in zip(kb, qb)]
        inst = [(t, d) for t in range(group) for d in range(2)]
        gq = [bc_ref[sls[t], d * W:(d + 1) * W] for t, d in inst]
        bq = [bc_ref[sls[t], (2 + d) * W:(3 + d) * W] for t, d in inst]
        glast = [g[C - 1:C, :] if d == 0 else g[0:1, :] for g, (t, d) in zip(gq, inst)]
        r = [jnp.sum(jnp.where(eye, g, 0.0), axis=0, keepdims=True) for g in gq]
        dec = [jnp.where(masks[d][0], jnp.exp(jnp.minimum(g - rr, 0.0)), 0.0) for g, rr, (t, d) in zip(gq, r, inst)]
        nn = [jnp.where(masks[d][1], -(gram[t][0:C] * b * dc), 0.0) for b, dc, (t, d) in zip(bq, dec, inst)]
        tm = [eye_f + jnp.where(lvl_masks[0], a, 0.0) for a in nn]
        for lm in lvl_masks[1:]:
            off = [jnp.where(lm, a, 0.0).astype(bf16) for a in nn]
            left = [cat(pmm(a.astype(bf16), o_)).astype(bf16) for a, o_ in zip(tm, off)]
            tm = [a + cat(pmm(l_, a.astype(bf16))) for a, l_ in zip(tm, left)]
        tb = [a.astype(bf16) for a in tm]
        eg = [jnp.exp(g) for g in gq]
        vbeta = [(vb[t].astype(f32) * b).astype(bf16) for b, (t, d) in zip(bq, inst)]
        kbg = [(kb[t].astype(f32) * b * e).astype(bf16) for b, e, (t, d) in zip(bq, eg, inst)]
        uw = [[_dot(th, jnp.concatenate([_bd2(vh, lo), _bd2(kh, lo)], axis=1)).astype(bf16)
               for th, vh, kh in zip(_halves(a), _halves(v_), _halves(k_))]
              for a, v_, k_ in zip(tb, vbeta, kbg)]
        kd = [(kb[t].astype(f32) * jnp.exp(gl - g)).astype(bf16) for g, gl, (t, d) in zip(gq, glast, inst)]
        fm = [[lax.dot_general(kh, uwp, TN, preferred_element_type=f32) for kh, uwp in zip(_halves(k_), uw_)]
              for k_, uw_ in zip(kd, uw)]
        qkb = [(gram[t][C:2 * C] * dc).astype(bf16) for dc, (t, d) in zip(dec, inst)]
        qo = [[_dot(qh, jnp.concatenate([_bd2(uwp[:, 0:LANES], lo), _bd2(uwp[:, LANES:2 * LANES], lo)], axis=1))
               for qh, uwp in zip(_halves(a), uw_)] for a, uw_ in zip(qkb, uw)]
        for idx, (t, d) in enumerate(inst):
            ci, sl = cis[t], sls[t]
            f0, f1 = fm[idx]
            n_ref[0, d, ci] = cat([jnp.where(lo, f[0:C, 0:LANES], f[C:2 * C, 0:LANES])
                                   for f in (f0, f1)]).astype(bf16)
            m_ref[0, d, ci] = cat([-jnp.where(lo, f[0:C, LANES:2 * LANES], f[C:2 * C, LANES:2 * LANES])
                                   for f in (f0, f1)]).astype(bf16)
            qd = qb[t].astype(f32) * eg[idx]
            qp_ref[0, d, sl, :] = (qd - cat([p_[:, LANES:2 * LANES] for p_ in qo[idx]])).astype(bf16)
            op_ref[0, d, sl, :] = cat([p_[:, 0:LANES] for p_ in qo[idx]]).astype(bf16)
            gt_ref[0, d, ci] = jnp.exp(glast[idx])
        return carry

    lax.fori_loop(0, nc // group, body, 0)


def _dn_intra(dq, dk, dv, gb, p, tb, group=4):
    B, L, _ = dq.shape
    nc = tb // DN_CHUNK
    nchunk = L // DN_CHUNK
    tok = lambda w: pl.BlockSpec((1, tb, w), lambda b, i: (b, i, 0))
    full = lambda a: pl.BlockSpec(a.shape, lambda b, i: (0,) * a.ndim)
    consts = [p['tri'], p['esel']]
    return pl.pallas_call(
        functools.partial(_dn_intra_kernel, nc=nc, group=group),
        grid=(B, L // tb),
        in_specs=[tok(DN_WIDTH), tok(DN_WIDTH), tok(DN_WIDTH), tok(LANES)] + [full(a) for a in consts],
        out_specs=[pl.BlockSpec((1, 2, nc, DN_CHUNK, DN_WIDTH), lambda b, i: (b, 0, i, 0, 0)),
                   pl.BlockSpec((1, 2, nc, DN_CHUNK, DN_WIDTH), lambda b, i: (b, 0, i, 0, 0)),
                   pl.BlockSpec((1, 2, tb, DN_WIDTH), lambda b, i: (b, 0, i, 0)),
                   pl.BlockSpec((1, 2, tb, DN_WIDTH), lambda b, i: (b, 0, i, 0)),
                   pl.BlockSpec((1, 2, nc, 1, DN_WIDTH), lambda b, i: (b, 0, i, 0, 0))],
        out_shape=[jax.ShapeDtypeStruct((B, 2, nchunk, DN_CHUNK, DN_WIDTH), bf16),
                   jax.ShapeDtypeStruct((B, 2, nchunk, DN_CHUNK, DN_WIDTH), bf16),
                   jax.ShapeDtypeStruct((B, 2, L, DN_WIDTH), bf16),
                   jax.ShapeDtypeStruct((B, 2, L, DN_WIDTH), bf16),
                   jax.ShapeDtypeStruct((B, 2, nchunk, 1, DN_WIDTH), f32)],
        scratch_shapes=[pltpu.VMEM((tb, 4 * DN_WIDTH), f32)],
        compiler_params=pltpu.CompilerParams(dimension_semantics=("parallel", "parallel"),
                                             vmem_limit_bytes=VMEM_LIMIT),
        name="dn_intra",
    )(dq, dk, dv, gb, *consts)


def _dn_scan_kernel(mf_ref, nf_ref, qf_ref, of_ref, gf_ref, mb_ref, nb_ref, qb_ref, ob_ref, gbk_ref,
                    yf_ref, yb_ref, s_ref, *, nc, nbat):
    C = DN_CHUNK

    @pl.when(pl.program_id(1) == 0)
    def _():
        s_ref[...] = jnp.zeros_like(s_ref)

    lo = lax.broadcasted_iota(jnp.int32, (C, LANES), 1) < HEAD_DIM
    dirs = ((mf_ref, nf_ref, qf_ref, of_ref, gf_ref, yf_ref), (mb_ref, nb_ref, qb_ref, ob_ref, gbk_ref, yb_ref))

    def step(j, carry):
        cis = (j, nc - 1 - j)
        sls = [pl.ds(pl.multiple_of(ci * C, C), C) for ci in cis]
        seqs = [(b, d) for b in range(nbat) for d in range(2)]
        inst = [(b, d, h) for b, d in seqs for h in range(2)]
        hs = lambda a, h: a[:, h * LANES:(h + 1) * LANES]
        m = {(b, d): dirs[d][0][b, 0, cis[d]] for b, d in seqs}
        q = {(b, d): dirs[d][2][b, 0, sls[d], :] for b, d in seqs}
        s = [s_ref[b, d, h] for b, d, h in inst]
        lhs = [jnp.concatenate([_bd2(hs(m[b, d], h), lo), hs(q[b, d], h)], axis=0) for b, d, h in inst]
        rr = [_dot(a, b_.astype(bf16)) for a, b_ in zip(lhs, s)]
        for idx, (b, d, h) in enumerate(inst):
            gt = dirs[d][4][b, 0, cis[d]]
            nn = dirs[d][1][b, 0, cis[d]]
            s_ref[b, d, h] = s[idx] * hs(gt, h) + rr[idx][0:2 * C] + _bd2(hs(nn, h), lo).astype(f32)
        for k, (b, d) in enumerate(seqs):
            y = jnp.concatenate([rr[2 * k][2 * C:3 * C], rr[2 * k + 1][2 * C:3 * C]], axis=1)
            dirs[d][5][b, sls[d], :] = (y + dirs[d][3][b, 0, sls[d], :].astype(f32)).astype(bf16)
        return carry

    lax.fori_loop(0, nc, step, 0)


def _dn_scan(m, n, qp, op, gt, tb, nbat):
    B, _, nchunk, C, W = m.shape
    L = nchunk * C
    nc = tb // C
    ns = L // tb
    assert B % nbat == 0
    fwd5 = lambda last: pl.BlockSpec((nbat, 1, nc, last, W), lambda b, i: (b, 0, i, 0, 0))
    bwd5 = lambda last: pl.BlockSpec((nbat, 1, nc, last, W), lambda b, i: (b, 1, ns - 1 - i, 0, 0))
    fwd4 = pl.BlockSpec((nbat, 1, tb, W), lambda b, i: (b, 0, i, 0))
    bwd4 = pl.BlockSpec((nbat, 1, tb, W), lambda b, i: (b, 1, ns - 1 - i, 0))
    return pl.pallas_call(
        functools.partial(_dn_scan_kernel, nc=nc, nbat=nbat),
        grid=(B // nbat, ns),
        in_specs=[fwd5(C), fwd5(C), fwd4, fwd4, fwd5(1), bwd5(C), bwd5(C), bwd4, bwd4, bwd5(1)],
        out_specs=[pl.BlockSpec((nbat, tb, W), lambda b, i: (b, i, 0)),
                   pl.BlockSpec((nbat, tb, W), lambda b, i: (b, ns - 1 - i, 0))],
        out_shape=[jax.ShapeDtypeStruct((B, L, W), bf16), jax.ShapeDtypeStruct((B, L, W), bf16)],
        scratch_shapes=[pltpu.VMEM((nbat, 2, 2, 2 * C, LANES), f32)],
        compiler_params=pltpu.CompilerParams(dimension_semantics=("parallel", "arbitrary"),
                                             vmem_limit_bytes=VMEM_LIMIT),
        name="dn_scan",
    )(m, n, qp, op, gt, m, n, qp, op, gt)


def _outproj_kernel(xa_ref, ya_ref, ym_ref, yf_ref, yb_ref, dz_ref, lg_ref, lb_ref,
                    ng_ref, wa_ref, wd_ref, wm_ref, ones_ref, o_ref):
    o = yf_ref[0].astype(f32) + yb_ref[0].astype(f32)
    ms = _dot((o * o).astype(bf16), ones_ref[...]) * (1.0 / HEAD_DIM)
    yd = o * lax.rsqrt(ms + RMS_EPS) * ng_ref[...] * dz_ref[0].astype(f32)
    y = _dot(ya_ref[0], wa_ref[...]) + _dot(yd.astype(bf16), wd_ref[...]) + _dot(ym_ref[0], wm_ref[...])
    o_ref[0] = _layer_norm(xa_ref[0] + y, lg_ref[...], lb_ref[...])


def _outproj(xa, ya, ym, yf, yb, dz, p, tm):
    B, L, _ = xa.shape
    tok = lambda w: pl.BlockSpec((1, tm, w), lambda b, i: (b, i, 0))
    full = lambda a: pl.BlockSpec(a.shape, lambda b, i: (0,) * a.ndim)
    consts = [p['ln_g'], p['ln_b'], p['norm_g'], p['wo_attn'], p['wo_dn'], p['wo_mem'], p['ones_bd']]
    return pl.pallas_call(
        _outproj_kernel,
        grid=(B, L // tm),
        in_specs=[tok(D_MODEL), tok(ATTN_WIDTH), tok(MEM_WIDTH), tok(DN_WIDTH), tok(DN_WIDTH), tok(DN_WIDTH)]
                 + [full(a) for a in consts],
        out_specs=tok(D_MODEL),
        out_shape=jax.ShapeDtypeStruct((B, L, D_MODEL), f32),
        compiler_params=pltpu.CompilerParams(dimension_semantics=("parallel", "parallel"),
                                             vmem_limit_bytes=VMEM_LIMIT),
        name="outproj",
    )(xa, ya, ym, yf, yb, dz, *consts)


def _t5_bucket(rel):
    nb = REL_BUCKETS // 2
    max_exact = nb // 2
    n = jnp.abs(rel)
    large = max_exact + (jnp.log(jnp.maximum(n, 1).astype(f32) / max_exact)
                         / math.log(REL_MAX_DIST / max_exact) * (nb - max_exact)).astype(jnp.int32)
    large = jnp.minimum(large, nb - 1)
    return jnp.where(rel > 0, nb, 0) + jnp.where(n < max_exact, n, large)


def _prepare(ln_in_g, ln_in_b, rel_bias, w_in, attn_sink, dn_conv, dn_A_log, dn_dt_bias, dn_norm_g,
             w_mem_kv, w_out, ln_g, ln_b):
    offs = np.cumsum([0, ATTN_WIDTH, ATTN_KV_WIDTH, ATTN_KV_WIDTH, ATTN_WIDTH, DN_WIDTH, DN_WIDTH, DN_WIDTH,
                      DN_WIDTH, 2 * DN_HEADS, 2 * DN_HEADS, MEM_WIDTH, MEM_WIDTH])
    (w_aq, w_ak, w_av, w_az, w_dq, w_dk, w_dv, w_dz, w_da, w_db, w_mq, w_mz) = [
        w_in[:, int(offs[j]):int(offs[j + 1])] for j in range(12)]
    order = [kv * ATTN_GROUP + c for c in range(ATTN_GROUP) for kv in range(ATTN_KV_HEADS)]
    head_cols = lambda w: jnp.concatenate([w[:, h * HEAD_DIM:(h + 1) * HEAD_DIM] for h in order], axis=1)
    scale = HEAD_DIM ** -0.5 * LOG2E
    p = {}
    p['w_attn'] = jnp.concatenate([head_cols(w_aq) * scale, w_ak, w_av, head_cols(w_az)], axis=1).astype(bf16)
    p['w_mem'] = jnp.concatenate([w_mq * scale, w_mz], axis=1).astype(bf16)
    p['w_dn'] = jnp.concatenate([w_dq, w_dk, w_dv, w_dz], axis=1).astype(bf16)
    p['w_ab'] = jnp.pad(jnp.concatenate([w_da, w_db], axis=1), ((0, 0), (0, LANES - 4 * DN_HEADS))).astype(bf16)
    p['conv'] = jnp.pad(dn_conv, ((0, SUBLANES - DN_CONV), (0, 0)))
    p['alog'] = jnp.pad(dn_A_log.reshape(1, 2 * DN_HEADS), ((0, 0), (0, LANES - 2 * DN_HEADS)))
    p['dtb'] = jnp.pad(dn_dt_bias.reshape(1, 2 * DN_HEADS), ((0, 0), (0, LANES - 2 * DN_HEADS)))
    p['ln_in_g'] = ln_in_g.reshape(1, D_MODEL)
    p['ln_in_b'] = ln_in_b.reshape(1, D_MODEL)
    p['ln_g'] = ln_g.reshape(1, D_MODEL)
    p['ln_b'] = ln_b.reshape(1, D_MODEL)
    p['norm_g'] = jnp.tile(dn_norm_g.reshape(1, HEAD_DIM), (1, DN_HEADS))
    p['w_memk'] = w_mem_kv[:, 0:MEM_WIDTH].astype(bf16)
    p['w_memvt'] = w_mem_kv[:, MEM_WIDTH:2 * MEM_WIDTH].T.astype(bf16)
    p['wo_attn'] = jnp.concatenate([w_out[h * HEAD_DIM:(h + 1) * HEAD_DIM] for h in order], axis=0).astype(bf16)
    p['wo_dn'] = w_out[ATTN_WIDTH:ATTN_WIDTH + DN_WIDTH].astype(bf16)
    p['wo_mem'] = w_out[ATTN_WIDTH + DN_WIDTH:].astype(bf16)
    p['sink'] = jnp.stack([attn_sink[h] for h in order]) * LOG2E

    blk = np.arange(DN_WIDTH) // HEAD_DIM
    bd = (blk[:, None] == blk[None, :])
    p['ones_bd'] = jnp.asarray(bd, bf16)
    p['tri'] = jnp.asarray(np.tril(np.ones((DN_CHUNK, DN_CHUNK))), bf16)
    esel = np.zeros((LANES, 4 * DN_WIDTH), np.float32)
    for c in range(4 * DN_HEADS):
        esel[c, c * HEAD_DIM:(c + 1) * HEAD_DIM] = 1.0
    p['esel'] = jnp.asarray(esel, bf16)

    t = jnp.arange(BLOCK)[None, :]
    s = jnp.arange(3 * BLOCK)[:, None]
    rel = s - BLOCK - t
    onehot = jax.nn.one_hot(_t5_bucket(rel), REL_BUCKETS, dtype=f32)
    rb = jnp.stack([rel_bias[:, h] for h in order], axis=0)
    bias = jnp.einsum('stk,hk->sht', onehot, rb, precision=lax.Precision.HIGHEST)
    bias = jnp.where((jnp.abs(rel) <= WINDOW)[:, None, :], bias * LOG2E, NEG)
    bias = bias.reshape(3 * BLOCK, ATTN_HEADS * BLOCK)
    no_prev = jnp.where(s < BLOCK, NEG, 0.0)
    no_next = jnp.where(s >= 2 * BLOCK, NEG, 0.0)
    p['bias'] = jnp.stack([bias, bias + no_next, bias + no_prev])
    return p


SCAN_SEQS = 4


def _trunk(x, mem, p, tm_in=1024, tm=1024, tb_intra=512, tb_scan=1024):
    B, L, _ = x.shape
    tm_in = min(tm_in, L)
    tm = min(tm, L)
    tb_intra = min(tb_intra, L)
    tb_scan = min(tb_scan, L)
    nbat = math.gcd(B, SCAN_SEQS)
    aq, ak, av, az, mq, mz, dq, dk, dv, dz, gb, xa = _inproj(x, p, tm_in)
    mk, mvt = _memkv(mem, p)
    ya, ym = _attn(aq, ak, av, az, mq, mz, mk, mvt, p)
    m, n, qp, op, gt = _dn_intra(dq, dk, dv, gb, p, tb_intra)
    yf, yb = _dn_scan(m, n, qp, op, gt, tb_scan, nbat)
    return _outproj(xa, ya, ym, yf, yb, dz, p, tm)


def kernel(x_prompt, x_sample, mem_prompt, mem_sample, ln_in_g, ln_in_b, rel_bias, w_in, attn_sink, dn_conv,
           dn_A_log, dn_dt_bias, dn_norm_g, w_mem_kv, w_out, ln_g, ln_b):
    p = _prepare(ln_in_g, ln_in_b, rel_bias, w_in[0], attn_sink[0], dn_conv[0], dn_A_log[0], dn_dt_bias[0],
                 dn_norm_g[0], w_mem_kv[0], w_out[0], ln_g[0], ln_b[0])
    return (_trunk(x_prompt, mem_prompt, p), _trunk(x_sample, mem_sample, p))
```

```python
import functools
import math

import jax
import jax.numpy as jnp
import numpy as np
from jax import lax
from jax.experimental import pallas as pl
from jax.experimental.pallas import tpu as pltpu

f32 = jnp.float32
bf16 = jnp.bfloat16

D_MODEL = 1024
HEAD_DIM = 64
ATTN_HEADS = 8
ATTN_KV_HEADS = 2
ATTN_GROUP = ATTN_HEADS // ATTN_KV_HEADS
ATTN_WIDTH = ATTN_HEADS * HEAD_DIM
ATTN_KV_WIDTH = ATTN_KV_HEADS * HEAD_DIM
WINDOW = 128
BLOCK = 128
REL_BUCKETS = 32
REL_MAX_DIST = 128
DN_HEADS = 4
DN_WIDTH = DN_HEADS * HEAD_DIM
DN_CONV = 5
DN_CONV_DIM = 3 * DN_WIDTH
DN_CHUNK = 64
MEM_HEADS = 4
MEM_WIDTH = MEM_HEADS * HEAD_DIM
DEPTH = 1
DEEPNORM_ALPHA = (2 * DEPTH) ** 0.25
LN_EPS = 1e-5
RMS_EPS = 1e-6
NEG = -1e30
LOG2E = 1.4426950408889634

LANES = 128
SUBLANES = 8
HALO = SUBLANES
VMEM_LIMIT = 56 * 1024 * 1024

NT = (((1,), (1,)), ((), ()))
TN = (((0,), (0,)), ((), ()))


def _dot(a, b):
    return jnp.dot(a, b, preferred_element_type=f32)


def _dot_nt(a, b):
    return lax.dot_general(a, b, NT, preferred_element_type=f32)


def _split3(x):
    hi = x.astype(bf16)
    r1 = x - hi.astype(f32)
    mid = r1.astype(bf16)
    lo = (r1 - mid.astype(f32)).astype(bf16)
    return hi, mid, lo


def _dot_exact_rhs(sel, x):
    hi, mid, lo = _split3(x)
    return _dot(sel, hi) + _dot(sel, mid) + _dot(sel, lo)


def _layer_norm(x, g, b):
    mu = jnp.mean(x, axis=-1, keepdims=True)
    xc = x - mu
    var = jnp.mean(xc * xc, axis=-1, keepdims=True)
    return xc * lax.rsqrt(var + LN_EPS) * g + b


def _silu(x):
    return x / (1.0 + jnp.exp(-x))


def _inproj_kernel(x_ref, xp_ref, xn_ref, lng_ref, lnb_ref, wattn_ref, wmem_ref, wdn_ref, wab_ref,
                   conv_ref, alog_ref, dtb_ref, ones_ref,
                   aq_ref, ak_ref, av_ref, az_ref, mq_ref, mz_ref, dq_ref, dk_ref, dv_ref, dz_ref, gb_ref, xa_ref,
                   hbuf, *, tm, nsub):
    i = pl.program_id(1)
    n = pl.num_programs(1)
    g = lng_ref[...]
    b = lnb_ref[...]
    ts = tm // nsub
    rows = ts + 2 * HALO
    half = DN_CONV // 2
    ones = ones_ref[...]

    def l2n(t, scale):
        ss = _dot((t * t).astype(bf16), ones)
        return t * (lax.rsqrt(ss + 1e-6) * scale)

    def sub_tile(k):
        r0 = k * ts
        rs = slice(r0, r0 + ts)
        st = {}

        def ln():
            xln = _layer_norm(x_ref[0, rs, :], g, b)
            xa_ref[0, rs, :] = DEEPNORM_ALPHA * xln
            st['xb'] = xln.astype(bf16)
            prev = xp_ref[0] if k == 0 else x_ref[0, r0 - HALO:r0, :]
            nxt = xn_ref[0] if k == nsub - 1 else x_ref[0, r0 + ts:r0 + ts + HALO, :]
            st['xsb'] = jnp.concatenate([_layer_norm(prev, g, b), xln, _layer_norm(nxt, g, b)], axis=0).astype(bf16)

        def proj_qkv():
            hbuf[k] = _dot(st['xsb'], wdn_ref[:, 0:DN_CONV_DIM])
            if k == 0:
                hbuf[k, 0:HALO, :] = jnp.where(i > 0, hbuf[k, 0:HALO, :], 0.0)
            if k == nsub - 1:
                hbuf[k, HALO + ts:rows, :] = jnp.where(i < n - 1, hbuf[k, HALO + ts:rows, :], 0.0)
            st['ab'] = _dot(st['xb'], wab_ref[...])

        def proj_aq():
            aq_ref[0, rs, :] = _dot(st['xb'], wattn_ref[:, 0:512]).astype(bf16)

        def conv():
            hfull = hbuf[k]
            c = conv_ref[half:half + 1, :] * hfull[HALO:HALO + ts, :]
            for j in range(DN_CONV):
                if j != half:
                    c = c + conv_ref[j:j + 1, :] * pltpu.roll(hfull, (half - j) % rows, axis=0)[HALO:HALO + ts, :]
            st['c'] = c

        def proj_kv_az():
            kv = _dot(st['xb'], wattn_ref[:, 512:768]).astype(bf16)
            ak_ref[0, rs, :] = kv[:, 0:ATTN_KV_WIDTH]
            av_ref[0, rs, :] = kv[:, ATTN_KV_WIDTH:2 * ATTN_KV_WIDTH]
            az_ref[0, rs, :] = _silu(_dot(st['xb'], wattn_ref[:, 768:1280])).astype(bf16)

        def act_q():
            st['c'] = _silu(st['c'])
            dq_ref[0, rs, :] = l2n(st['c'][:, 0:DN_WIDTH], HEAD_DIM ** -0.5).astype(bf16)

        def proj_dz():
            dz_ref[0, rs, :] = _silu(_dot(st['xb'], wdn_ref[:, DN_CONV_DIM:DN_CONV_DIM + DN_WIDTH])).astype(bf16)

        def act_k():
            dk_ref[0, rs, :] = l2n(st['c'][:, DN_WIDTH:2 * DN_WIDTH], 1.0).astype(bf16)

        def proj_mz():
            mz_ref[0, rs, :] = _silu(_dot(st['xb'], wmem_ref[:, 256:512])).astype(bf16)

        def act_v_gb():
            dv_ref[0, rs, :] = st['c'][:, 2 * DN_WIDTH:3 * DN_WIDTH].astype(bf16)
            ab = st['ab']
            z = ab + dtb_ref[...]
            sp = jnp.maximum(z, 0.0) + jnp.log1p(jnp.exp(-jnp.abs(z)))
            gdec = -jnp.exp(alog_ref[...]) * sp
            beta = 1.0 / (1.0 + jnp.exp(-ab))
            lane = lax.broadcasted_iota(jnp.int32, ab.shape, 1)
            gb_ref[0, rs, :] = jnp.where(lane < 2 * DN_HEADS, gdec, beta)

        def proj_mq():
            mq_ref[0, rs, :] = _dot(st['xb'], wmem_ref[:, 0:256]).astype(bf16)

        return [ln, proj_qkv, proj_aq, conv, proj_kv_az, act_q, proj_dz, act_k, proj_mz, act_v_gb, proj_mq]

    subs = [sub_tile(k) for k in range(nsub)]
    nst = len(subs[0])
    for s in range(nst + SUB_LAG * (nsub - 1)):
        for k in range(nsub):
            if 0 <= s - SUB_LAG * k < nst:
                subs[k][s - SUB_LAG * k]()


SUB_TILE = 512
SUB_LAG = 2


def _inproj(x, p, tm):
    B, L, _ = x.shape
    nsub = max(tm // SUB_TILE, 1)
    nt = L // tm
    r8 = tm // HALO
    full = lambda a: pl.BlockSpec(a.shape, lambda b, i: (0,) * a.ndim)
    consts = [p['ln_in_g'], p['ln_in_b'], p['w_attn'], p['w_mem'], p['w_dn'], p['w_ab'],
              p['conv'], p['alog'], p['dtb'], p['ones_bd']]
    tok = lambda w: pl.BlockSpec((1, tm, w), lambda b, i: (b, i, 0))
    widths = [512, 128, 128, 512, 256, 256, 256, 256, 256, 256]
    out_shape = ([jax.ShapeDtypeStruct((B, L, w), bf16) for w in widths]
                 + [jax.ShapeDtypeStruct((B, L, LANES), f32), jax.ShapeDtypeStruct((B, L, D_MODEL), f32)])
    return pl.pallas_call(
        functools.partial(_inproj_kernel, tm=tm, nsub=nsub),
        grid=(B, nt),
        in_specs=[pl.BlockSpec((1, tm, D_MODEL), lambda b, i: (b, i, 0)),
                  pl.BlockSpec((1, HALO, D_MODEL), lambda b, i: (b, jnp.maximum(i * r8 - 1, 0), 0)),
                  pl.BlockSpec((1, HALO, D_MODEL), lambda b, i: (b, jnp.minimum((i + 1) * r8, L // HALO - 1), 0)),
                  ] + [full(a) for a in consts],
        out_specs=[tok(w) for w in widths] + [tok(LANES), tok(D_MODEL)],
        out_shape=out_shape,
        scratch_shapes=[pltpu.VMEM((nsub, tm // nsub + 2 * HALO, DN_CONV_DIM), f32)],
        compiler_params=pltpu.CompilerParams(dimension_semantics=("parallel", "parallel"),
                                             vmem_limit_bytes=VMEM_LIMIT),
        name="inproj",
    )(x, x, x, *consts)


def _memkv_kernel(m_ref, lng_ref, lnb_ref, wk_ref, wvt_ref, k_ref, vt_ref):
    mb = _layer_norm(m_ref[0], lng_ref[...], lnb_ref[...]).astype(bf16)
    k_ref[0] = _dot(mb, wk_ref[...]).astype(bf16)
    vt_ref[0] = _dot_nt(wvt_ref[...], mb).astype(bf16)


def _memkv(mem, p):
    B, M, _ = mem.shape
    return pl.pallas_call(
        _memkv_kernel,
        grid=(B,),
        in_specs=[pl.BlockSpec((1, M, D_MODEL), lambda b: (b, 0, 0)),
                  pl.BlockSpec((1, D_MODEL), lambda b: (0, 0)),
                  pl.BlockSpec((1, D_MODEL), lambda b: (0, 0)),
                  pl.BlockSpec((D_MODEL, MEM_WIDTH), lambda b: (0, 0)),
                  pl.BlockSpec((MEM_WIDTH, D_MODEL), lambda b: (0, 0))],
        out_specs=[pl.BlockSpec((1, M, MEM_WIDTH), lambda b: (b, 0, 0)),
                   pl.BlockSpec((1, MEM_WIDTH, M), lambda b: (b, 0, 0))],
        out_shape=[jax.ShapeDtypeStruct((B, M, MEM_WIDTH), bf16), jax.ShapeDtypeStruct((B, MEM_WIDTH, M), bf16)],
        compiler_params=pltpu.CompilerParams(dimension_semantics=("parallel",), vmem_limit_bytes=VMEM_LIMIT),
        name="memkv",
    )(mem, p['ln_in_g'], p['ln_in_b'], p['w_memk'], p['w_memvt'])


def _softmax_cols(s, extra=None):
    m = jnp.max(s, axis=0, keepdims=True)
    if extra is not None:
        m = jnp.maximum(m, extra)
    e = jnp.exp2(s - m)
    return e.astype(bf16), (None if extra is None else jnp.exp2(extra - m))


QB = 4


def _attn_kernel(sink_ref, aq_ref, ak_ref, av_ref, az_ref, mq_ref, mz_ref, mk_ref, mvt_ref,
                 bias_first_ref, bias_mid_ref, bias_last_ref, ya_ref, ym_ref):
    j = pl.program_id(1)
    nb = QB * pl.num_programs(1)
    bias_refs = [bias_first_ref] + [bias_mid_ref] * (QB - 2) + [bias_last_ref]
    lane = lax.broadcasted_iota(jnp.int32, (BLOCK, LANES), 1)
    low = lane < HEAD_DIM
    zero = jnp.zeros((BLOCK, LANES), bf16)

    def rows(ref, blk):
        return ref[0, pl.ds(pl.multiple_of(blk * BLOCK, BLOCK), BLOCK), :]

    def band(ref, i):
        return jnp.concatenate([rows(ref, jnp.maximum(i - 1, 0)), rows(ref, i), rows(ref, jnp.minimum(i + 1, nb - 1))],
                               axis=0)

    def stack_q(q):
        parts = []
        for c in range(ATTN_GROUP):
            t = q[:, c * LANES:(c + 1) * LANES]
            parts.append(jnp.where(low, t, zero))
            parts.append(jnp.where(low, zero, t))
        return jnp.concatenate(parts, axis=0)

    blks = [j * QB + t for t in range(QB)]
    kcat = [band(ak_ref, i) for i in blks]
    vcat = [band(av_ref, i) for i in blks]
    qs = [stack_q(aq_ref[0, t * BLOCK:(t + 1) * BLOCK, :]) for t in range(QB)]
    mq = mq_ref[0]
    nq = QB * BLOCK
    lane_m = lax.broadcasted_iota(jnp.int32, (nq, MEM_WIDTH), 1) // HEAD_DIM
    zm = jnp.zeros((nq, MEM_WIDTH), bf16)
    mqs = jnp.concatenate([jnp.where(lane_m == h, mq, zm) for h in range(MEM_HEADS)], axis=0)

    st = [_dot_nt(k, q) + b[0] for k, q, b in zip(kcat, qs, bias_refs)]
    smt = _dot_nt(mk_ref[0], mqs)
    ones_v = jnp.ones((2 * SUBLANES, 3 * BLOCK), bf16)
    vt = [jnp.concatenate([v.astype(f32).T.astype(bf16), ones_v], axis=0) for v in vcat]
    mvt = jnp.concatenate([mvt_ref[0], jnp.ones((2 * SUBLANES, mk_ref.shape[1]), bf16)], axis=0)

    soft = [[_softmax_cols(s[:, h * BLOCK:(h + 1) * BLOCK], sink_ref[h]) for h in range(ATTN_HEADS)] for s in st]
    softm = [_softmax_cols(smt[:, h * nq:(h + 1) * nq]) for h in range(MEM_HEADS)]

    ot = [_dot(v, jnp.concatenate([p_ for p_, _ in sf], axis=1)) for v, sf in zip(vt, soft)]
    omt = _dot(mvt, jnp.concatenate([p_ for p_, _ in softm], axis=1))

    half = HEAD_DIM
    vrows = ATTN_KV_WIDTH
    for t in range(QB):
        az = az_ref[0, t * BLOCK:(t + 1) * BLOCK, :].astype(f32)
        rinv = [1.0 / (ot[t][vrows:vrows + 1, h * BLOCK:(h + 1) * BLOCK] + soft[t][h][1]) for h in range(ATTN_HEADS)]
        for c in range(ATTN_GROUP):
            o0 = ot[t][0:half, (2 * c) * BLOCK:(2 * c + 1) * BLOCK] * rinv[2 * c]
            o1 = ot[t][half:2 * half, (2 * c + 1) * BLOCK:(2 * c + 2) * BLOCK] * rinv[2 * c + 1]
            y = jnp.concatenate([o0, o1], axis=0).T * az[:, c * LANES:(c + 1) * LANES]
            ya_ref[0, t * BLOCK:(t + 1) * BLOCK, c * LANES:(c + 1) * LANES] = y.astype(bf16)
    ymt = jnp.concatenate([omt[h * HEAD_DIM:(h + 1) * HEAD_DIM, h * nq:(h + 1) * nq]
                           * (1.0 / omt[MEM_WIDTH:MEM_WIDTH + 1, h * nq:(h + 1) * nq])
                           for h in range(MEM_HEADS)], axis=0)
    ym_ref[0] = (ymt.T * mz_ref[0].astype(f32)).astype(bf16)


def _attn(aq, ak, av, az, mq, mz, mk, mvt, p):
    B, L, _ = aq.shape
    assert L % (QB * BLOCK) == 0 and QB >= 2
    ns = L // (QB * BLOCK)
    M = mk.shape[1]
    tok = lambda w: pl.BlockSpec((1, QB * BLOCK, w), lambda b, j: (b, j, 0))
    seq = lambda w: pl.BlockSpec((1, L, w), lambda b, j: (b, 0, 0))
    bias_shape = (1, 3 * BLOCK, ATTN_HEADS * BLOCK)
    return pl.pallas_call(
        _attn_kernel,
        grid=(B, ns),
        in_specs=[pl.BlockSpec(memory_space=pltpu.SMEM),
                  tok(ATTN_WIDTH), seq(ATTN_KV_WIDTH), seq(ATTN_KV_WIDTH), tok(ATTN_WIDTH),
                  tok(MEM_WIDTH), tok(MEM_WIDTH),
                  pl.BlockSpec((1, M, MEM_WIDTH), lambda b, j: (b, 0, 0)),
                  pl.BlockSpec((1, MEM_WIDTH, M), lambda b, j: (b, 0, 0)),
                  pl.BlockSpec(bias_shape, lambda b, j: (jnp.where(j == 0, 2, 0), 0, 0)),
                  pl.BlockSpec(bias_shape, lambda b, j: (0, 0, 0)),
                  pl.BlockSpec(bias_shape, lambda b, j: (jnp.where(j == ns - 1, 1, 0), 0, 0))],
        out_specs=[tok(ATTN_WIDTH), tok(MEM_WIDTH)],
        out_shape=[jax.ShapeDtypeStruct((B, L, ATTN_WIDTH), bf16), jax.ShapeDtypeStruct((B, L, MEM_WIDTH), bf16)],
        compiler_params=pltpu.CompilerParams(dimension_semantics=("parallel", "arbitrary"),
                                             vmem_limit_bytes=VMEM_LIMIT),
        name="attn",
    )(p['sink'], aq, ak, av, az, mq, mz, mk, mvt, p['bias'], p['bias'], p['bias'])


def _bd2(y, lo):
    z = jnp.zeros_like(y)
    return jnp.concatenate([jnp.where(lo, y, z), jnp.where(lo, z, y)], axis=0)


def _halves(x):
    return x[:, 0:LANES], x[:, LANES:2 * LANES]


def _dn_intra_kernel(q_ref, k_ref, v_ref, gb_ref, tri_ref, esel_ref,
                     m_ref, n_ref, qp_ref, op_ref, gt_ref, bc_ref, *, nc, group):
    C = DN_CHUNK
    W = DN_WIDTH
    ri = lax.broadcasted_iota(jnp.int32, (C, W), 0)
    lj = lax.broadcasted_iota(jnp.int32, (C, W), 1) % C
    eye = ri == lj
    eye_f = jnp.where(eye, 1.0, 0.0)
    masks = ((ri >= lj, ri > lj), (ri <= lj, ri < lj))
    lvl_masks = [((ri // (2 * s)) == (lj // (2 * s))) & ((ri // s) != (lj // s))
                 for s in (2 ** k for k in range(C.bit_length() - 1))]
    lo = lax.broadcasted_iota(jnp.int32, (C, LANES), 1) < HEAD_DIM
    col = lax.broadcasted_iota(jnp.int32, (C, LANES), 1)
    is_bwd = (col >= DN_HEADS) & (col < 2 * DN_HEADS)
    is_beta = col >= 2 * DN_HEADS
    tri = tri_ref[...]

    srcs = []
    for c in range(nc):
        gbv = gb_ref[0, c * C:(c + 1) * C, :]
        gc = _dot_exact_rhs(tri, gbv)
        srcs.append(jnp.where(is_beta, gbv, jnp.where(is_bwd, gc[C - 1:C, :] - gc + gbv, gc)))
    hi, mid, _ = _split3(jnp.concatenate(srcs, axis=0))
    bc_ref[:, 0:2 * W] = _dot(hi, esel_ref[:, 0:2 * W]) + _dot(mid, esel_ref[:, 0:2 * W])
    bc_ref[:, 2 * W:4 * W] = _dot(hi, esel_ref[:, 2 * W:4 * W])

    def pmm(x, y):
        return [_dot(xh, _bd2(yh, lo)) for xh, yh in zip(_halves(x), _halves(y))]

    def cat(parts):
        return jnp.concatenate(parts, axis=1)

    def body(gi, carry):
        cis = [gi * group + t for t in range(group)]
        sls = [pl.ds(pl.multiple_of(ci * C, C), C) for ci in cis]
        qb = [q_ref[0, sl, :] for sl in sls]
        kb = [k_ref[0, sl, :] for sl in sls]
        vb = [v_ref[0, sl, :] for sl in sls]
        gram = [cat([_dot_nt(xh, _bd2(kh, lo)) for xh, kh in zip(_halves(jnp.concatenate([k, q], axis=0)), _halves(k))])
                for k, q in zip(kb, qb)]
        inst = [(t, d) for t in range(group) for d in range(2)]
        gq = [bc_ref[sls[t], d * W:(d + 1) * W] for t, d in inst]
        bq = [bc_ref[sls[t], (2 + d) * W:(3 + d) * W] for t, d in inst]
        glast = [g[C - 1:C, :] if d == 0 else g[0:1, :] for g, (t, d) in zip(gq, inst)]
        r = [jnp.sum(jnp.where(eye, g, 0.0), axis=0, keepdims=True) for g in gq]
        dec = [jnp.where(masks[d][0], jnp.exp(jnp.minimum(g - rr, 0.0)), 0.0) for g, rr, (t, d) in zip(gq, r, inst)]
        nn = [jnp.where(masks[d][1], -(gram[t][0:C] * b * dc), 0.0) for b, dc, (t, d) in zip(bq, dec, inst)]
        tm = [eye_f + jnp.where(lvl_masks[0], a, 0.0) for a in nn]
        for lm in lvl_masks[1:]:
            off = [jnp.where(lm, a, 0.0).astype(bf16) for a in nn]
            left = [cat(pmm(a.astype(bf16), o_)).astype(bf16) for a, o_ in zip(tm, off)]
            tm = [a + cat(pmm(l_, a.astype(bf16))) for a, l_ in zip(tm, left)]
        tb = [a.astype(bf16) for a in tm]
        eg = [jnp.exp(g) for g in gq]
        vbeta = [(vb[t].astype(f32) * b).astype(bf16) for b, (t, d) in zip(bq, inst)]
        kbg = [(kb[t].astype(f32) * b * e).astype(bf16) for b, e, (t, d) in zip(bq, eg, inst)]
        uw = [[_dot(th, jnp.concatenate([_bd2(vh, lo), _bd2(kh, lo)], axis=1)).astype(bf16)
               for th, vh, kh in zip(_halves(a), _halves(v_), _halves(k_))]
              for a, v_, k_ in zip(tb, vbeta, kbg)]
        kd = [(kb[t].astype(f32) * jnp.exp(gl - g)).astype(bf16) for g, gl, (t, d) in zip(gq, glast, inst)]
        fm = [[lax.dot_general(kh, uwp, TN, preferred_element_type=f32) for kh, uwp in zip(_halves(k_), uw_)]
              for k_, uw_ in zip(kd, uw)]
        qkb = [(gram[t][C:2 * C] * dc).astype(bf16) for dc, (t, d) in zip(dec, inst)]
        qo = [[_dot(qh, jnp.concatenate([_bd2(uwp[:, 0:LANES], lo), _bd2(uwp[:, LANES:2 * LANES], lo)], axis=1))
               for qh, uwp in zip(_halves(a), uw_)] for a, uw_ in zip(qkb, uw)]
        for idx, (t, d) in enumerate(inst):
            ci, sl = cis[t], sls[t]
            f0, f1 = fm[idx]
            n_ref[0, d, ci] = cat([jnp.where(lo, f[0:C, 0:LANES], f[C:2 * C, 0:LANES])
                                   for f in (f0, f1)]).astype(bf16)
            m_ref[0, d, ci] = cat([-jnp.where(lo, f[0:C, LANES:2 * LANES], f[C:2 * C, LANES:2 * LANES])
                                   for f in (f0, f1)]).astype(bf16)
            qd = qb[t].astype(f32) * eg[idx]
            qp_ref[0, d, sl, :] = (qd - cat([p_[:, LANES:2 * LANES] for p_ in qo[idx]])).astype(bf16)
            op_ref[0, d, sl, :] = cat([p_[:, 0:LANES] for p_ in qo[idx]]).astype(bf16)
            gt_ref[0, d, ci] = jnp.exp(glast[idx])
        return carry

    lax.fori_loop(0, nc // group, body, 0)


def _dn_intra(dq, dk, dv, gb, p, tb, group=4):
    B, L, _ = dq.shape
    nc = tb // DN_CHUNK
    nchunk = L // DN_CHUNK
    tok = lambda w: pl.BlockSpec((1, tb, w), lambda b, i: (b, i, 0))
    full = lambda a: pl.BlockSpec(a.shape, lambda b, i: (0,) * a.ndim)
    consts = [p['tri'], p['esel']]
    return pl.pallas_call(
        functools.partial(_dn_intra_kernel, nc=nc, group=group),
        grid=(B, L // tb),
        in_specs=[tok(DN_WIDTH), tok(DN_WIDTH), tok(DN_WIDTH), tok(LANES)] + [full(a) for a in consts],
        out_specs=[pl.BlockSpec((1, 2, nc, DN_CHUNK, DN_WIDTH), lambda b, i: (b, 0, i, 0, 0)),
                   pl.BlockSpec((1, 2, nc, DN_CHUNK, DN_WIDTH), lambda b, i: (b, 0, i, 0, 0)),
                   pl.BlockSpec((1, 2, tb, DN_WIDTH), lambda b, i: (b, 0, i, 0)),
                   pl.BlockSpec((1, 2, tb, DN_WIDTH), lambda b, i: (b, 0, i, 0)),
                   pl.BlockSpec((1, 2, nc, 1, DN_WIDTH), lambda b, i: (b, 0, i, 0, 0))],
        out_shape=[jax.ShapeDtypeStruct((B, 2, nchunk, DN_CHUNK, DN_WIDTH), bf16),
                   jax.ShapeDtypeStruct((B, 2, nchunk, DN_CHUNK, DN_WIDTH), bf16),
                   jax.ShapeDtypeStruct((B, 2, L, DN_WIDTH), bf16),
                   jax.ShapeDtypeStruct((B, 2, L, DN_WIDTH), bf16),
                   jax.ShapeDtypeStruct((B, 2, nchunk, 1, DN_WIDTH), f32)],
        scratch_shapes=[pltpu.VMEM((tb, 4 * DN_WIDTH), f32)],
        compiler_params=pltpu.CompilerParams(dimension_semantics=("parallel", "parallel"),
                                             vmem_limit_bytes=VMEM_LIMIT),
        name="dn_intra",
    )(dq, dk, dv, gb, *consts)


def _dn_scan_kernel(mf_ref, nf_ref, qf_ref, of_ref, gf_ref, mb_ref, nb_ref, qb_ref, ob_ref, gbk_ref,
                    yf_ref, yb_ref, s_ref, *, nc, nbat):
    C = DN_CHUNK

    @pl.when(pl.program_id(1) == 0)
    def _():
        s_ref[...] = jnp.zeros_like(s_ref)

    lo = lax.broadcasted_iota(jnp.int32, (C, LANES), 1) < HEAD_DIM
    dirs = ((mf_ref, nf_ref, qf_ref, of_ref, gf_ref, yf_ref), (mb_ref, nb_ref, qb_ref, ob_ref, gbk_ref, yb_ref))

    def step(j, carry):
        cis = (j, nc - 1 - j)
        sls = [pl.ds(pl.multiple_of(ci * C, C), C) for ci in cis]
        seqs = [(b, d) for b in range(nbat) for d in range(2)]
        inst = [(b, d, h) for b, d in seqs for h in range(2)]
        hs = lambda a, h: a[:, h * LANES:(h + 1) * LANES]
        m = {(b, d): dirs[d][0][b, 0, cis[d]] for b, d in seqs}
        q = {(b, d): dirs[d][2][b, 0, sls[d], :] for b, d in seqs}
        s = [s_ref[b, d, h] for b, d, h in inst]
        lhs = [jnp.concatenate([_bd2(hs(m[b, d], h), lo), hs(q[b, d], h)], axis=0) for b, d, h in inst]
        rr = [_dot(a, b_.astype(bf16)) for a, b_ in zip(lhs, s)]
        for idx, (b, d, h) in enumerate(inst):
            gt = dirs[d][4][b, 0, cis[d]]
            nn = dirs[d][1][b, 0, cis[d]]
            s_ref[b, d, h] = s[idx] * hs(gt, h) + rr[idx][0:2 * C] + _bd2(hs(nn, h), lo).astype(f32)
        for k, (b, d) in enumerate(seqs):
            y = jnp.concatenate([rr[2 * k][2 * C:3 * C], rr[2 * k + 1][2 * C:3 * C]], axis=1)
            dirs[d][5][b, sls[d], :] = (y + dirs[d][3][b, 0, sls[d], :].astype(f32)).astype(bf16)
        return carry

    lax.fori_loop(0, nc, step, 0)


def _dn_scan(m, n, qp, op, gt, tb, nbat):
    B, _, nchunk, C, W = m.shape
    L = nchunk * C
    nc = tb // C
    ns = L // tb
    assert B % nbat == 0
    fwd5 = lambda last: pl.BlockSpec((nbat, 1, nc, last, W), lambda b, i: (b, 0, i, 0, 0))
    bwd5 = lambda last: pl.BlockSpec((nbat, 1, nc, last, W), lambda b, i: (b, 1, ns - 1 - i, 0, 0))
    fwd4 = pl.BlockSpec((nbat, 1, tb, W), lambda b, i: (b, 0, i, 0))
    bwd4 = pl.BlockSpec((nbat, 1, tb, W), lambda b, i: (b, 1, ns - 1 - i, 0))
    return pl.pallas_call(
        functools.partial(_dn_scan_kernel, nc=nc, nbat=nbat),
        grid=(B // nbat, ns),
        in_specs=[fwd5(C), fwd5(C), fwd4, fwd4, fwd5(1), bwd5(C), bwd5(C), bwd4, bwd4, bwd5(1)],
        out_specs=[pl.BlockSpec((nbat, tb, W), lambda b, i: (b, i, 0)),
                   pl.BlockSpec((nbat, tb, W), lambda b, i: (b, ns - 1 - i, 0))],
        out_shape=[jax.ShapeDtypeStruct((B, L, W), bf16), jax.ShapeDtypeStruct((B, L, W), bf16)],
        scratch_shapes=[pltpu.VMEM((nbat, 2, 2, 2 * C, LANES), f32)],
        compiler_params=pltpu.CompilerParams(dimension_semantics=("parallel", "arbitrary"),
                                             vmem_limit_bytes=VMEM_LIMIT),
        name="dn_scan",
    )(m, n, qp, op, gt, m, n, qp, op, gt)


def _outproj_kernel(xa_ref, ya_ref, ym_ref, yf_ref, yb_ref, dz_ref, lg_ref, lb_ref,
                    ng_ref, wa_ref, wd_ref, wm_ref, ones_ref, o_ref):
    o = yf_ref[0].astype(f32) + yb_ref[0].astype(f32)
    ms = _dot((o * o).astype(bf16), ones_ref[...]) * (1.0 / HEAD_DIM)
    yd = o * lax.rsqrt(ms + RMS_EPS) * ng_ref[...] * dz_ref[0].astype(f32)
    y = _dot(ya_ref[0], wa_ref[...]) + _dot(yd.astype(bf16), wd_ref[...]) + _dot(ym_ref[0], wm_ref[...])
    o_ref[0] = _layer_norm(xa_ref[0] + y, lg_ref[...], lb_ref[...])


def _outproj(xa, ya, ym, yf, yb, dz, p, tm):
    B, L, _ = xa.shape
    tok = lambda w: pl.BlockSpec((1, tm, w), lambda b, i: (b, i, 0))
    full = lambda a: pl.BlockSpec(a.shape, lambda b, i: (0,) * a.ndim)
    consts = [p['ln_g'], p['ln_b'], p['norm_g'], p['wo_attn'], p['wo_dn'], p['wo_mem'], p['ones_bd']]
    return pl.pallas_call(
        _outproj_kernel,
        grid=(B, L // tm),
        in_specs=[tok(D_MODEL), tok(ATTN_WIDTH), tok(MEM_WIDTH), tok(DN_WIDTH), tok(DN_WIDTH), tok(DN_WIDTH)]
                 + [full(a) for a in consts],
        out_specs=tok(D_MODEL),
        out_shape=jax.ShapeDtypeStruct((B, L, D_MODEL), f32),
        compiler_params=pltpu.CompilerParams(dimension_semantics=("parallel", "parallel"),
                                             vmem_limit_bytes=VMEM_LIMIT),
        name="outproj",
    )(xa, ya, ym, yf, yb, dz, *consts)


def _t5_bucket(rel):
    nb = REL_BUCKETS // 2
    max_exact = nb // 2
    n = jnp.abs(rel)
    large = max_exact + (jnp.log(jnp.maximum(n, 1).astype(f32) / max_exact)
                         / math.log(REL_MAX_DIST / max_exact) * (nb - max_exact)).astype(jnp.int32)
    large = jnp.minimum(large, nb - 1)
    return jnp.where(rel > 0, nb, 0) + jnp.where(n < max_exact, n, large)


def _prepare(ln_in_g, ln_in_b, rel_bias, w_in, attn_sink, dn_conv, dn_A_log, dn_dt_bias, dn_norm_g,
             w_mem_kv, w_out, ln_g, ln_b):
    offs = np.cumsum([0, ATTN_WIDTH, ATTN_KV_WIDTH, ATTN_KV_WIDTH, ATTN_WIDTH, DN_WIDTH, DN_WIDTH, DN_WIDTH,
                      DN_WIDTH, 2 * DN_HEADS, 2 * DN_HEADS, MEM_WIDTH, MEM_WIDTH])
    (w_aq, w_ak, w_av, w_az, w_dq, w_dk, w_dv, w_dz, w_da, w_db, w_mq, w_mz) = [
        w_in[:, int(offs[j]):int(offs[j + 1])] for j in range(12)]
    order = [kv * ATTN_GROUP + c for c in range(ATTN_GROUP) for kv in range(ATTN_KV_HEADS)]
    head_cols = lambda w: jnp.concatenate([w[:, h * HEAD_DIM:(h + 1) * HEAD_DIM] for h in order], axis=1)
    scale = HEAD_DIM ** -0.5 * LOG2E
    p = {}
    p['w_attn'] = jnp.concatenate([head_cols(w_aq) * scale, w_ak, w_av, head_cols(w_az)], axis=1).astype(bf16)
    p['w_mem'] = jnp.concatenate([w_mq * scale, w_mz], axis=1).astype(bf16)
    p['w_dn'] = jnp.concatenate([w_dq, w_dk, w_dv, w_dz], axis=1).astype(bf16)
    p['w_ab'] = jnp.pad(jnp.concatenate([w_da, w_db], axis=1), ((0, 0), (0, LANES - 4 * DN_HEADS))).astype(bf16)
    p['conv'] = jnp.pad(dn_conv, ((0, SUBLANES - DN_CONV), (0, 0)))
    p['alog'] = jnp.pad(dn_A_log.reshape(1, 2 * DN_HEADS), ((0, 0), (0, LANES - 2 * DN_HEADS)))
    p['dtb'] = jnp.pad(dn_dt_bias.reshape(1, 2 * DN_HEADS), ((0, 0), (0, LANES - 2 * DN_HEADS)))
    p['ln_in_g'] = ln_in_g.reshape(1, D_MODEL)
    p['ln_in_b'] = ln_in_b.reshape(1, D_MODEL)
    p['ln_g'] = ln_g.reshape(1, D_MODEL)
    p['ln_b'] = ln_b.reshape(1, D_MODEL)
    p['norm_g'] = jnp.tile(dn_norm_g.reshape(1, HEAD_DIM), (1, DN_HEADS))
    p['w_memk'] = w_mem_kv[:, 0:MEM_WIDTH].astype(bf16)
    p['w_memvt'] = w_mem_kv[:, MEM_WIDTH:2 * MEM_WIDTH].T.astype(bf16)
    p['wo_attn'] = jnp.concatenate([w_out[h * HEAD_DIM:(h + 1) * HEAD_DIM] for h in order], axis=0).astype(bf16)
    p['wo_dn'] = w_out[ATTN_WIDTH:ATTN_WIDTH + DN_WIDTH].astype(bf16)
    p['wo_mem'] = w_out[ATTN_WIDTH + DN_WIDTH:].astype(bf16)
    p['sink'] = jnp.stack([attn_sink[h] for h in order]) * LOG2E

    blk = np.arange(DN_WIDTH) // HEAD_DIM
    bd = (blk[:, None] == blk[None, :])
    p['ones_bd'] = jnp.asarray(bd, bf16)
    p['tri'] = jnp.asarray(np.tril(np.ones((DN_CHUNK, DN_CHUNK))), bf16)
    esel = np.zeros((LANES, 4 * DN_WIDTH), np.float32)
    for c in range(4 * DN_HEADS):
        esel[c, c * HEAD_DIM:(c + 1) * HEAD_DIM] = 1.0
    p['esel'] = jnp.asarray(esel, bf16)

    t = jnp.arange(BLOCK)[None, :]
    s = jnp.arange(3 * BLOCK)[:, None]
    rel = s - BLOCK - t
    onehot = jax.nn.one_hot(_t5_bucket(rel), REL_BUCKETS, dtype=f32)
    rb = jnp.stack([rel_bias[:, h] for h in order], axis=0)
    bias = jnp.einsum('stk,hk->sht', onehot, rb, precision=lax.Precision.HIGHEST)
    bias = jnp.where((jnp.abs(rel) <= WINDOW)[:, None, :], bias * LOG2E, NEG)
    bias = bias.reshape(3 * BLOCK, ATTN_HEADS * BLOCK)
    no_prev = jnp.where(s < BLOCK, NEG, 0.0)
    no_next = jnp.where(s >= 2 * BLOCK, NEG, 0.0)
    p['bias'] = jnp.stack([bias, bias + no_next, bias + no_prev])
    return p


SCAN_SEQS = 4


def _trunk(x, mem, p, tm_in=1024, tm=1024, tb_intra=1024, tb_scan=1024):
    B, L, _ = x.shape
    tm_in = min(tm_in, L)
    tm = min(tm, L)
    tb_intra = min(tb_intra, L)
    tb_scan = min(tb_scan, L)
    nbat = math.gcd(B, SCAN_SEQS)
    aq, ak, av, az, mq, mz, dq, dk, dv, dz, gb, xa = _inproj(x, p, tm_in)
    mk, mvt = _memkv(mem, p)
    ya, ym = _attn(aq, ak, av, az, mq, mz, mk, mvt, p)
    m, n, qp, op, gt = _dn_intra(dq, dk, dv, gb, p, tb_intra)
    yf, yb = _dn_scan(m, n, qp, op, gt, tb_scan, nbat)
    return _outproj(xa, ya, ym, yf, yb, dz, p, tm)


def kernel(x_prompt, x_sample, mem_prompt, mem_sample, ln_in_g, ln_in_b, rel_bias, w_in, attn_sink, dn_conv,
           dn_A_log, dn_dt_bias, dn_norm_g, w_mem_kv, w_out, ln_g, ln_b):
    p = _prepare(ln_in_g, ln_in_b, rel_bias, w_in[0], attn_sink[0], dn_conv[0], dn_A_log[0], dn_dt_bias[0],
                 dn_norm_g[0], w_mem_kv[0], w_out[0], ln_g[0], ln_b[0])
    return (_trunk(x_prompt, mem_prompt, p), _trunk(x_sample, mem_sample, p))
```

```python
import functools
import math

import jax
import jax.numpy as jnp
import numpy as np
from jax import lax
from jax.experimental import pallas as pl
from jax.experimental.pallas import tpu as pltpu

f32 = jnp.float32
bf16 = jnp.bfloat16

D_MODEL = 1024
HEAD_DIM = 64
ATTN_HEADS = 8
ATTN_KV_HEADS = 2
ATTN_GROUP = ATTN_HEADS // ATTN_KV_HEADS
ATTN_WIDTH = ATTN_HEADS * HEAD_DIM
ATTN_KV_WIDTH = ATTN_KV_HEADS * HEAD_DIM
WINDOW = 128
BLOCK = 128
REL_BUCKETS = 32
REL_MAX_DIST = 128
DN_HEADS = 4
DN_WIDTH = DN_HEADS * HEAD_DIM
DN_CONV = 5
DN_CONV_DIM = 3 * DN_WIDTH
DN_CHUNK = 64
MEM_HEADS = 4
MEM_WIDTH = MEM_HEADS * HEAD_DIM
DEPTH = 1
DEEPNORM_ALPHA = (2 * DEPTH) ** 0.25
LN_EPS = 1e-5
RMS_EPS = 1e-6
NEG = -1e30
LOG2E = 1.4426950408889634

LANES = 128
SUBLANES = 8
HALO = SUBLANES
VMEM_LIMIT = 56 * 1024 * 1024

NT = (((1,), (1,)), ((), ()))
TN = (((0,), (0,)), ((), ()))


def _dot(a, b):
    return jnp.dot(a, b, preferred_element_type=f32)


def _dot_nt(a, b):
    return lax.dot_general(a, b, NT, preferred_element_type=f32)


def _split3(x):
    hi = x.astype(bf16)
    r1 = x - hi.astype(f32)
    mid = r1.astype(bf16)
    lo = (r1 - mid.astype(f32)).astype(bf16)
    return hi, mid, lo


def _dot_exact_rhs(sel, x):
    hi, mid, lo = _split3(x)
    return _dot(sel, hi) + _dot(sel, mid) + _dot(sel, lo)


def _layer_norm(x, g, b):
    mu = jnp.mean(x, axis=-1, keepdims=True)
    xc = x - mu
    var = jnp.mean(xc * xc, axis=-1, keepdims=True)
    return xc * lax.rsqrt(var + LN_EPS) * g + b


def _silu(x):
    return x / (1.0 + jnp.exp(-x))


def _inproj_kernel(x_ref, xp_ref, xn_ref, lng_ref, lnb_ref, wattn_ref, wmem_ref, wdn_ref, wab_ref,
                   conv_ref, alog_ref, dtb_ref, ones_ref,
                   aq_ref, ak_ref, av_ref, az_ref, mq_ref, mz_ref, dq_ref, dk_ref, dv_ref, dz_ref, gb_ref, xa_ref,
                   hbuf, *, tm, nsub):
    i = pl.program_id(1)
    n = pl.num_programs(1)
    g = lng_ref[...]
    b = lnb_ref[...]
    ts = tm // nsub
    rows = ts + 2 * HALO
    half = DN_CONV // 2
    ones = ones_ref[...]

    def l2n(t, scale):
        ss = _dot((t * t).astype(bf16), ones)
        return t * (lax.rsqrt(ss + 1e-6) * scale)

    def sub_tile(k):
        r0 = k * ts
        rs = slice(r0, r0 + ts)
        st = {}

        def ln():
            xln = _layer_norm(x_ref[0, rs, :], g, b)
            xa_ref[0, rs, :] = DEEPNORM_ALPHA * xln
            st['xb'] = xln.astype(bf16)
            prev = xp_ref[0] if k == 0 else x_ref[0, r0 - HALO:r0, :]
            nxt = xn_ref[0] if k == nsub - 1 else x_ref[0, r0 + ts:r0 + ts + HALO, :]
            st['xsb'] = jnp.concatenate([_layer_norm(prev, g, b), xln, _layer_norm(nxt, g, b)], axis=0).astype(bf16)

        def proj_qkv():
            hbuf[k] = _dot(st['xsb'], wdn_ref[:, 0:DN_CONV_DIM])
            if k == 0:
                hbuf[k, 0:HALO, :] = jnp.where(i > 0, hbuf[k, 0:HALO, :], 0.0)
            if k == nsub - 1:
                hbuf[k, HALO + ts:rows, :] = jnp.where(i < n - 1, hbuf[k, HALO + ts:rows, :], 0.0)
            st['ab'] = _dot(st['xb'], wab_ref[...])

        def proj_aq():
            aq_ref[0, rs, :] = _dot(st['xb'], wattn_ref[:, 0:512]).astype(bf16)

        def conv():
            hfull = hbuf[k]
            c = conv_ref[half:half + 1, :] * hfull[HALO:HALO + ts, :]
            for j in range(DN_CONV):
                if j != half:
                    c = c + conv_ref[j:j + 1, :] * pltpu.roll(hfull, (half - j) % rows, axis=0)[HALO:HALO + ts, :]
            st['c'] = c

        def proj_kv_az():
            kv = _dot(st['xb'], wattn_ref[:, 512:768]).astype(bf16)
            ak_ref[0, rs, :] = kv[:, 0:ATTN_KV_WIDTH]
            av_ref[0, rs, :] = kv[:, ATTN_KV_WIDTH:2 * ATTN_KV_WIDTH]
            az_ref[0, rs, :] = _silu(_dot(st['xb'], wattn_ref[:, 768:1280])).astype(bf16)

        def act_q():
            st['c'] = _silu(st['c'])
            dq_ref[0, rs, :] = l2n(st['c'][:, 0:DN_WIDTH], HEAD_DIM ** -0.5).astype(bf16)

        def proj_dz():
            dz_ref[0, rs, :] = _silu(_dot(st['xb'], wdn_ref[:, DN_CONV_DIM:DN_CONV_DIM + DN_WIDTH])).astype(bf16)

        def act_k():
            dk_ref[0, rs, :] = l2n(st['c'][:, DN_WIDTH:2 * DN_WIDTH], 1.0).astype(bf16)

        def proj_mz():
            mz_ref[0, rs, :] = _silu(_dot(st['xb'], wmem_ref[:, 256:512])).astype(bf16)

        def act_v_gb():
            dv_ref[0, rs, :] = st['c'][:, 2 * DN_WIDTH:3 * DN_WIDTH].astype(bf16)
            ab = st['ab']
            z = ab + dtb_ref[...]
            sp = jnp.maximum(z, 0.0) + jnp.log1p(jnp.exp(-jnp.abs(z)))
            gdec = -jnp.exp(alog_ref[...]) * sp
            beta = 1.0 / (1.0 + jnp.exp(-ab))
            lane = lax.broadcasted_iota(jnp.int32, ab.shape, 1)
            gb_ref[0, rs, :] = jnp.where(lane < 2 * DN_HEADS, gdec, beta)

        def proj_mq():
            mq_ref[0, rs, :] = _dot(st['xb'], wmem_ref[:, 0:256]).astype(bf16)

        return [ln, proj_qkv, proj_aq, conv, proj_kv_az, act_q, proj_dz, act_k, proj_mz, act_v_gb, proj_mq]

    subs = [sub_tile(k) for k in range(nsub)]
    nst = len(subs[0])
    for s in range(nst + SUB_LAG * (nsub - 1)):
        for k in range(nsub):
            if 0 <= s - SUB_LAG * k < nst:
                subs[k][s - SUB_LAG * k]()


SUB_TILE = 512
SUB_LAG = 2


def _inproj(x, p, tm):
    B, L, _ = x.shape
    nsub = max(tm // SUB_TILE, 1)
    nt = L // tm
    r8 = tm // HALO
    full = lambda a: pl.BlockSpec(a.shape, lambda b, i: (0,) * a.ndim)
    consts = [p['ln_in_g'], p['ln_in_b'], p['w_attn'], p['w_mem'], p['w_dn'], p['w_ab'],
              p['conv'], p['alog'], p['dtb'], p['ones_bd']]
    tok = lambda w: pl.BlockSpec((1, tm, w), lambda b, i: (b, i, 0))
    widths = [512, 128, 128, 512, 256, 256, 256, 256, 256, 256]
    out_shape = ([jax.ShapeDtypeStruct((B, L, w), bf16) for w in widths]
                 + [jax.ShapeDtypeStruct((B, L, LANES), f32), jax.ShapeDtypeStruct((B, L, D_MODEL), f32)])
    return pl.pallas_call(
        functools.partial(_inproj_kernel, tm=tm, nsub=nsub),
        grid=(B, nt),
        in_specs=[pl.BlockSpec((1, tm, D_MODEL), lambda b, i: (b, i, 0)),
                  pl.BlockSpec((1, HALO, D_MODEL), lambda b, i: (b, jnp.maximum(i * r8 - 1, 0), 0)),
                  pl.BlockSpec((1, HALO, D_MODEL), lambda b, i: (b, jnp.minimum((i + 1) * r8, L // HALO - 1), 0)),
                  ] + [full(a) for a in consts],
        out_specs=[tok(w) for w in widths] + [tok(LANES), tok(D_MODEL)],
        out_shape=out_shape,
        scratch_shapes=[pltpu.VMEM((nsub, tm // nsub + 2 * HALO, DN_CONV_DIM), f32)],
        compiler_params=pltpu.CompilerParams(dimension_semantics=("parallel", "parallel"),
                                             vmem_limit_bytes=VMEM_LIMIT),
        name="inproj",
    )(x, x, x, *consts)


def _memkv_kernel(m_ref, lng_ref, lnb_ref, wk_ref, wvt_ref, k_ref, vt_ref):
    mb = _layer_norm(m_ref[0], lng_ref[...], lnb_ref[...]).astype(bf16)
    k_ref[0] = _dot(mb, wk_ref[...]).astype(bf16)
    vt_ref[0] = _dot_nt(wvt_ref[...], mb).astype(bf16)


def _memkv(mem, p):
    B, M, _ = mem.shape
    return pl.pallas_call(
        _memkv_kernel,
        grid=(B,),
        in_specs=[pl.BlockSpec((1, M, D_MODEL), lambda b: (b, 0, 0)),
                  pl.BlockSpec((1, D_MODEL), lambda b: (0, 0)),
                  pl.BlockSpec((1, D_MODEL), lambda b: (0, 0)),
                  pl.BlockSpec((D_MODEL, MEM_WIDTH), lambda b: (0, 0)),
                  pl.BlockSpec((MEM_WIDTH, D_MODEL), lambda b: (0, 0))],
        out_specs=[pl.BlockSpec((1, M, MEM_WIDTH), lambda b: (b, 0, 0)),
                   pl.BlockSpec((1, MEM_WIDTH, M), lambda b: (b, 0, 0))],
        out_shape=[jax.ShapeDtypeStruct((B, M, MEM_WIDTH), bf16), jax.ShapeDtypeStruct((B, MEM_WIDTH, M), bf16)],
        compiler_params=pltpu.CompilerParams(dimension_semantics=("parallel",), vmem_limit_bytes=VMEM_LIMIT),
        name="memkv",
    )(mem, p['ln_in_g'], p['ln_in_b'], p['w_memk'], p['w_memvt'])


def _softmax_cols(s, extra=None):
    m = jnp.max(s, axis=0, keepdims=True)
    if extra is not None:
        m = jnp.maximum(m, extra)
    e = jnp.exp2(s - m)
    return e.astype(bf16), (None if extra is None else jnp.exp2(extra - m))


QB = 8


def _attn_kernel(sink_ref, aq_ref, ak_ref, av_ref, az_ref, mq_ref, mz_ref, mk_ref, mvt_ref,
                 bias_first_ref, bias_mid_ref, bias_last_ref, ya_ref, ym_ref):
    j = pl.program_id(1)
    nb = QB * pl.num_programs(1)
    bias_refs = [bias_first_ref] + [bias_mid_ref] * (QB - 2) + [bias_last_ref]
    lane = lax.broadcasted_iota(jnp.int32, (BLOCK, LANES), 1)
    low = lane < HEAD_DIM
    zero = jnp.zeros((BLOCK, LANES), bf16)

    def rows(ref, blk):
        return ref[0, pl.ds(pl.multiple_of(blk * BLOCK, BLOCK), BLOCK), :]

    def band(ref, i):
        return jnp.concatenate([rows(ref, jnp.maximum(i - 1, 0)), rows(ref, i), rows(ref, jnp.minimum(i + 1, nb - 1))],
                               axis=0)

    def stack_q(q):
        parts = []
        for c in range(ATTN_GROUP):
            t = q[:, c * LANES:(c + 1) * LANES]
            parts.append(jnp.where(low, t, zero))
            parts.append(jnp.where(low, zero, t))
        return jnp.concatenate(parts, axis=0)

    blks = [j * QB + t for t in range(QB)]
    kcat = [band(ak_ref, i) for i in blks]
    vcat = [band(av_ref, i) for i in blks]
    qs = [stack_q(aq_ref[0, t * BLOCK:(t + 1) * BLOCK, :]) for t in range(QB)]
    mq = mq_ref[0]
    nq = QB * BLOCK
    lane_m = lax.broadcasted_iota(jnp.int32, (nq, MEM_WIDTH), 1) // HEAD_DIM
    zm = jnp.zeros((nq, MEM_WIDTH), bf16)
    mqs = jnp.concatenate([jnp.where(lane_m == h, mq, zm) for h in range(MEM_HEADS)], axis=0)

    st = [_dot_nt(k, q) + b[0] for k, q, b in zip(kcat, qs, bias_refs)]
    smt = _dot_nt(mk_ref[0], mqs)
    ones_v = jnp.ones((2 * SUBLANES, 3 * BLOCK), bf16)
    vt = [jnp.concatenate([v.astype(f32).T.astype(bf16), ones_v], axis=0) for v in vcat]
    mvt = jnp.concatenate([mvt_ref[0], jnp.ones((2 * SUBLANES, mk_ref.shape[1]), bf16)], axis=0)

    soft = [[_softmax_cols(s[:, h * BLOCK:(h + 1) * BLOCK], sink_ref[h]) for h in range(ATTN_HEADS)] for s in st]
    softm = [_softmax_cols(smt[:, h * nq:(h + 1) * nq]) for h in range(MEM_HEADS)]

    ot = [_dot(v, jnp.concatenate([p_ for p_, _ in sf], axis=1)) for v, sf in zip(vt, soft)]
    omt = _dot(mvt, jnp.concatenate([p_ for p_, _ in softm], axis=1))

    half = HEAD_DIM
    vrows = ATTN_KV_WIDTH
    for t in range(QB):
        az = az_ref[0, t * BLOCK:(t + 1) * BLOCK, :].astype(f32)
        rinv = [1.0 / (ot[t][vrows:vrows + 1, h * BLOCK:(h + 1) * BLOCK] + soft[t][h][1]) for h in range(ATTN_HEADS)]
        for c in range(ATTN_GROUP):
            o0 = ot[t][0:half, (2 * c) * BLOCK:(2 * c + 1) * BLOCK] * rinv[2 * c]
            o1 = ot[t][half:2 * half, (2 * c + 1) * BLOCK:(2 * c + 2) * BLOCK] * rinv[2 * c + 1]
            y = jnp.concatenate([o0, o1], axis=0).T * az[:, c * LANES:(c + 1) * LANES]
            ya_ref[0, t * BLOCK:(t + 1) * BLOCK, c * LANES:(c + 1) * LANES] = y.astype(bf16)
    ymt = jnp.concatenate([omt[h * HEAD_DIM:(h + 1) * HEAD_DIM, h * nq:(h + 1) * nq]
                           * (1.0 / omt[MEM_WIDTH:MEM_WIDTH + 1, h * nq:(h + 1) * nq])
                           for h in range(MEM_HEADS)], axis=0)
    ym_ref[0] = (ymt.T * mz_ref[0].astype(f32)).astype(bf16)


def _attn(aq, ak, av, az, mq, mz, mk, mvt, p):
    B, L, _ = aq.shape
    assert L % (QB * BLOCK) == 0 and QB >= 2
    ns = L // (QB * BLOCK)
    M = mk.shape[1]
    tok = lambda w: pl.BlockSpec((1, QB * BLOCK, w), lambda b, j: (b, j, 0))
    seq = lambda w: pl.BlockSpec((1, L, w), lambda b, j: (b, 0, 0))
    bias_shape = (1, 3 * BLOCK, ATTN_HEADS * BLOCK)
    return pl.pallas_call(
        _attn_kernel,
        grid=(B, ns),
        in_specs=[pl.BlockSpec(memory_space=pltpu.SMEM),
                  tok(ATTN_WIDTH), seq(ATTN_KV_WIDTH), seq(ATTN_KV_WIDTH), tok(ATTN_WIDTH),
                  tok(MEM_WIDTH), tok(MEM_WIDTH),
                  pl.BlockSpec((1, M, MEM_WIDTH), lambda b, j: (b, 0, 0)),
                  pl.BlockSpec((1, MEM_WIDTH, M), lambda b, j: (b, 0, 0)),
                  pl.BlockSpec(bias_shape, lambda b, j: (jnp.where(j == 0, 2, 0), 0, 0)),
                  pl.BlockSpec(bias_shape, lambda b, j: (0, 0, 0)),
                  pl.BlockSpec(bias_shape, lambda b, j: (jnp.where(j == ns - 1, 1, 0), 0, 0))],
        out_specs=[tok(ATTN_WIDTH), tok(MEM_WIDTH)],
        out_shape=[jax.ShapeDtypeStruct((B, L, ATTN_WIDTH), bf16), jax.ShapeDtypeStruct((B, L, MEM_WIDTH), bf16)],
        compiler_params=pltpu.CompilerParams(dimension_semantics=("parallel", "arbitrary"),
                                             vmem_limit_bytes=VMEM_LIMIT),
        name="attn",
    )(p['sink'], aq, ak, av, az, mq, mz, mk, mvt, p['bias'], p['bias'], p['bias'])


def _bd2(y, lo):
    z = jnp.zeros_like(y)
    return jnp.concatenate([jnp.where(lo, y, z), jnp.where(lo, z, y)], axis=0)


def _halves(x):
    return x[:, 0:LANES], x[:, LANES:2 * LANES]


def _dn_intra_kernel(q_ref, k_ref, v_ref, gb_ref, tri_ref, esel_ref,
                     m_ref, n_ref, qp_ref, op_ref, gt_ref, bc_ref, *, nc, group):
    C = DN_CHUNK
    W = DN_WIDTH
    ri = lax.broadcasted_iota(jnp.int32, (C, W), 0)
    lj = lax.broadcasted_iota(jnp.int32, (C, W), 1) % C
    eye = ri == lj
    eye_f = jnp.where(eye, 1.0, 0.0)
    masks = ((ri >= lj, ri > lj), (ri <= lj, ri < lj))
    lvl_masks = [((ri // (2 * s)) == (lj // (2 * s))) & ((ri // s) != (lj // s))
                 for s in (2 ** k for k in range(C.bit_length() - 1))]
    lo = lax.broadcasted_iota(jnp.int32, (C, LANES), 1) < HEAD_DIM
    col = lax.broadcasted_iota(jnp.int32, (C, LANES), 1)
    is_bwd = (col >= DN_HEADS) & (col < 2 * DN_HEADS)
    is_beta = col >= 2 * DN_HEADS
    tri = tri_ref[...]

    srcs = []
    for c in range(nc):
        gbv = gb_ref[0, c * C:(c + 1) * C, :]
        gc = _dot_exact_rhs(tri, gbv)
        srcs.append(jnp.where(is_beta, gbv, jnp.where(is_bwd, gc[C - 1:C, :] - gc + gbv, gc)))
    hi, mid, _ = _split3(jnp.concatenate(srcs, axis=0))
    bc_ref[:, 0:2 * W] = _dot(hi, esel_ref[:, 0:2 * W]) + _dot(mid, esel_ref[:, 0:2 * W])
    bc_ref[:, 2 * W:4 * W] = _dot(hi, esel_ref[:, 2 * W:4 * W])

    def pmm(x, y):
        return [_dot(xh, _bd2(yh, lo)) for xh, yh in zip(_halves(x), _halves(y))]

    def cat(parts):
        return jnp.concatenate(parts, axis=1)

    def body(gi, carry):
        cis = [gi * group + t for t in range(group)]
        sls = [pl.ds(pl.multiple_of(ci * C, C), C) for ci in cis]
        qb = [q_ref[0, sl, :] for sl in sls]
        kb = [k_ref[0, sl, :] for sl in sls]
        vb = [v_ref[0, sl, :] for sl in sls]
        gram = [cat([_dot_nt(xh, _bd2(kh, lo)) for xh, kh in zip(_halves(jnp.concatenate([k, q], axis=0)), _halves(k))])
                for k, q in zip(kb, qb)]
        inst = [(t, d) for t in range(group) for d in range(2)]
        gq = [bc_ref[sls[t], d * W:(d + 1) * W] for t, d in inst]
        bq = [bc_ref[sls[t], (2 + d) * W:(3 + d) * W] for t, d in inst]
        glast = [g[C - 1:C, :] if d == 0 else g[0:1, :] for g, (t, d) in zip(gq, inst)]
        r = [jnp.sum(jnp.where(eye, g, 0.0), axis=0, keepdims=True) for g in gq]
        dec = [jnp.where(masks[d][0], jnp.exp(jnp.minimum(g - rr, 0.0)), 0.0) for g, rr, (t, d) in zip(gq, r, inst)]
        nn = [jnp.where(masks[d][1], -(gram[t][0:C] * b * dc), 0.0) for b, dc, (t, d) in zip(bq, dec, inst)]
        tm = [eye_f + jnp.where(lvl_masks[0], a, 0.0) for a in nn]
        for lm in lvl_masks[1:]:
            off = [jnp.where(lm, a, 0.0).astype(bf16) for a in nn]
            left = [cat(pmm(a.astype(bf16), o_)).astype(bf16) for a, o_ in zip(tm, off)]
            tm = [a + cat(pmm(l_, a.astype(bf16))) for a, l_ in zip(tm, left)]
        tb = [a.astype(bf16) for a in tm]
        eg = [jnp.exp(g) for g in gq]
        vbeta = [(vb[t].astype(f32) * b).astype(bf16) for b, (t, d) in zip(bq, inst)]
        kbg = [(kb[t].astype(f32) * b * e).astype(bf16) for b, e, (t, d) in zip(bq, eg, inst)]
        uw = [[_dot(th, jnp.concatenate([_bd2(vh, lo), _bd2(kh, lo)], axis=1)).astype(bf16)
               for th, vh, kh in zip(_halves(a), _halves(v_), _halves(k_))]
              for a, v_, k_ in zip(tb, vbeta, kbg)]
        kd = [(kb[t].astype(f32) * jnp.exp(gl - g)).astype(bf16) for g, gl, (t, d) in zip(gq, glast, inst)]
        fm = [[lax.dot_general(kh, uwp, TN, preferred_element_type=f32) for kh, uwp in zip(_halves(k_), uw_)]
              for k_, uw_ in zip(kd, uw)]
        qkb = [(gram[t][C:2 * C] * dc).astype(bf16) for dc, (t, d) in zip(dec, inst)]
        qo = [[_dot(qh, jnp.concatenate([_bd2(uwp[:, 0:LANES], lo), _bd2(uwp[:, LANES:2 * LANES], lo)], axis=1))
               for qh, uwp in zip(_halves(a), uw_)] for a, uw_ in zip(qkb, uw)]
        for idx, (t, d) in enumerate(inst):
            ci, sl = cis[t], sls[t]
            f0, f1 = fm[idx]
            n_ref[0, d, ci] = cat([jnp.where(lo, f[0:C, 0:LANES], f[C:2 * C, 0:LANES])
                                   for f in (f0, f1)]).astype(bf16)
            m_ref[0, d, ci] = cat([-jnp.where(lo, f[0:C, LANES:2 * LANES], f[C:2 * C, LANES:2 * LANES])
                                   for f in (f0, f1)]).astype(bf16)
            qd = qb[t].astype(f32) * eg[idx]
            qp_ref[0, d, sl, :] = (qd - cat([p_[:, LANES:2 * LANES] for p_ in qo[idx]])).astype(bf16)
            op_ref[0, d, sl, :] = cat([p_[:, 0:LANES] for p_ in qo[idx]]).astype(bf16)
            gt_ref[0, d, ci] = jnp.exp(glast[idx])
        return carry

    lax.fori_loop(0, nc // group, body, 0)


def _dn_intra(dq, dk, dv, gb, p, tb, group=4):
    B, L, _ = dq.shape
    nc = tb // DN_CHUNK
    nchunk = L // DN_CHUNK
    tok = lambda w: pl.BlockSpec((1, tb, w), lambda b, i: (b, i, 0))
    full = lambda a: pl.BlockSpec(a.shape, lambda b, i: (0,) * a.ndim)
    consts = [p['tri'], p['esel']]
    return pl.pallas_call(
        functools.partial(_dn_intra_kernel, nc=nc, group=group),
        grid=(B, L // tb),
        in_specs=[tok(DN_WIDTH), tok(DN_WIDTH), tok(DN_WIDTH), tok(LANES)] + [full(a) for a in consts],
        out_specs=[pl.BlockSpec((1, 2, nc, DN_CHUNK, DN_WIDTH), lambda b, i: (b, 0, i, 0, 0)),
                   pl.BlockSpec((1, 2, nc, DN_CHUNK, DN_WIDTH), lambda b, i: (b, 0, i, 0, 0)),
                   pl.BlockSpec((1, 2, tb, DN_WIDTH), lambda b, i: (b, 0, i, 0)),
                   pl.BlockSpec((1, 2, tb, DN_WIDTH), lambda b, i: (b, 0, i, 0)),
                   pl.BlockSpec((1, 2, nc, 1, DN_WIDTH), lambda b, i: (b, 0, i, 0, 0))],
        out_shape=[jax.ShapeDtypeStruct((B, 2, nchunk, DN_CHUNK, DN_WIDTH), bf16),
                   jax.ShapeDtypeStruct((B, 2, nchunk, DN_CHUNK, DN_WIDTH), bf16),
                   jax.ShapeDtypeStruct((B, 2, L, DN_WIDTH), bf16),
                   jax.ShapeDtypeStruct((B, 2, L, DN_WIDTH), bf16),
                   jax.ShapeDtypeStruct((B, 2, nchunk, 1, DN_WIDTH), f32)],
        scratch_shapes=[pltpu.VMEM((tb, 4 * DN_WIDTH), f32)],
        compiler_params=pltpu.CompilerParams(dimension_semantics=("parallel", "parallel"),
                                             vmem_limit_bytes=VMEM_LIMIT),
        name="dn_intra",
    )(dq, dk, dv, gb, *consts)


def _dn_scan_kernel(mf_ref, nf_ref, qf_ref, of_ref, gf_ref, mb_ref, nb_ref, qb_ref, ob_ref, gbk_ref,
                    yf_ref, yb_ref, s_ref, *, nc, nbat):
    C = DN_CHUNK

    @pl.when(pl.program_id(1) == 0)
    def _():
        s_ref[...] = jnp.zeros_like(s_ref)

    lo = lax.broadcasted_iota(jnp.int32, (C, LANES), 1) < HEAD_DIM
    dirs = ((mf_ref, nf_ref, qf_ref, of_ref, gf_ref, yf_ref), (mb_ref, nb_ref, qb_ref, ob_ref, gbk_ref, yb_ref))

    def step(j, carry):
        cis = (j, nc - 1 - j)
        sls = [pl.ds(pl.multiple_of(ci * C, C), C) for ci in cis]
        seqs = [(b, d) for b in range(nbat) for d in range(2)]
        inst = [(b, d, h) for b, d in seqs for h in range(2)]
        hs = lambda a, h: a[:, h * LANES:(h + 1) * LANES]
        m = {(b, d): dirs[d][0][b, 0, cis[d]] for b, d in seqs}
        q = {(b, d): dirs[d][2][b, 0, sls[d], :] for b, d in seqs}
        s = [s_ref[b, d, h] for b, d, h in inst]
        lhs = [jnp.concatenate([_bd2(hs(m[b, d], h), lo), hs(q[b, d], h)], axis=0) for b, d, h in inst]
        rr = [_dot(a, b_.astype(bf16)) for a, b_ in zip(lhs, s)]
        for idx, (b, d, h) in enumerate(inst):
            gt = dirs[d][4][b, 0, cis[d]]
            nn = dirs[d][1][b, 0, cis[d]]
            s_ref[b, d, h] = s[idx] * hs(gt, h) + rr[idx][0:2 * C] + _bd2(hs(nn, h), lo).astype(f32)
        for k, (b, d) in enumerate(seqs):
            y = jnp.concatenate([rr[2 * k][2 * C:3 * C], rr[2 * k + 1][2 * C:3 * C]], axis=1)
            dirs[d][5][b, sls[d], :] = (y + dirs[d][3][b, 0, sls[d], :].astype(f32)).astype(bf16)
        return carry

    lax.fori_loop(0, nc, step, 0)


def _dn_scan(m, n, qp, op, gt, tb, nbat):
    B, _, nchunk, C, W = m.shape
    L = nchunk * C
    nc = tb // C
    ns = L // tb
    assert B % nbat == 0
    fwd5 = lambda last: pl.BlockSpec((nbat, 1, nc, last, W), lambda b, i: (b, 0, i, 0, 0))
    bwd5 = lambda last: pl.BlockSpec((nbat, 1, nc, last, W), lambda b, i: (b, 1, ns - 1 - i, 0, 0))
    fwd4 = pl.BlockSpec((nbat, 1, tb, W), lambda b, i: (b, 0, i, 0))
    bwd4 = pl.BlockSpec((nbat, 1, tb, W), lambda b, i: (b, 1, ns - 1 - i, 0))
    return pl.pallas_call(
        functools.partial(_dn_scan_kernel, nc=nc, nbat=nbat),
        grid=(B // nbat, ns),
        in_specs=[fwd5(C), fwd5(C), fwd4, fwd4, fwd5(1), bwd5(C), bwd5(C), bwd4, bwd4, bwd5(1)],
        out_specs=[pl.BlockSpec((nbat, tb, W), lambda b, i: (b, i, 0)),
                   pl.BlockSpec((nbat, tb, W), lambda b, i: (b, ns - 1 - i, 0))],
        out_shape=[jax.ShapeDtypeStruct((B, L, W), bf16), jax.ShapeDtypeStruct((B, L, W), bf16)],
        scratch_shapes=[pltpu.VMEM((nbat, 2, 2, 2 * C, LANES), f32)],
        compiler_params=pltpu.CompilerParams(dimension_semantics=("parallel", "arbitrary"),
                                             vmem_limit_bytes=VMEM_LIMIT),
        name="dn_scan",
    )(m, n, qp, op, gt, m, n, qp, op, gt)


def _outproj_kernel(xa_ref, ya_ref, ym_ref, yf_ref, yb_ref, dz_ref, lg_ref, lb_ref,
                    ng_ref, wa_ref, wd_ref, wm_ref, ones_ref, o_ref):
    o = yf_ref[0].astype(f32) + yb_ref[0].astype(f32)
    ms = _dot((o * o).astype(bf16), ones_ref[...]) * (1.0 / HEAD_DIM)
    yd = o * lax.rsqrt(ms + RMS_EPS) * ng_ref[...] * dz_ref[0].astype(f32)
    y = _dot(ya_ref[0], wa_ref[...]) + _dot(yd.astype(bf16), wd_ref[...]) + _dot(ym_ref[0], wm_ref[...])
    o_ref[0] = _layer_norm(xa_ref[0] + y, lg_ref[...], lb_ref[...])


def _outproj(xa, ya, ym, yf, yb, dz, p, tm):
    B, L, _ = xa.shape
    tok = lambda w: pl.BlockSpec((1, tm, w), lambda b, i: (b, i, 0))
    full = lambda a: pl.BlockSpec(a.shape, lambda b, i: (0,) * a.ndim)
    consts = [p['ln_g'], p['ln_b'], p['norm_g'], p['wo_attn'], p['wo_dn'], p['wo_mem'], p['ones_bd']]
    return pl.pallas_call(
        _outproj_kernel,
        grid=(B, L // tm),
        in_specs=[tok(D_MODEL), tok(ATTN_WIDTH), tok(MEM_WIDTH), tok(DN_WIDTH), tok(DN_WIDTH), tok(DN_WIDTH)]
                 + [full(a) for a in consts],
        out_specs=tok(D_MODEL),
        out_shape=jax.ShapeDtypeStruct((B, L, D_MODEL), f32),
        compiler_params=pltpu.CompilerParams(dimension_semantics=("parallel", "parallel"),
                                             vmem_limit_bytes=VMEM_LIMIT),
        name="outproj",
    )(xa, ya, ym, yf, yb, dz, *consts)


def _t5_bucket(rel):
    nb = REL_BUCKETS // 2
    max_exact = nb // 2
    n = jnp.abs(rel)
    large = max_exact + (jnp.log(jnp.maximum(n, 1).astype(f32) / max_exact)
                         / math.log(REL_MAX_DIST / max_exact) * (nb - max_exact)).astype(jnp.int32)
    large = jnp.minimum(large, nb - 1)
    return jnp.where(rel > 0, nb, 0) + jnp.where(n < max_exact, n, large)


def _prepare(ln_in_g, ln_in_b, rel_bias, w_in, attn_sink, dn_conv, dn_A_log, dn_dt_bias, dn_norm_g,
             w_mem_kv, w_out, ln_g, ln_b):
    offs = np.cumsum([0, ATTN_WIDTH, ATTN_KV_WIDTH, ATTN_KV_WIDTH, ATTN_WIDTH, DN_WIDTH, DN_WIDTH, DN_WIDTH,
                      DN_WIDTH, 2 * DN_HEADS, 2 * DN_HEADS, MEM_WIDTH, MEM_WIDTH])
    (w_aq, w_ak, w_av, w_az, w_dq, w_dk, w_dv, w_dz, w_da, w_db, w_mq, w_mz) = [
        w_in[:, int(offs[j]):int(offs[j + 1])] for j in range(12)]
    order = [kv * ATTN_GROUP + c for c in range(ATTN_GROUP) for kv in range(ATTN_KV_HEADS)]
    head_cols = lambda w: jnp.concatenate([w[:, h * HEAD_DIM:(h + 1) * HEAD_DIM] for h in order], axis=1)
    scale = HEAD_DIM ** -0.5 * LOG2E
    p = {}
    p['w_attn'] = jnp.concatenate([head_cols(w_aq) * scale, w_ak, w_av, head_cols(w_az)], axis=1).astype(bf16)
    p['w_mem'] = jnp.concatenate([w_mq * scale, w_mz], axis=1).astype(bf16)
    p['w_dn'] = jnp.concatenate([w_dq, w_dk, w_dv, w_dz], axis=1).astype(bf16)
    p['w_ab'] = jnp.pad(jnp.concatenate([w_da, w_db], axis=1), ((0, 0), (0, LANES - 4 * DN_HEADS))).astype(bf16)
    p['conv'] = jnp.pad(dn_conv, ((0, SUBLANES - DN_CONV), (0, 0)))
    p['alog'] = jnp.pad(dn_A_log.reshape(1, 2 * DN_HEADS), ((0, 0), (0, LANES - 2 * DN_HEADS)))
    p['dtb'] = jnp.pad(dn_dt_bias.reshape(1, 2 * DN_HEADS), ((0, 0), (0, LANES - 2 * DN_HEADS)))
    p['ln_in_g'] = ln_in_g.reshape(1, D_MODEL)
    p['ln_in_b'] = ln_in_b.reshape(1, D_MODEL)
    p['ln_g'] = ln_g.reshape(1, D_MODEL)
    p['ln_b'] = ln_b.reshape(1, D_MODEL)
    p['norm_g'] = jnp.tile(dn_norm_g.reshape(1, HEAD_DIM), (1, DN_HEADS))
    p['w_memk'] = w_mem_kv[:, 0:MEM_WIDTH].astype(bf16)
    p['w_memvt'] = w_mem_kv[:, MEM_WIDTH:2 * MEM_WIDTH].T.astype(bf16)
    p['wo_attn'] = jnp.concatenate([w_out[h * HEAD_DIM:(h + 1) * HEAD_DIM] for h in order], axis=0).astype(bf16)
    p['wo_dn'] = w_out[ATTN_WIDTH:ATTN_WIDTH + DN_WIDTH].astype(bf16)
    p['wo_mem'] = w_out[ATTN_WIDTH + DN_WIDTH:].astype(bf16)
    p['sink'] = jnp.stack([attn_sink[h] for h in order]) * LOG2E

    blk = np.arange(DN_WIDTH) // HEAD_DIM
    bd = (blk[:, None] == blk[None, :])
    p['ones_bd'] = jnp.asarray(bd, bf16)
    p['tri'] = jnp.asarray(np.tril(np.ones((DN_CHUNK, DN_CHUNK))), bf16)
    esel = np.zeros((LANES, 4 * DN_WIDTH), np.float32)
    for c in range(4 * DN_HEADS):
        esel[c, c * HEAD_DIM:(c + 1) * HEAD_DIM] = 1.0
    p['esel'] = jnp.asarray(esel, bf16)

    t = jnp.arange(BLOCK)[None, :]
    s = jnp.arange(3 * BLOCK)[:, None]
    rel = s - BLOCK - t
    onehot = jax.nn.one_hot(_t5_bucket(rel), REL_BUCKETS, dtype=f32)
    rb = jnp.stack([rel_bias[:, h] for h in order], axis=0)
    bias = jnp.einsum('stk,hk->sht', onehot, rb, precision=lax.Precision.HIGHEST)
    bias = jnp.where((jnp.abs(rel) <= WINDOW)[:, None, :], bias * LOG2E, NEG)
    bias = bias.reshape(3 * BLOCK, ATTN_HEADS * BLOCK)
    no_prev = jnp.where(s < BLOCK, NEG, 0.0)
    no_next = jnp.where(s >= 2 * BLOCK, NEG, 0.0)
    p['bias'] = jnp.stack([bias, bias + no_next, bias + no_prev])
    return p


SCAN_SEQS = 4


def _trunk(x, mem, p, tm_in=1024, tm=1024, tb_intra=1024, tb_scan=1024):
    B, L, _ = x.shape
    tm_in = min(tm_in, L)
    tm = min(tm, L)
    tb_intra = min(tb_intra, L)
    tb_scan = min(tb_scan, L)
    nbat = math.gcd(B, SCAN_SEQS)
    aq, ak, av, az, mq, mz, dq, dk, dv, dz, gb, xa = _inproj(x, p, tm_in)
    mk, mvt = _memkv(mem, p)
    ya, ym = _attn(aq, ak, av, az, mq, mz, mk, mvt, p)
    m, n, qp, op, gt = _dn_intra(dq, dk, dv, gb, p, tb_intra)
    yf, yb = _dn_scan(m, n, qp, op, gt, tb_scan, nbat)
    return _outproj(xa, ya, ym, yf, yb, dz, p, tm)


def kernel(x_prompt, x_sample, mem_prompt, mem_sample, ln_in_g, ln_in_b, rel_bias, w_in, attn_sink, dn_conv,
           dn_A_log, dn_dt_bias, dn_norm_g, w_mem_kv, w_out, ln_g, ln_b):
    p = _prepare(ln_in_g, ln_in_b, rel_bias, w_in[0], attn_sink[0], dn_conv[0], dn_A_log[0], dn_dt_bias[0],
                 dn_norm_g[0], w_mem_kv[0], w_out[0], ln_g[0], ln_b[0])
    return (_trunk(x_prompt, mem_prompt, p), _trunk(x_sample, mem_sample, p))
```

```python
import functools
import math

import jax
import jax.numpy as jnp
import numpy as np
from jax import lax
from jax.experimental import pallas as pl
from jax.experimental.pallas import tpu as pltpu

f32 = jnp.float32
bf16 = jnp.bfloat16

D_MODEL = 1024
HEAD_DIM = 64
ATTN_HEADS = 8
ATTN_KV_HEADS = 2
ATTN_GROUP = ATTN_HEADS // ATTN_KV_HEADS
ATTN_WIDTH = ATTN_HEADS * HEAD_DIM
ATTN_KV_WIDTH = ATTN_KV_HEADS * HEAD_DIM
WINDOW = 128
BLOCK = 128
REL_BUCKETS = 32
REL_MAX_DIST = 128
DN_HEADS = 4
DN_WIDTH = DN_HEADS * HEAD_DIM
DN_CONV = 5
DN_CONV_DIM = 3 * DN_WIDTH
DN_CHUNK = 64
MEM_HEADS = 4
MEM_WIDTH = MEM_HEADS * HEAD_DIM
DEPTH = 1
DEEPNORM_ALPHA = (2 * DEPTH) ** 0.25
LN_EPS = 1e-5
RMS_EPS = 1e-6
NEG = -1e30
LOG2E = 1.4426950408889634

LANES = 128
SUBLANES = 8
HALO = SUBLANES
VMEM_LIMIT = 56 * 1024 * 1024

NT = (((1,), (1,)), ((), ()))
TN = (((0,), (0,)), ((), ()))


def _dot(a, b):
    return jnp.dot(a, b, preferred_element_type=f32)


def _dot_nt(a, b):
    return lax.dot_general(a, b, NT, preferred_element_type=f32)


def _split3(x):
    hi = x.astype(bf16)
    r1 = x - hi.astype(f32)
    mid = r1.astype(bf16)
    lo = (r1 - mid.astype(f32)).astype(bf16)
    return hi, mid, lo


def _dot_exact_rhs(sel, x):
    hi, mid, lo = _split3(x)
    return _dot(sel, hi) + _dot(sel, mid) + _dot(sel, lo)


def _layer_norm(x, g, b):
    mu = jnp.mean(x, axis=-1, keepdims=True)
    xc = x - mu
    var = jnp.mean(xc * xc, axis=-1, keepdims=True)
    return xc * lax.rsqrt(var + LN_EPS) * g + b


def _silu(x):
    return x / (1.0 + jnp.exp(-x))


def _inproj_kernel(x_ref, xp_ref, xn_ref, lng_ref, lnb_ref, wattn_ref, wmem_ref, wdn_ref, wab_ref,
                   conv_ref, alog_ref, dtb_ref, ones_ref,
                   aq_ref, ak_ref, av_ref, az_ref, mq_ref, mz_ref, dq_ref, dk_ref, dv_ref, dz_ref, gb_ref, xa_ref,
                   hbuf, *, tm, nsub):
    i = pl.program_id(1)
    n = pl.num_programs(1)
    g = lng_ref[...]
    b = lnb_ref[...]
    ts = tm // nsub
    rows = ts + 2 * HALO
    half = DN_CONV // 2
    ones = ones_ref[...]

    def l2n(t, scale):
        ss = _dot((t * t).astype(bf16), ones)
        return t * (lax.rsqrt(ss + 1e-6) * scale)

    def sub_tile(k):
        r0 = k * ts
        rs = slice(r0, r0 + ts)
        st = {}

        def ln():
            xln = _layer_norm(x_ref[0, rs, :], g, b)
            xa_ref[0, rs, :] = DEEPNORM_ALPHA * xln
            st['xb'] = xln.astype(bf16)
            prev = xp_ref[0] if k == 0 else x_ref[0, r0 - HALO:r0, :]
            nxt = xn_ref[0] if k == nsub - 1 else x_ref[0, r0 + ts:r0 + ts + HALO, :]
            st['xsb'] = jnp.concatenate([_layer_norm(prev, g, b), xln, _layer_norm(nxt, g, b)], axis=0).astype(bf16)

        def proj_qkv():
            hbuf[k] = _dot(st['xsb'], wdn_ref[:, 0:DN_CONV_DIM])
            if k == 0:
                hbuf[k, 0:HALO, :] = jnp.where(i > 0, hbuf[k, 0:HALO, :], 0.0)
            if k == nsub - 1:
                hbuf[k, HALO + ts:rows, :] = jnp.where(i < n - 1, hbuf[k, HALO + ts:rows, :], 0.0)
            st['ab'] = _dot(st['xb'], wab_ref[...])

        def proj_aq():
            aq_ref[0, rs, :] = _dot(st['xb'], wattn_ref[:, 0:512]).astype(bf16)

        def conv():
            hfull = hbuf[k]
            c = conv_ref[half:half + 1, :] * hfull[HALO:HALO + ts, :]
            for j in range(DN_CONV):
                if j != half:
                    c = c + conv_ref[j:j + 1, :] * pltpu.roll(hfull, (half - j) % rows, axis=0)[HALO:HALO + ts, :]
            st['c'] = c

        def proj_kv_az():
            kv = _dot(st['xb'], wattn_ref[:, 512:768]).astype(bf16)
            ak_ref[0, rs, :] = kv[:, 0:ATTN_KV_WIDTH]
            av_ref[0, rs, :] = kv[:, ATTN_KV_WIDTH:2 * ATTN_KV_WIDTH]
            az_ref[0, rs, :] = _silu(_dot(st['xb'], wattn_ref[:, 768:1280])).astype(bf16)

        def act_q():
            st['c'] = _silu(st['c'])
            dq_ref[0, rs, :] = l2n(st['c'][:, 0:DN_WIDTH], HEAD_DIM ** -0.5).astype(bf16)

        def proj_dz():
            dz_ref[0, rs, :] = _dot(st['xb'], wdn_ref[:, DN_CONV_DIM:DN_CONV_DIM + DN_WIDTH]).astype(bf16)

        def act_k():
            dk_ref[0, rs, :] = l2n(st['c'][:, DN_WIDTH:2 * DN_WIDTH], 1.0).astype(bf16)

        def proj_mz():
            mz_ref[0, rs, :] = _silu(_dot(st['xb'], wmem_ref[:, 256:512])).astype(bf16)

        def act_v_gb():
            dv_ref[0, rs, :] = st['c'][:, 2 * DN_WIDTH:3 * DN_WIDTH].astype(bf16)
            ab = st['ab']
            z = ab + dtb_ref[...]
            sp = jnp.maximum(z, 0.0) + jnp.log1p(jnp.exp(-jnp.abs(z)))
            gdec = -jnp.exp(alog_ref[...]) * sp
            beta = 1.0 / (1.0 + jnp.exp(-ab))
            lane = lax.broadcasted_iota(jnp.int32, ab.shape, 1)
            gb_ref[0, rs, :] = jnp.where(lane < 2 * DN_HEADS, gdec, beta)

        def proj_mq():
            mq_ref[0, rs, :] = _dot(st['xb'], wmem_ref[:, 0:256]).astype(bf16)

        return [ln, proj_qkv, proj_aq, conv, proj_kv_az, act_q, proj_dz, act_k, proj_mz, act_v_gb, proj_mq]

    subs = [sub_tile(k) for k in range(nsub)]
    nst = len(subs[0])
    for s in range(nst + SUB_LAG * (nsub - 1)):
        for k in range(nsub):
            if 0 <= s - SUB_LAG * k < nst:
                subs[k][s - SUB_LAG * k]()


SUB_TILE = 512
SUB_LAG = 2


def _inproj(x, p, tm):
    B, L, _ = x.shape
    nsub = max(tm // SUB_TILE, 1)
    nt = L // tm
    r8 = tm // HALO
    full = lambda a: pl.BlockSpec(a.shape, lambda b, i: (0,) * a.ndim)
    consts = [p['ln_in_g'], p['ln_in_b'], p['w_attn'], p['w_mem'], p['w_dn'], p['w_ab'],
              p['conv'], p['alog'], p['dtb'], p['ones_bd']]
    tok = lambda w: pl.BlockSpec((1, tm, w), lambda b, i: (b, i, 0))
    widths = [512, 128, 128, 512, 256, 256, 256, 256, 256, 256]
    out_shape = ([jax.ShapeDtypeStruct((B, L, w), bf16) for w in widths]
                 + [jax.ShapeDtypeStruct((B, L, LANES), f32), jax.ShapeDtypeStruct((B, L, D_MODEL), f32)])
    return pl.pallas_call(
        functools.partial(_inproj_kernel, tm=tm, nsub=nsub),
        grid=(B, nt),
        in_specs=[pl.BlockSpec((1, tm, D_MODEL), lambda b, i: (b, i, 0)),
                  pl.BlockSpec((1, HALO, D_MODEL), lambda b, i: (b, jnp.maximum(i * r8 - 1, 0), 0)),
                  pl.BlockSpec((1, HALO, D_MODEL), lambda b, i: (b, jnp.minimum((i + 1) * r8, L // HALO - 1), 0)),
                  ] + [full(a) for a in consts],
        out_specs=[tok(w) for w in widths] + [tok(LANES), tok(D_MODEL)],
        out_shape=out_shape,
        scratch_shapes=[pltpu.VMEM((nsub, tm // nsub + 2 * HALO, DN_CONV_DIM), f32)],
        compiler_params=pltpu.CompilerParams(dimension_semantics=("parallel", "parallel"),
                                             vmem_limit_bytes=VMEM_LIMIT),
        name="inproj",
    )(x, x, x, *consts)


def _memkv_kernel(m_ref, lng_ref, lnb_ref, wk_ref, wvt_ref, k_ref, vt_ref):
    mb = _layer_norm(m_ref[0], lng_ref[...], lnb_ref[...]).astype(bf16)
    k_ref[0] = _dot(mb, wk_ref[...]).astype(bf16)
    vt_ref[0] = _dot_nt(wvt_ref[...], mb).astype(bf16)


def _memkv(mem, p):
    B, M, _ = mem.shape
    return pl.pallas_call(
        _memkv_kernel,
        grid=(B,),
        in_specs=[pl.BlockSpec((1, M, D_MODEL), lambda b: (b, 0, 0)),
                  pl.BlockSpec((1, D_MODEL), lambda b: (0, 0)),
                  pl.BlockSpec((1, D_MODEL), lambda b: (0, 0)),
                  pl.BlockSpec((D_MODEL, MEM_WIDTH), lambda b: (0, 0)),
                  pl.BlockSpec((MEM_WIDTH, D_MODEL), lambda b: (0, 0))],
        out_specs=[pl.BlockSpec((1, M, MEM_WIDTH), lambda b: (b, 0, 0)),
                   pl.BlockSpec((1, MEM_WIDTH, M), lambda b: (b, 0, 0))],
        out_shape=[jax.ShapeDtypeStruct((B, M, MEM_WIDTH), bf16), jax.ShapeDtypeStruct((B, MEM_WIDTH, M), bf16)],
        compiler_params=pltpu.CompilerParams(dimension_semantics=("parallel",), vmem_limit_bytes=VMEM_LIMIT),
        name="memkv",
    )(mem, p['ln_in_g'], p['ln_in_b'], p['w_memk'], p['w_memvt'])


def _softmax_cols(s, extra=None):
    m = jnp.max(s, axis=0, keepdims=True)
    if extra is not None:
        m = jnp.maximum(m, extra)
    e = jnp.exp2(s - m)
    return e.astype(bf16), (None if extra is None else jnp.exp2(extra - m))


QB = 8


def _attn_kernel(sink_ref, aq_ref, ak_ref, av_ref, az_ref, mq_ref, mz_ref, mk_ref, mvt_ref,
                 bias_first_ref, bias_mid_ref, bias_last_ref, ya_ref, ym_ref):
    j = pl.program_id(1)
    nb = QB * pl.num_programs(1)
    bias_refs = [bias_first_ref] + [bias_mid_ref] * (QB - 2) + [bias_last_ref]
    lane = lax.broadcasted_iota(jnp.int32, (BLOCK, LANES), 1)
    low = lane < HEAD_DIM
    zero = jnp.zeros((BLOCK, LANES), bf16)

    def rows(ref, blk):
        return ref[0, pl.ds(pl.multiple_of(blk * BLOCK, BLOCK), BLOCK), :]

    def band(ref, i):
        return jnp.concatenate([rows(ref, jnp.maximum(i - 1, 0)), rows(ref, i), rows(ref, jnp.minimum(i + 1, nb - 1))],
                               axis=0)

    def stack_q(q):
        parts = []
        for c in range(ATTN_GROUP):
            t = q[:, c * LANES:(c + 1) * LANES]
            parts.append(jnp.where(low, t, zero))
            parts.append(jnp.where(low, zero, t))
        return jnp.concatenate(parts, axis=0)

    blks = [j * QB + t for t in range(QB)]
    kcat = [band(ak_ref, i) for i in blks]
    vcat = [band(av_ref, i) for i in blks]
    qs = [stack_q(aq_ref[0, t * BLOCK:(t + 1) * BLOCK, :]) for t in range(QB)]
    mq = mq_ref[0]
    nq = QB * BLOCK
    lane_m = lax.broadcasted_iota(jnp.int32, (nq, MEM_WIDTH), 1) // HEAD_DIM
    zm = jnp.zeros((nq, MEM_WIDTH), bf16)
    mqs = jnp.concatenate([jnp.where(lane_m == h, mq, zm) for h in range(MEM_HEADS)], axis=0)

    st = [_dot_nt(k, q) + b[0] for k, q, b in zip(kcat, qs, bias_refs)]
    smt = _dot_nt(mk_ref[0], mqs)
    ones_v = jnp.ones((2 * SUBLANES, 3 * BLOCK), bf16)
    vt = [jnp.concatenate([v.astype(f32).T.astype(bf16), ones_v], axis=0) for v in vcat]
    mvt = jnp.concatenate([mvt_ref[0], jnp.ones((2 * SUBLANES, mk_ref.shape[1]), bf16)], axis=0)

    soft = [[_softmax_cols(s[:, h * BLOCK:(h + 1) * BLOCK], sink_ref[h]) for h in range(ATTN_HEADS)] for s in st]
    softm = [_softmax_cols(smt[:, h * nq:(h + 1) * nq]) for h in range(MEM_HEADS)]

    ot = [_dot(v, jnp.concatenate([p_ for p_, _ in sf], axis=1)) for v, sf in zip(vt, soft)]
    omt = _dot(mvt, jnp.concatenate([p_ for p_, _ in softm], axis=1))

    half = HEAD_DIM
    vrows = ATTN_KV_WIDTH
    for t in range(QB):
        az = az_ref[0, t * BLOCK:(t + 1) * BLOCK, :].astype(f32)
        rinv = [1.0 / (ot[t][vrows:vrows + 1, h * BLOCK:(h + 1) * BLOCK] + soft[t][h][1]) for h in range(ATTN_HEADS)]
        for c in range(ATTN_GROUP):
            o0 = ot[t][0:half, (2 * c) * BLOCK:(2 * c + 1) * BLOCK] * rinv[2 * c]
            o1 = ot[t][half:2 * half, (2 * c + 1) * BLOCK:(2 * c + 2) * BLOCK] * rinv[2 * c + 1]
            y = jnp.concatenate([o0, o1], axis=0).T * az[:, c * LANES:(c + 1) * LANES]
            ya_ref[0, t * BLOCK:(t + 1) * BLOCK, c * LANES:(c + 1) * LANES] = y.astype(bf16)
    ymt = jnp.concatenate([omt[h * HEAD_DIM:(h + 1) * HEAD_DIM, h * nq:(h + 1) * nq]
                           * (1.0 / omt[MEM_WIDTH:MEM_WIDTH + 1, h * nq:(h + 1) * nq])
                           for h in range(MEM_HEADS)], axis=0)
    ym_ref[0] = (ymt.T * mz_ref[0].astype(f32)).astype(bf16)


def _attn(aq, ak, av, az, mq, mz, mk, mvt, p):
    B, L, _ = aq.shape
    assert L % (QB * BLOCK) == 0 and QB >= 2
    ns = L // (QB * BLOCK)
    M = mk.shape[1]
    tok = lambda w: pl.BlockSpec((1, QB * BLOCK, w), lambda b, j: (b, j, 0))
    seq = lambda w: pl.BlockSpec((1, L, w), lambda b, j: (b, 0, 0))
    bias_shape = (1, 3 * BLOCK, ATTN_HEADS * BLOCK)
    return pl.pallas_call(
        _attn_kernel,
        grid=(B, ns),
        in_specs=[pl.BlockSpec(memory_space=pltpu.SMEM),
                  tok(ATTN_WIDTH), seq(ATTN_KV_WIDTH), seq(ATTN_KV_WIDTH), tok(ATTN_WIDTH),
                  tok(MEM_WIDTH), tok(MEM_WIDTH),
                  pl.BlockSpec((1, M, MEM_WIDTH), lambda b, j: (b, 0, 0)),
                  pl.BlockSpec((1, MEM_WIDTH, M), lambda b, j: (b, 0, 0)),
                  pl.BlockSpec(bias_shape, lambda b, j: (jnp.where(j == 0, 2, 0), 0, 0)),
                  pl.BlockSpec(bias_shape, lambda b, j: (0, 0, 0)),
                  pl.BlockSpec(bias_shape, lambda b, j: (jnp.where(j == ns - 1, 1, 0), 0, 0))],
        out_specs=[tok(ATTN_WIDTH), tok(MEM_WIDTH)],
        out_shape=[jax.ShapeDtypeStruct((B, L, ATTN_WIDTH), bf16), jax.ShapeDtypeStruct((B, L, MEM_WIDTH), bf16)],
        compiler_params=pltpu.CompilerParams(dimension_semantics=("parallel", "arbitrary"),
                                             vmem_limit_bytes=VMEM_LIMIT),
        name="attn",
    )(p['sink'], aq, ak, av, az, mq, mz, mk, mvt, p['bias'], p['bias'], p['bias'])


def _bd2(y, lo):
    z = jnp.zeros_like(y)
    return jnp.concatenate([jnp.where(lo, y, z), jnp.where(lo, z, y)], axis=0)


def _halves(x):
    return x[:, 0:LANES], x[:, LANES:2 * LANES]


def _dn_intra_kernel(q_ref, k_ref, v_ref, gb_ref, tri_ref, esel_ref,
                     m_ref, n_ref, qp_ref, op_ref, gt_ref, bc_ref, *, nc, group):
    C = DN_CHUNK
    W = DN_WIDTH
    ri = lax.broadcasted_iota(jnp.int32, (C, W), 0)
    lj = lax.broadcasted_iota(jnp.int32, (C, W), 1) % C
    eye = ri == lj
    eye_f = jnp.where(eye, 1.0, 0.0)
    masks = ((ri >= lj, ri > lj), (ri <= lj, ri < lj))
    lvl_masks = [((ri // (2 * s)) == (lj // (2 * s))) & ((ri // s) != (lj // s))
                 for s in (2 ** k for k in range(C.bit_length() - 1))]
    lo = lax.broadcasted_iota(jnp.int32, (C, LANES), 1) < HEAD_DIM
    col = lax.broadcasted_iota(jnp.int32, (C, LANES), 1)
    is_bwd = (col >= DN_HEADS) & (col < 2 * DN_HEADS)
    is_beta = col >= 2 * DN_HEADS
    tri = tri_ref[...]

    srcs = []
    for c in range(nc):
        gbv = gb_ref[0, c * C:(c + 1) * C, :]
        gc = _dot_exact_rhs(tri, gbv)
        srcs.append(jnp.where(is_beta, gbv, jnp.where(is_bwd, gc[C - 1:C, :] - gc + gbv, gc)))
    hi, mid, _ = _split3(jnp.concatenate(srcs, axis=0))
    bc_ref[:, 0:2 * W] = _dot(hi, esel_ref[:, 0:2 * W]) + _dot(mid, esel_ref[:, 0:2 * W])
    bc_ref[:, 2 * W:4 * W] = _dot(hi, esel_ref[:, 2 * W:4 * W])

    def pmm(x, y):
        return [_dot(xh, _bd2(yh, lo)) for xh, yh in zip(_halves(x), _halves(y))]

    def cat(parts):
        return jnp.concatenate(parts, axis=1)

    def body(gi, carry):
        cis = [gi * group + t for t in range(group)]
        sls = [pl.ds(pl.multiple_of(ci * C, C), C) for ci in cis]
        qb = [q_ref[0, sl, :] for sl in sls]
        kb = [k_ref[0, sl, :] for sl in sls]
        vb = [v_ref[0, sl, :] for sl in sls]
        gram = [cat([_dot_nt(xh, _bd2(kh, lo)) for xh, kh in zip(_halves(jnp.concatenate([k, q], axis=0)), _halves(k))])
                for k, q in zip(kb, qb)]
        inst = [(t, d) for t in range(group) for d in range(2)]
        gq = [bc_ref[sls[t], d * W:(d + 1) * W] for t, d in inst]
        bq = [bc_ref[sls[t], (2 + d) * W:(3 + d) * W] for t, d in inst]
        glast = [g[C - 1:C, :] if d == 0 else g[0:1, :] for g, (t, d) in zip(gq, inst)]
        r = [jnp.sum(jnp.where(eye, g, 0.0), axis=0, keepdims=True) for g in gq]
        dec = [jnp.where(masks[d][0], jnp.exp(jnp.minimum(g - rr, 0.0)), 0.0) for g, rr, (t, d) in zip(gq, r, inst)]
        nn = [jnp.where(masks[d][1], -(gram[t][0:C] * b * dc), 0.0) for b, dc, (t, d) in zip(bq, dec, inst)]
        tm = [eye_f + jnp.where(lvl_masks[0], a, 0.0) for a in nn]
        for lm in lvl_masks[1:]:
            off = [jnp.where(lm, a, 0.0).astype(bf16) for a in nn]
            left = [cat(pmm(a.astype(bf16), o_)).astype(bf16) for a, o_ in zip(tm, off)]
            tm = [a + cat(pmm(l_, a.astype(bf16))) for a, l_ in zip(tm, left)]
        tb = [a.astype(bf16) for a in tm]
        eg = [jnp.exp(g) for g in gq]
        vbeta = [(vb[t].astype(f32) * b).astype(bf16) for b, (t, d) in zip(bq, inst)]
        kbg = [(kb[t].astype(f32) * b * e).astype(bf16) for b, e, (t, d) in zip(bq, eg, inst)]
        uw = [[_dot(th, jnp.concatenate([_bd2(vh, lo), _bd2(kh, lo)], axis=1)).astype(bf16)
               for th, vh, kh in zip(_halves(a), _halves(v_), _halves(k_))]
              for a, v_, k_ in zip(tb, vbeta, kbg)]
        kd = [(kb[t].astype(f32) * jnp.exp(gl - g)).astype(bf16) for g, gl, (t, d) in zip(gq, glast, inst)]
        fm = [[lax.dot_general(kh, uwp, TN, preferred_element_type=f32) for kh, uwp in zip(_halves(k_), uw_)]
              for k_, uw_ in zip(kd, uw)]
        qkb = [(gram[t][C:2 * C] * dc).astype(bf16) for dc, (t, d) in zip(dec, inst)]
        qo = [[_dot(qh, jnp.concatenate([_bd2(uwp[:, 0:LANES], lo), _bd2(uwp[:, LANES:2 * LANES], lo)], axis=1))
               for qh, uwp in zip(_halves(a), uw_)] for a, uw_ in zip(qkb, uw)]
        for idx, (t, d) in enumerate(inst):
            ci, sl = cis[t], sls[t]
            f0, f1 = fm[idx]
            n_ref[0, d, ci] = cat([jnp.where(lo, f[0:C, 0:LANES], f[C:2 * C, 0:LANES])
                                   for f in (f0, f1)]).astype(bf16)
            m_ref[0, d, ci] = cat([-jnp.where(lo, f[0:C, LANES:2 * LANES], f[C:2 * C, LANES:2 * LANES])
                                   for f in (f0, f1)]).astype(bf16)
            qd = qb[t].astype(f32) * eg[idx]
            qp_ref[0, d, sl, :] = (qd - cat([p_[:, LANES:2 * LANES] for p_ in qo[idx]])).astype(bf16)
            op_ref[0, d, sl, :] = cat([p_[:, 0:LANES] for p_ in qo[idx]]).astype(bf16)
            gt_ref[0, d, ci] = jnp.exp(glast[idx])
        return carry

    lax.fori_loop(0, nc // group, body, 0)


def _dn_intra(dq, dk, dv, gb, p, tb, group=4):
    B, L, _ = dq.shape
    nc = tb // DN_CHUNK
    nchunk = L // DN_CHUNK
    tok = lambda w: pl.BlockSpec((1, tb, w), lambda b, i: (b, i, 0))
    full = lambda a: pl.BlockSpec(a.shape, lambda b, i: (0,) * a.ndim)
    consts = [p['tri'], p['esel']]
    return pl.pallas_call(
        functools.partial(_dn_intra_kernel, nc=nc, group=group),
        grid=(B, L // tb),
        in_specs=[tok(DN_WIDTH), tok(DN_WIDTH), tok(DN_WIDTH), tok(LANES)] + [full(a) for a in consts],
        out_specs=[pl.BlockSpec((1, 2, nc, DN_CHUNK, DN_WIDTH), lambda b, i: (b, 0, i, 0, 0)),
                   pl.BlockSpec((1, 2, nc, DN_CHUNK, DN_WIDTH), lambda b, i: (b, 0, i, 0, 0)),
                   pl.BlockSpec((1, 2, tb, DN_WIDTH), lambda b, i: (b, 0, i, 0)),
                   pl.BlockSpec((1, 2, tb, DN_WIDTH), lambda b, i: (b, 0, i, 0)),
                   pl.BlockSpec((1, 2, nc, 1, DN_WIDTH), lambda b, i: (b, 0, i, 0, 0))],
        out_shape=[jax.ShapeDtypeStruct((B, 2, nchunk, DN_CHUNK, DN_WIDTH), bf16),
                   jax.ShapeDtypeStruct((B, 2, nchunk, DN_CHUNK, DN_WIDTH), bf16),
                   jax.ShapeDtypeStruct((B, 2, L, DN_WIDTH), bf16),
                   jax.ShapeDtypeStruct((B, 2, L, DN_WIDTH), bf16),
                   jax.ShapeDtypeStruct((B, 2, nchunk, 1, DN_WIDTH), f32)],
        scratch_shapes=[pltpu.VMEM((tb, 4 * DN_WIDTH), f32)],
        compiler_params=pltpu.CompilerParams(dimension_semantics=("parallel", "parallel"),
                                             vmem_limit_bytes=VMEM_LIMIT),
        name="dn_intra",
    )(dq, dk, dv, gb, *consts)


def _dn_scan_kernel(mf_ref, nf_ref, qf_ref, of_ref, gf_ref, mb_ref, nb_ref, qb_ref, ob_ref, gbk_ref,
                    yf_ref, yb_ref, s_ref, *, nc, nbat):
    C = DN_CHUNK

    @pl.when(pl.program_id(1) == 0)
    def _():
        s_ref[...] = jnp.zeros_like(s_ref)

    lo = lax.broadcasted_iota(jnp.int32, (C, LANES), 1) < HEAD_DIM
    dirs = ((mf_ref, nf_ref, qf_ref, of_ref, gf_ref, yf_ref), (mb_ref, nb_ref, qb_ref, ob_ref, gbk_ref, yb_ref))

    def step(j, carry):
        cis = (j, nc - 1 - j)
        sls = [pl.ds(pl.multiple_of(ci * C, C), C) for ci in cis]
        seqs = [(b, d) for b in range(nbat) for d in range(2)]
        inst = [(b, d, h) for b, d in seqs for h in range(2)]
        hs = lambda a, h: a[:, h * LANES:(h + 1) * LANES]
        m = {(b, d): dirs[d][0][b, 0, cis[d]] for b, d in seqs}
        q = {(b, d): dirs[d][2][b, 0, sls[d], :] for b, d in seqs}
        s = [s_ref[b, d, h] for b, d, h in inst]
        lhs = [jnp.concatenate([_bd2(hs(m[b, d], h), lo), hs(q[b, d], h)], axis=0) for b, d, h in inst]
        rr = [_dot(a, b_.astype(bf16)) for a, b_ in zip(lhs, s)]
        for idx, (b, d, h) in enumerate(inst):
            gt = dirs[d][4][b, 0, cis[d]]
            nn = dirs[d][1][b, 0, cis[d]]
            s_ref[b, d, h] = s[idx] * hs(gt, h) + rr[idx][0:2 * C] + _bd2(hs(nn, h), lo).astype(f32)
        for k, (b, d) in enumerate(seqs):
            y = jnp.concatenate([rr[2 * k][2 * C:3 * C], rr[2 * k + 1][2 * C:3 * C]], axis=1)
            dirs[d][5][b, sls[d], :] = (y + dirs[d][3][b, 0, sls[d], :].astype(f32)).astype(bf16)
        return carry

    lax.fori_loop(0, nc, step, 0)


def _dn_scan(m, n, qp, op, gt, tb, nbat):
    B, _, nchunk, C, W = m.shape
    L = nchunk * C
    nc = tb // C
    ns = L // tb
    assert B % nbat == 0
    fwd5 = lambda last: pl.BlockSpec((nbat, 1, nc, last, W), lambda b, i: (b, 0, i, 0, 0))
    bwd5 = lambda last: pl.BlockSpec((nbat, 1, nc, last, W), lambda b, i: (b, 1, ns - 1 - i, 0, 0))
    fwd4 = pl.BlockSpec((nbat, 1, tb, W), lambda b, i: (b, 0, i, 0))
    bwd4 = pl.BlockSpec((nbat, 1, tb, W), lambda b, i: (b, 1, ns - 1 - i, 0))
    return pl.pallas_call(
        functools.partial(_dn_scan_kernel, nc=nc, nbat=nbat),
        grid=(B // nbat, ns),
        in_specs=[fwd5(C), fwd5(C), fwd4, fwd4, fwd5(1), bwd5(C), bwd5(C), bwd4, bwd4, bwd5(1)],
        out_specs=[pl.BlockSpec((nbat, tb, W), lambda b, i: (b, i, 0)),
                   pl.BlockSpec((nbat, tb, W), lambda b, i: (b, ns - 1 - i, 0))],
        out_shape=[jax.ShapeDtypeStruct((B, L, W), bf16), jax.ShapeDtypeStruct((B, L, W), bf16)],
        scratch_shapes=[pltpu.VMEM((nbat, 2, 2, 2 * C, LANES), f32)],
        compiler_params=pltpu.CompilerParams(dimension_semantics=("parallel", "arbitrary"),
                                             vmem_limit_bytes=VMEM_LIMIT),
        name="dn_scan",
    )(m, n, qp, op, gt, m, n, qp, op, gt)


def _outproj_kernel(xa_ref, ya_ref, ym_ref, yf_ref, yb_ref, dz_ref, lg_ref, lb_ref,
                    ng_ref, wa_ref, wd_ref, wm_ref, ones_ref, o_ref):
    o = yf_ref[0].astype(f32) + yb_ref[0].astype(f32)
    ms = _dot((o * o).astype(bf16), ones_ref[...]) * (1.0 / HEAD_DIM)
    yd = o * lax.rsqrt(ms + RMS_EPS) * ng_ref[...] * _silu(dz_ref[0].astype(f32))
    y = _dot(ya_ref[0], wa_ref[...]) + _dot(yd.astype(bf16), wd_ref[...]) + _dot(ym_ref[0], wm_ref[...])
    o_ref[0] = _layer_norm(xa_ref[0] + y, lg_ref[...], lb_ref[...])


def _outproj(xa, ya, ym, yf, yb, dz, p, tm):
    B, L, _ = xa.shape
    tok = lambda w: pl.BlockSpec((1, tm, w), lambda b, i: (b, i, 0))
    full = lambda a: pl.BlockSpec(a.shape, lambda b, i: (0,) * a.ndim)
    consts = [p['ln_g'], p['ln_b'], p['norm_g'], p['wo_attn'], p['wo_dn'], p['wo_mem'], p['ones_bd']]
    return pl.pallas_call(
        _outproj_kernel,
        grid=(B, L // tm),
        in_specs=[tok(D_MODEL), tok(ATTN_WIDTH), tok(MEM_WIDTH), tok(DN_WIDTH), tok(DN_WIDTH), tok(DN_WIDTH)]
                 + [full(a) for a in consts],
        out_specs=tok(D_MODEL),
        out_shape=jax.ShapeDtypeStruct((B, L, D_MODEL), f32),
        compiler_params=pltpu.CompilerParams(dimension_semantics=("parallel", "parallel"),
                                             vmem_limit_bytes=VMEM_LIMIT),
        name="outproj",
    )(xa, ya, ym, yf, yb, dz, *consts)


def _t5_bucket(rel):
    nb = REL_BUCKETS // 2
    max_exact = nb // 2
    n = jnp.abs(rel)
    large = max_exact + (jnp.log(jnp.maximum(n, 1).astype(f32) / max_exact)
                         / math.log(REL_MAX_DIST / max_exact) * (nb - max_exact)).astype(jnp.int32)
    large = jnp.minimum(large, nb - 1)
    return jnp.where(rel > 0, nb, 0) + jnp.where(n < max_exact, n, large)


def _prepare(ln_in_g, ln_in_b, rel_bias, w_in, attn_sink, dn_conv, dn_A_log, dn_dt_bias, dn_norm_g,
             w_mem_kv, w_out, ln_g, ln_b):
    offs = np.cumsum([0, ATTN_WIDTH, ATTN_KV_WIDTH, ATTN_KV_WIDTH, ATTN_WIDTH, DN_WIDTH, DN_WIDTH, DN_WIDTH,
                      DN_WIDTH, 2 * DN_HEADS, 2 * DN_HEADS, MEM_WIDTH, MEM_WIDTH])
    (w_aq, w_ak, w_av, w_az, w_dq, w_dk, w_dv, w_dz, w_da, w_db, w_mq, w_mz) = [
        w_in[:, int(offs[j]):int(offs[j + 1])] for j in range(12)]
    order = [kv * ATTN_GROUP + c for c in range(ATTN_GROUP) for kv in range(ATTN_KV_HEADS)]
    head_cols = lambda w: jnp.concatenate([w[:, h * HEAD_DIM:(h + 1) * HEAD_DIM] for h in order], axis=1)
    scale = HEAD_DIM ** -0.5 * LOG2E
    p = {}
    p['w_attn'] = jnp.concatenate([head_cols(w_aq) * scale, w_ak, w_av, head_cols(w_az)], axis=1).astype(bf16)
    p['w_mem'] = jnp.concatenate([w_mq * scale, w_mz], axis=1).astype(bf16)
    p['w_dn'] = jnp.concatenate([w_dq, w_dk, w_dv, w_dz], axis=1).astype(bf16)
    p['w_ab'] = jnp.pad(jnp.concatenate([w_da, w_db], axis=1), ((0, 0), (0, LANES - 4 * DN_HEADS))).astype(bf16)
    p['conv'] = jnp.pad(dn_conv, ((0, SUBLANES - DN_CONV), (0, 0)))
    p['alog'] = jnp.pad(dn_A_log.reshape(1, 2 * DN_HEADS), ((0, 0), (0, LANES - 2 * DN_HEADS)))
    p['dtb'] = jnp.pad(dn_dt_bias.reshape(1, 2 * DN_HEADS), ((0, 0), (0, LANES - 2 * DN_HEADS)))
    p['ln_in_g'] = ln_in_g.reshape(1, D_MODEL)
    p['ln_in_b'] = ln_in_b.reshape(1, D_MODEL)
    p['ln_g'] = ln_g.reshape(1, D_MODEL)
    p['ln_b'] = ln_b.reshape(1, D_MODEL)
    p['norm_g'] = jnp.tile(dn_norm_g.reshape(1, HEAD_DIM), (1, DN_HEADS))
    p['w_memk'] = w_mem_kv[:, 0:MEM_WIDTH].astype(bf16)
    p['w_memvt'] = w_mem_kv[:, MEM_WIDTH:2 * MEM_WIDTH].T.astype(bf16)
    p['wo_attn'] = jnp.concatenate([w_out[h * HEAD_DIM:(h + 1) * HEAD_DIM] for h in order], axis=0).astype(bf16)
    p['wo_dn'] = w_out[ATTN_WIDTH:ATTN_WIDTH + DN_WIDTH].astype(bf16)
    p['wo_mem'] = w_out[ATTN_WIDTH + DN_WIDTH:].astype(bf16)
    p['sink'] = jnp.stack([attn_sink[h] for h in order]) * LOG2E

    blk = np.arange(DN_WIDTH) // HEAD_DIM
    bd = (blk[:, None] == blk[None, :])
    p['ones_bd'] = jnp.asarray(bd, bf16)
    p['tri'] = jnp.asarray(np.tril(np.ones((DN_CHUNK, DN_CHUNK))), bf16)
    esel = np.zeros((LANES, 4 * DN_WIDTH), np.float32)
    for c in range(4 * DN_HEADS):
        esel[c, c * HEAD_DIM:(c + 1) * HEAD_DIM] = 1.0
    p['esel'] = jnp.asarray(esel, bf16)

    t = jnp.arange(BLOCK)[None, :]
    s = jnp.arange(3 * BLOCK)[:, None]
    rel = s - BLOCK - t
    onehot = jax.nn.one_hot(_t5_bucket(rel), REL_BUCKETS, dtype=f32)
    rb = jnp.stack([rel_bias[:, h] for h in order], axis=0)
    bias = jnp.einsum('stk,hk->sht', onehot, rb, precision=lax.Precision.HIGHEST)
    bias = jnp.where((jnp.abs(rel) <= WINDOW)[:, None, :], bias * LOG2E, NEG)
    bias = bias.reshape(3 * BLOCK, ATTN_HEADS * BLOCK)
    no_prev = jnp.where(s < BLOCK, NEG, 0.0)
    no_next = jnp.where(s >= 2 * BLOCK, NEG, 0.0)
    p['bias'] = jnp.stack([bias, bias + no_next, bias + no_prev])
    return p


SCAN_SEQS = 4


def _trunk(x, mem, p, tm_in=1024, tm=1024, tb_intra=1024, tb_scan=1024):
    B, L, _ = x.shape
    tm_in = min(tm_in, L)
    tm = min(tm, L)
    tb_intra = min(tb_intra, L)
    tb_scan = min(tb_scan, L)
    nbat = math.gcd(B, SCAN_SEQS)
    aq, ak, av, az, mq, mz, dq, dk, dv, dz, gb, xa = _inproj(x, p, tm_in)
    mk, mvt = _memkv(mem, p)
    ya, ym = _attn(aq, ak, av, az, mq, mz, mk, mvt, p)
    m, n, qp, op, gt = _dn_intra(dq, dk, dv, gb, p, tb_intra)
    yf, yb = _dn_scan(m, n, qp, op, gt, tb_scan, nbat)
    return _outproj(xa, ya, ym, yf, yb, dz, p, tm)


def kernel(x_prompt, x_sample, mem_prompt, mem_sample, ln_in_g, ln_in_b, rel_bias, w_in, attn_sink, dn_conv,
           dn_A_log, dn_dt_bias, dn_norm_g, w_mem_kv, w_out, ln_g, ln_b):
    p = _prepare(ln_in_g, ln_in_b, rel_bias, w_in[0], attn_sink[0], dn_conv[0], dn_A_log[0], dn_dt_bias[0],
                 dn_norm_g[0], w_mem_kv[0], w_out[0], ln_g[0], ln_b[0])
    return (_trunk(x_prompt, mem_prompt, p), _trunk(x_sample, mem_sample, p))
```
